```python
import math
import jax
import jax.numpy as jnp
from jax import lax
import numpy as np

D_MODEL = 2048
BATCH = 8
SEQ = 2048
DEPTH = 2
DEC_BATCH = 16
DEC_SEQ = 64
PAST_LEN = 4096

CHUNK = 64
QBLOCK = 128
HEAD_DIM = 128
N_HEADS_A = 8
N_MAPS_A = 2 * N_HEADS_A
DA_QK = HEAD_DIM // 2
N_HEADS_B = 8
WIDTH_A = N_HEADS_A * HEAD_DIM
WIDTH_B = N_HEADS_B * HEAD_DIM
N_HEADS_C = 16
Q_LORA = 512
KV_LORA = 512
N_IDX_HEADS = 16
D_IDX = 128
TOPK_MAX = 256
N_BUCKETS = 32
MAX_DISTANCE = 128
N_BIAS_HEADS = max(N_MAPS_A, N_HEADS_C)
N_EXPERTS = 16
N_GROUPS = 4
EXPERTS_PER_GROUP = N_EXPERTS // N_GROUPS
TOP_K_EXPERTS = 2
D_EXPERT = 512
N_EVEN = (DEPTH + 1) // 2
N_ODD = DEPTH // 2
D_IN_EVEN = 2 * N_MAPS_A * DA_QK + WIDTH_A + 3 * WIDTH_B
D_IN_ODD = Q_LORA + KV_LORA + D_IDX + N_IDX_HEADS
EPS = 1e-6

kernel_name = "hybrid_stream_encoder_step"


def rms_norm(x, g):
    xf = x.astype(jnp.float32)
    y = xf * lax.rsqrt(jnp.mean(xf * xf, axis=-1, keepdims=True) + EPS)
    return (y * g.astype(jnp.float32)).astype(x.dtype)


def rel_bucket(q_pos, k_pos):
    rel = k_pos - q_pos
    half = N_BUCKETS // 2
    max_exact = half // 2
    n = jnp.abs(rel)
    nf = jnp.maximum(n, 1).astype(jnp.float32)
    large = max_exact + (jnp.log(nf / max_exact) / math.log(MAX_DISTANCE / max_exact)
                         * (half - max_exact)).astype(jnp.int32)
    large = jnp.minimum(large, half - 1)
    return jnp.where(rel > 0, half, 0) + jnp.where(n < max_exact, n, large)


def rel_bias(table, q_pos, k_pos):
    return table[rel_bucket(q_pos, k_pos)].astype(jnp.float32)


def chunk_visible(q_pos, k_pos):
    return (k_pos[None, :] // CHUNK) <= (q_pos[:, None] // CHUNK)


def sweep_queries(fn, q_arrays, q_pos):
    n_q = q_pos.shape[0]
    if n_q <= QBLOCK or n_q % QBLOCK:
        return fn(q_arrays, q_pos)
    nb = n_q // QBLOCK
    blocks = tuple(jnp.moveaxis(a.reshape(a.shape[0], nb, QBLOCK, *a.shape[2:]), 1, 0) for a in q_arrays)
    out = lax.map(lambda args: fn(*args), (blocks, q_pos.reshape(nb, QBLOCK)))
    out = jnp.moveaxis(out, 0, 1)
    return out.reshape(out.shape[0], n_q, *out.shape[3:])


def diff_attention(q, k, v, q_pos, k_pos, bias_table, lam):
    def block(qs, qp):
        (qb,) = qs
        s = jnp.einsum('bqmd,bkmd->bmqk', qb, k).astype(jnp.float32) * DA_QK ** -0.5
        bias = jnp.moveaxis(rel_bias(bias_table, qp[:, None], k_pos[None, :]), -1, 0)[:N_MAPS_A]
        s = jnp.where(chunk_visible(qp, k_pos)[None, None], s + bias[None], -jnp.inf)
        p = jax.nn.softmax(s, axis=-1)
        p = p.reshape(p.shape[0], N_HEADS_A, 2, *p.shape[2:])
        w = p[:, :, 0] - lam * p[:, :, 1]
        return jnp.einsum('bhqk,bkhd->bqhd', w.astype(v.dtype), v)
    return sweep_queries(block, (q,), q_pos)


def stick_breaking(q, k, v, q_pos, k_pos):
    def block(qs, qp):
        (qb,) = qs
        z = jnp.einsum('bqhd,bkhd->bhqk', qb, k).astype(jnp.float32) * HEAD_DIM ** -0.5
        earlier = (k_pos[None, :] < qp[:, None])[None, None]
        log_beta = jax.nn.log_sigmoid(z)
        log_keep = jnp.where(earlier, jax.nn.log_sigmoid(-z), 0.0)
        later = lax.cumsum(log_keep, axis=3, reverse=True) - log_keep
        a = jnp.where(earlier, jnp.exp(log_beta + later), 0.0)
        return jnp.einsum('bhqk,bkhd->bqhd', a.astype(v.dtype), v)
    return sweep_queries(block, (q,), q_pos)


def dsa_attention(q_lat, q_idx, w_idx, kv_lat, k_idx, q_pos, k_pos, bias_table):
    top_k = min(TOPK_MAX, k_pos.shape[0] // 4)
    def block(qs, qp):
        ql, qi, wi = qs
        dots = jnp.einsum('bqid,bkd->bqik', qi, k_idx).astype(jnp.float32) * D_IDX ** -0.5
        score = jnp.einsum('bqi,bqik->bqk', wi.astype(jnp.float32) * N_IDX_HEADS ** -0.5, jax.nn.relu(dots))
        score = jnp.where(chunk_visible(qp, k_pos)[None], score, -jnp.inf)
        top_score, sel = lax.top_k(score, top_k)
        valid = jnp.isfinite(top_score)
        kv_sel = jax.vmap(lambda kv, ix: kv[ix])(kv_lat, sel)
        bias = rel_bias(bias_table, qp[None, :, None], k_pos[sel])[..., :N_HEADS_C]
        s = jnp.einsum('bqhc,bqkc->bqhk', ql, kv_sel).astype(jnp.float32) * HEAD_DIM ** -0.5
        s = jnp.where(valid[:, :, None, :], s + jnp.moveaxis(bias, -1, 2), -jnp.inf)
        p = jax.nn.softmax(s, axis=-1)
        return jnp.einsum('bqhk,bqkc->bqhc', p.astype(kv_sel.dtype), kv_sel)
    return sweep_queries(block, (q_lat, q_idx, w_idx), q_pos)


def even_mixer(h, q_pos, k_pos, past, w_in, lam_q1, lam_k1, lam_q2, lam_k2, subln_g, w_out, bias_table, lam_init):
    bsz, n, _ = h.shape
    qk_w = N_MAPS_A * DA_QK
    cuts = np.cumsum([qk_w, qk_w, WIDTH_A, WIDTH_B, WIDTH_B]).tolist()
    qa, ka, va, qb, kb, vb = jnp.split(h @ w_in, cuts, axis=-1)
    qa = qa.reshape(bsz, n, N_MAPS_A, DA_QK)
    ka = ka.reshape(bsz, n, N_MAPS_A, DA_QK)
    va = va.reshape(bsz, n, N_HEADS_A, HEAD_DIM)
    qb = qb.reshape(bsz, n, N_HEADS_B, HEAD_DIM)
    kb = kb.reshape(bsz, n, N_HEADS_B, HEAD_DIM)
    vb = vb.reshape(bsz, n, N_HEADS_B, HEAD_DIM)
    new_rows = (ka, va, kb, vb)
    if past is not None:
        ka, va, kb, vb = (jnp.concatenate([old, new], axis=1) for old, new in zip(past, new_rows))
    f32 = jnp.float32
    lam = (jnp.exp(jnp.sum(lam_q1.astype(f32) * lam_k1.astype(f32)))
           - jnp.exp(jnp.sum(lam_q2.astype(f32) * lam_k2.astype(f32))) + lam_init)
    o_a = diff_attention(qa, ka, va, q_pos, k_pos, bias_table, lam)
    o_a = (rms_norm(o_a, subln_g) * (1.0 - lam_init)).reshape(bsz, n, WIDTH_A)
    o_b = stick_breaking(qb, kb, vb, q_pos, k_pos).reshape(bsz, n, WIDTH_B)
    return jnp.concatenate([o_a, o_b], axis=-1) @ w_out, new_rows


def odd_mixer(h, q_pos, k_pos, past, w_in, g_q, g_kv, w_uq, w_qidx, w_uk, w_uv, w_out, bias_table):
    bsz, n, _ = h.shape
    c_q, c_kv, k_i, w_i = jnp.split(h @ w_in, [Q_LORA, Q_LORA + KV_LORA, Q_LORA + KV_LORA + D_IDX], axis=-1)
    c_q = rms_norm(c_q, g_q)
    kv_lat = rms_norm(c_kv, g_kv)
    q = (c_q @ w_uq).reshape(bsz, n, N_HEADS_C, HEAD_DIM)
    q_lat = jnp.einsum('bnhd,chd->bnhc', q, w_uk)
    q_i = (c_q @ w_qidx).reshape(bsz, n, N_IDX_HEADS, D_IDX)
    new_rows = (kv_lat, k_i)
    kv_all, ki_all = kv_lat, k_i
    if past is not None:
        kv_all = jnp.concatenate([past[0], kv_lat], axis=1)
        ki_all = jnp.concatenate([past[1], k_i], axis=1)
    o_lat = dsa_attention(q_lat, q_i, w_i, kv_all, ki_all, q_pos, k_pos, bias_table)
    o = jnp.einsum('bnhc,chd->bnhd', o_lat, w_uv).reshape(bsz, n, N_HEADS_C * HEAD_DIM)
    return o @ w_out, new_rows


def moe(h, w_router, router_bias, w_gate, w_up, w_down):
    affinity = jax.nn.sigmoid((h @ w_router).astype(jnp.float32))
    grouped = (affinity + router_bias.astype(jnp.float32)).reshape(-1, N_GROUPS, EXPERTS_PER_GROUP)
    group_score = lax.top_k(grouped, TOP_K_EXPERTS)[0].sum(-1)
    g_best = jnp.argmax(group_score, axis=-1)
    in_group = jnp.take_along_axis(grouped, g_best[:, None, None], axis=1)[:, 0]
    _, local = lax.top_k(in_group, TOP_K_EXPERTS)
    expert_ids = g_best[:, None] * EXPERTS_PER_GROUP + local
    w_sel = jnp.take_along_axis(affinity, expert_ids, axis=1)
    w_sel = w_sel / jnp.sum(w_sel, axis=-1, keepdims=True)
    gates = jnp.einsum('tk,tke->te', w_sel, jax.nn.one_hot(expert_ids, N_EXPERTS, dtype=jnp.float32)).astype(h.dtype)
    y = jnp.zeros_like(h)
    for e in range(N_EXPERTS):
        he = jax.nn.silu(h @ w_gate[e]) * (h @ w_up[e])
        y = y + gates[:, e:e + 1] * (he @ w_down[e])
    return y


def trunk(x, c, q_pos, k_pos, past, p):
    bsz, n, d = x.shape
    c_act = jax.nn.silu(c)
    even_rows, odd_rows = [], []
    for l in range(DEPTH):
        mod = (c_act @ p['w_ada'][l] + p['b_ada'][l]).reshape(bsz, 6, d)
        sh1, sc1, g1, sh2, sc2, g2 = (mod[:, i, None, :] for i in range(6))
        h = rms_norm(x, p['norm_mix_g'][l]) * (1 + sc1) + sh1
        if l % 2 == 0:
            i = l // 2
            past_i = None if past is None else tuple(a[i] for a in past[:4])
            out, rows = even_mixer(h, q_pos, k_pos, past_i, p['w_in_even'][i], p['lam_q1'][i], p['lam_k1'][i],
                                   p['lam_q2'][i], p['lam_k2'][i], p['subln_g'][i], p['w_out_even'][i],
                                   p['rel_bias_table'], 0.8 - 0.6 * math.exp(-0.3 * l))
            even_rows.append(rows)
        else:
            j = l // 2
            past_j = None if past is None else tuple(a[j] for a in past[4:])
            out, rows = odd_mixer(h, q_pos, k_pos, past_j, p['w_in_odd'][j], p['g_q'][j], p['g_kv'][j],
                                  p['w_uq'][j], p['w_qidx'][j], p['w_uk'][j], p['w_uv'][j], p['w_out_odd'][j],
                                  p['rel_bias_table'])
            odd_rows.append(rows)
        x = x + g1 * out
        h = rms_norm(x, p['norm_ffn_g'][l]) * (1 + sc2) + sh2
        ff = moe(h.reshape(-1, d), p['w_router'], p['router_bias'], p['w_gate'][l], p['w_up'][l], p['w_down'][l])
        x = x + g2 * ff.reshape(bsz, n, d)
    y = rms_norm(x, p['final_norm_g'])
    ev = tuple(jnp.stack([r[m] for r in even_rows]) for m in range(4))
    od = tuple(jnp.stack([r[m] for r in odd_rows]) for m in range(2))
    return y, ev, od


def setup_inputs(seed: int = 0) -> dict:
    key = jax.random.key(seed)
    keys = iter(jax.random.split(key, 40))

    def nrm(shape, scale):
        return jax.random.normal(next(keys), shape, jnp.float32) * scale

    D = D_MODEL
    return {
        'x_prompt': nrm((BATCH, SEQ, D), 1.0),
        'x_sample': nrm((DEC_BATCH, DEC_SEQ, D), 1.0),
        'cache_a_k': nrm((N_EVEN, DEC_BATCH, PAST_LEN, N_MAPS_A, DA_QK), 1.0),
        'cache_a_v': nrm((N_EVEN, DEC_BATCH, PAST_LEN, N_HEADS_A, HEAD_DIM), 1.0),
        'cache_b_k': nrm((N_EVEN, DEC_BATCH, PAST_LEN, N_HEADS_B, HEAD_DIM), 1.0),
        'cache_b_v': nrm((N_EVEN, DEC_BATCH, PAST_LEN, N_HEADS_B, HEAD_DIM), 1.0),
        'cache_c_kv': nrm((N_ODD, DEC_BATCH, PAST_LEN, KV_LORA), 1.0),
        'cache_c_idx': nrm((N_ODD, DEC_BATCH, PAST_LEN, D_IDX), 1.0),
        'c_prompt': nrm((BATCH, D), 1.0),
        'c_sample': nrm((DEC_BATCH, D), 1.0),
        'rel_bias_table': nrm((N_BUCKETS, N_BIAS_HEADS), 0.5),
        'norm_mix_g': 1.0 + nrm((DEPTH, D), 0.05),
        'norm_ffn_g': 1.0 + nrm((DEPTH, D), 0.05),
        'final_norm_g': 1.0 + nrm((D,), 0.05),
        'w_ada': nrm((DEPTH, D, 6 * D), 0.5 * D ** -0.5),
        'b_ada': nrm((DEPTH, 6 * D), 0.02),
        'w_in_even': nrm((N_EVEN, D, D_IN_EVEN), D ** -0.5),
        'lam_q1': nrm((N_EVEN, DA_QK), 0.1),
        'lam_k1': nrm((N_EVEN, DA_QK), 0.1),
        'lam_q2': nrm((N_EVEN, DA_QK), 0.1),
        'lam_k2': nrm((N_EVEN, DA_QK), 0.1),
        'subln_g': 1.0 + nrm((N_EVEN, HEAD_DIM), 0.05),
        'w_out_even': nrm((N_EVEN, WIDTH_A + WIDTH_B, D), (WIDTH_A + WIDTH_B) ** -0.5),
        'w_in_odd': nrm((N_ODD, D, D_IN_ODD), D ** -0.5),
        'g_q': 1.0 + nrm((N_ODD, Q_LORA), 0.05),
        'g_kv': 1.0 + nrm((N_ODD, KV_LORA), 0.05),
        'w_uq': nrm((N_ODD, Q_LORA, N_HEADS_C * HEAD_DIM), Q_LORA ** -0.5),
        'w_qidx': nrm((N_ODD, Q_LORA, N_IDX_HEADS * D_IDX), Q_LORA ** -0.5),
        'w_uk': nrm((N_ODD, KV_LORA, N_HEADS_C, HEAD_DIM), KV_LORA ** -0.5),
        'w_uv': nrm((N_ODD, KV_LORA, N_HEADS_C, HEAD_DIM), KV_LORA ** -0.5),
        'w_out_odd': nrm((N_ODD, N_HEADS_C * HEAD_DIM, D), (N_HEADS_C * HEAD_DIM) ** -0.5),
        'w_router': nrm((D, N_EXPERTS), D ** -0.5),
        'router_bias': nrm((N_EXPERTS,), 0.01),
        'w_gate': nrm((DEPTH, N_EXPERTS, D, D_EXPERT), D ** -0.5),
        'w_up': nrm((DEPTH, N_EXPERTS, D, D_EXPERT), D ** -0.5),
        'w_down': nrm((DEPTH, N_EXPERTS, D_EXPERT, D), D_EXPERT ** -0.5),
    }


def reference(x_prompt, x_sample, cache_a_k, cache_a_v, cache_b_k, cache_b_v, cache_c_kv, cache_c_idx,
              c_prompt, c_sample, rel_bias_table, norm_mix_g, norm_ffn_g, final_norm_g, w_ada, b_ada,
              w_in_even, lam_q1, lam_k1, lam_q2, lam_k2, subln_g, w_out_even, w_in_odd, g_q, g_kv,
              w_uq, w_qidx, w_uk, w_uv, w_out_odd, w_router, router_bias, w_gate, w_up, w_down):
    p = dict(rel_bias_table=rel_bias_table, norm_mix_g=norm_mix_g, norm_ffn_g=norm_ffn_g,
             final_norm_g=final_norm_g, w_ada=w_ada, b_ada=b_ada, w_in_even=w_in_even,
             lam_q1=lam_q1, lam_k1=lam_k1, lam_q2=lam_q2, lam_k2=lam_k2, subln_g=subln_g,
             w_out_even=w_out_even, w_in_odd=w_in_odd, g_q=g_q, g_kv=g_kv, w_uq=w_uq, w_qidx=w_qidx,
             w_uk=w_uk, w_uv=w_uv, w_out_odd=w_out_odd, w_router=w_router, router_bias=router_bias,
             w_gate=w_gate, w_up=w_up, w_down=w_down)
    pos_p = jnp.arange(x_prompt.shape[1], dtype=jnp.int32)
    y_prompt, ev_p, od_p = trunk(x_prompt, c_prompt, pos_p, pos_p, None, p)
    n_new = x_sample.shape[1]
    k_pos_s = jnp.arange(cache_a_k.shape[2] + n_new, dtype=jnp.int32)
    q_pos_s = k_pos_s[k_pos_s.shape[0] - n_new:]
    past = (cache_a_k, cache_a_v, cache_b_k, cache_b_v, cache_c_kv, cache_c_idx)
    y_sample, ev_s, od_s = trunk(x_sample, c_sample, q_pos_s, k_pos_s, past, p)
    a_k_p, a_v_p, b_k_p, b_v_p = ev_p
    c_kv_p, c_idx_p = od_p
    a_k_s, a_v_s, b_k_s, b_v_s = ev_s
    c_kv_s, c_idx_s = od_s
    return (y_prompt, y_sample, a_k_p, a_v_p, b_k_p, b_v_p, c_kv_p, c_idx_p,
            a_k_s, a_v_s, b_k_s, b_v_s, c_kv_s, c_idx_s)
```

```python
import functools
import math

import jax
import jax.numpy as jnp
import numpy as np
from jax import lax
from jax.experimental import pallas as pl
from jax.experimental.pallas import tpu as pltpu

F32 = jnp.float32
MXU = jnp.bfloat16

CHUNK = 64
HEAD_DIM = 128
N_HEADS_A = 8
N_MAPS_A = 16
DA_QK = 64
N_HEADS_B = 8
N_HEADS_C = 16
Q_LORA = 512
KV_LORA = 512
N_IDX_HEADS = 16
D_IDX = 128
TOPK_MAX = 256
N_BUCKETS = 32
MAX_DISTANCE = 128
N_EXPERTS = 16
N_GROUPS = 4
EXPERTS_PER_GROUP = 4
EPS = 1e-6
LANES = 128
TK = 256
VMEM_LIMIT = 56 * 1024 * 1024
NEG_INF = float("-inf")


def _cparams(sem):
    return pltpu.CompilerParams(dimension_semantics=sem, vmem_limit_bytes=VMEM_LIMIT)


def _dot(a, b):
    return jnp.dot(a, b, preferred_element_type=F32)


def _dot_nt(a, b):
    return lax.dot_general(a, b, (((1,), (1,)), ((), ())), preferred_element_type=F32)


def _mm_kernel(*refs, has_bias, has_res, tm):
    it = iter(refs)
    a_ref, b_ref = next(it), next(it)
    bias_ref = next(it) if has_bias else None
    gate_ref, res_ref = (next(it), next(it)) if has_res else (None, None)
    o_ref, bsc = next(it), next(it)

    @pl.when(pl.program_id(1) == 0)
    def _():
        bsc[...] = b_ref[...].reshape(bsc.shape).astype(bsc.dtype)

    a = a_ref[...]
    a = a.reshape(a.shape[-2:]).astype(MXU)
    acc = _dot(a, bsc[...])
    if has_bias:
        acc = acc + bias_ref[...]
    if has_res:
        tn = acc.shape[-1]
        acc = (acc.reshape(tm // CHUNK, CHUNK, tn) * gate_ref[...]).reshape(tm, tn)
        acc = acc + res_ref[...]
    o_ref[...] = acc.reshape(o_ref.shape).astype(o_ref.dtype)


def _mm(a, b, *, n_out, tm, tn, bias=None, gate=None, res=None, out_dtype=F32):
    m, k = a.shape
    assert m % tm == 0 and n_out % tn == 0 and b.shape[0] == k
    in_specs = [pl.BlockSpec((tm, k), lambda j, i: (i, 0)),
                pl.BlockSpec((k, tn), lambda j, i: (0, j))]
    args = [a, b]
    if bias is not None:
        in_specs.append(pl.BlockSpec((1, tn), lambda j, i: (0, j)))
        args.append(bias.reshape(1, n_out))
    if res is not None:
        in_specs.append(pl.BlockSpec((tm // CHUNK, 1, tn), lambda j, i: (i, 0, j)))
        in_specs.append(pl.BlockSpec((tm, tn), lambda j, i: (i, j)))
        args += [gate, res]
    return pl.pallas_call(
        functools.partial(_mm_kernel, has_bias=bias is not None, has_res=res is not None, tm=tm),
        grid=(n_out // tn, m // tm), in_specs=in_specs,
        out_specs=pl.BlockSpec((tm, tn), lambda j, i: (i, j)),
        out_shape=jax.ShapeDtypeStruct((m, n_out), out_dtype),
        scratch_shapes=[pltpu.VMEM((k, tn), MXU)],
        compiler_params=_cparams(("arbitrary", "arbitrary")),
    )(*args)


def _mm_heads_out(a, b_h, *, tm):
    m = a.shape[0]
    nh, ka, n = b_h.shape
    return pl.pallas_call(
        functools.partial(_mm_kernel, has_bias=False, has_res=False, tm=tm),
        grid=(nh, m // tm),
        in_specs=[pl.BlockSpec((tm, ka), lambda h, i: (i, h)),
                  pl.BlockSpec((1, ka, n), lambda h, i: (h, 0, 0))],
        out_specs=pl.BlockSpec((1, tm, n), lambda h, i: (h, i, 0)),
        out_shape=jax.ShapeDtypeStruct((nh, m, n), MXU),
        scratch_shapes=[pltpu.VMEM((ka, n), MXU)],
        compiler_params=_cparams(("arbitrary", "arbitrary")),
    )(a, b_h)


def _mm_heads_in(a_h, b_h, *, tm):
    nh, m, ka = a_h.shape
    n = b_h.shape[2]
    return pl.pallas_call(
        functools.partial(_mm_kernel, has_bias=False, has_res=False, tm=tm),
        grid=(nh, m // tm),
        in_specs=[pl.BlockSpec((1, tm, ka), lambda h, i: (h, i, 0)),
                  pl.BlockSpec((1, ka, n), lambda h, i: (h, 0, 0))],
        out_specs=pl.BlockSpec((tm, n), lambda h, i: (i, h)),
        out_shape=jax.ShapeDtypeStruct((m, nh * n), MXU),
        scratch_shapes=[pltpu.VMEM((ka, n), MXU)],
        compiler_params=_cparams(("arbitrary", "arbitrary")),
    )(a_h, b_h)


def _rms(x, g):
    return x * lax.rsqrt(jnp.mean(x * x, axis=-1, keepdims=True) + EPS) * g


def _modulate(y, sc_ref, sh_ref, tm):
    d = y.shape[-1]
    y3 = y.reshape(tm // CHUNK, CHUNK, d)
    y3 = y3 * (1.0 + sc_ref[...]) + sh_ref[...]
    return y3.reshape(tm, d)


def _norm_kernel(*refs, has_mod, tm):
    if has_mod:
        x_ref, g_ref, sc_ref, sh_ref, o_ref = refs
    else:
        x_ref, g_ref, o_ref = refs
    y = _rms(x_ref[...], g_ref[...])
    if has_mod:
        y = _modulate(y, sc_ref, sh_ref, tm)
    o_ref[...] = y.astype(o_ref.dtype)


def _norm(x, g, *, tm, width=None, col_block=0, sc=None, sh=None, out_dtype=F32):
    width = x.shape[1] if width is None else width
    rows = x.shape[0]
    assert rows % tm == 0
    has_mod = sc is not None
    in_specs = [pl.BlockSpec((tm, width), lambda i: (i, col_block)),
                pl.BlockSpec((1, width), lambda i: (0, 0))]
    args = [x, g.reshape(1, width)]
    if has_mod:
        in_specs += [pl.BlockSpec((tm // CHUNK, 1, width), lambda i: (i, 0, 0))] * 2
        args += [sc, sh]
    return pl.pallas_call(
        functools.partial(_norm_kernel, has_mod=has_mod, tm=tm),
        grid=(rows // tm,), in_specs=in_specs,
        out_specs=pl.BlockSpec((tm, width), lambda i: (i, 0)),
        out_shape=jax.ShapeDtypeStruct((rows, width), out_dtype),
        compiler_params=_cparams(("arbitrary",)),
    )(*args)


def _split2(x):
    hi = x.astype(MXU)
    lo = (x - hi.astype(F32)).astype(MXU)
    return hi, lo


def _route(logits, rb_ref):
    tm = logits.shape[0]
    aff = [1.0 / (1.0 + jnp.exp(-logits[:, e:e + 1])) for e in range(N_EXPERTS)]
    sc = [aff[e] + rb_ref[e] for e in range(N_EXPERTS)]
    npg = EXPERTS_PER_GROUP
    gscore = []
    for g in range(N_GROUPS):
        v = sc[g * npg:(g + 1) * npg]
        best = None
        for a in range(npg):
            for b in range(a + 1, npg):
                s = v[a] + v[b]
                best = s if best is None else jnp.maximum(best, s)
        gscore.append(best)
    gb = jnp.zeros((tm, 1), jnp.int32)
    gv = gscore[0]
    for g in range(1, N_GROUPS):
        better = gscore[g] > gv
        gb = jnp.where(better, g, gb)
        gv = jnp.where(better, gscore[g], gv)
    u = [sc[j] for j in range(npg)]
    a4 = [aff[j] for j in range(npg)]
    for g in range(1, N_GROUPS):
        pick = gb == g
        u = [jnp.where(pick, sc[g * npg + j], u[j]) for j in range(npg)]
        a4 = [jnp.where(pick, aff[g * npg + j], a4[j]) for j in range(npg)]
    i1 = jnp.zeros((tm, 1), jnp.int32)
    v1 = u[0]
    for j in range(1, npg):
        better = u[j] > v1
        i1 = jnp.where(better, j, i1)
        v1 = jnp.where(better, u[j], v1)
    i2 = jnp.full((tm, 1), -1, jnp.int32)
    v2 = jnp.full((tm, 1), NEG_INF, F32)
    for j in range(npg):
        better = (i1 != j) & ((u[j] > v2) | (i2 < 0))
        i2 = jnp.where(better, j, i2)
        v2 = jnp.where(better, u[j], v2)
    w1 = a4[0]
    w2 = a4[0]
    for j in range(1, npg):
        w1 = jnp.where(i1 == j, a4[j], w1)
        w2 = jnp.where(i2 == j, a4[j], w2)
    tot = w1 + w2
    w1 = w1 / tot
    w2 = w2 / tot
    e1 = gb * npg + i1
    e2 = gb * npg + i2
    lane = lax.broadcasted_iota(jnp.int32, (tm, LANES), 1)
    return jnp.where(lane == e1, w1, 0.0) + jnp.where(lane == e2, w2, 0.0)


def _norm_router_kernel(x_ref, g_ref, sc_ref, sh_ref, wr_ref, rb_ref, h_ref, gates_ref, *, tm):
    h = _modulate(_rms(x_ref[...], g_ref[...]), sc_ref, sh_ref, tm)
    h_ref[...] = h.astype(h_ref.dtype)
    hh, hl = _split2(h)
    wh, wl = _split2(wr_ref[...])
    logits = _dot(hh, wh) + (_dot(hl, wh) + _dot(hh, wl))
    gates_ref[...] = _route(logits, rb_ref)


def _norm_router(x, g, sc, sh, w_router_pad, router_bias, *, tm):
    t, d = x.shape
    return pl.pallas_call(
        functools.partial(_norm_router_kernel, tm=tm),
        grid=(t // tm,),
        in_specs=[pl.BlockSpec((tm, d), lambda i: (i, 0)),
                  pl.BlockSpec((1, d), lambda i: (0, 0)),
                  pl.BlockSpec((tm // CHUNK, 1, d), lambda i: (i, 0, 0)),
                  pl.BlockSpec((tm // CHUNK, 1, d), lambda i: (i, 0, 0)),
                  pl.BlockSpec((d, LANES), lambda i: (0, 0)),
                  pl.BlockSpec(memory_space=pltpu.SMEM)],
        out_specs=[pl.BlockSpec((tm, d), lambda i: (i, 0)),
                   pl.BlockSpec((tm, LANES), lambda i: (i, 0))],
        out_shape=[jax.ShapeDtypeStruct((t, d), MXU), jax.ShapeDtypeStruct((t, LANES), F32)],
        compiler_params=_cparams(("arbitrary",)),
    )(x, g.reshape(1, d), sc, sh, w_router_pad, router_bias)


def _moe_kernel(h_ref, gates_ref, wg_ref, wu_ref, wd_ref, g2_ref, res_ref, o_ref, acc_ref, *, tm):
    e = pl.program_id(1)

    @pl.when(e == 0)
    def _():
        acc_ref[...] = jnp.zeros_like(acc_ref)

    h = h_ref[...]
    a = _dot(h, wg_ref[0].astype(MXU))
    b = _dot(h, wu_ref[0].astype(MXU))
    he = (a / (1.0 + jnp.exp(-a))) * b
    lane = lax.broadcasted_iota(jnp.int32, gates_ref.shape, 1)
    ge = jnp.sum(jnp.where(lane == e, gates_ref[...], 0.0), axis=1, keepdims=True)
    y = _dot(he.astype(MXU), wd_ref[0].astype(MXU))
    acc_ref[...] += ge * y

    @pl.when(e == N_EXPERTS - 1)
    def _():
        d = acc_ref.shape[-1]
        y3 = acc_ref[...].reshape(tm // CHUNK, CHUNK, d) * g2_ref[...]
        o_ref[...] = res_ref[...] + y3.reshape(tm, d)


def _moe(h, gates, w_gate, w_up, w_down, g2, res, *, tm):
    t, d = h.shape
    de = w_gate.shape[-1]
    return pl.pallas_call(
        functools.partial(_moe_kernel, tm=tm),
        grid=(t // tm, N_EXPERTS),
        in_specs=[pl.BlockSpec((tm, d), lambda i, e: (i, 0)),
                  pl.BlockSpec((tm, LANES), lambda i, e: (i, 0)),
                  pl.BlockSpec((1, d, de), lambda i, e: (e, 0, 0)),
                  pl.BlockSpec((1, d, de), lambda i, e: (e, 0, 0)),
                  pl.BlockSpec((1, de, d), lambda i, e: (e, 0, 0)),
                  pl.BlockSpec((tm // CHUNK, 1, d), lambda i, e: (i, 0, 0)),
                  pl.BlockSpec((tm, d), lambda i, e: (i, 0))],
        out_specs=pl.BlockSpec((tm, d), lambda i, e: (i, 0)),
        out_shape=jax.ShapeDtypeStruct((t, d), F32),
        scratch_shapes=[pltpu.VMEM((tm, d), F32)],
        compiler_params=_cparams(("arbitrary", "arbitrary")),
    )(h, gates, w_gate, w_up, w_down, g2, res)


def _rel_bucket(rel):
    half = N_BUCKETS // 2
    max_exact = half // 2
    n = jnp.abs(rel)
    nf = jnp.maximum(n, 1).astype(F32)
    large = max_exact + (jnp.log(nf / max_exact) / math.log(MAX_DISTANCE / max_exact)
                         * (half - max_exact)).astype(jnp.int32)
    large = jnp.minimum(large, half - 1)
    return jnp.where(rel > 0, half, 0) + jnp.where(n < max_exact, n, large)


def _bias_tiles(table, q_offsets, k_offset, tq, tk, n_heads):
    tiles = []
    for q0 in q_offsets:
        rel = (k_offset + np.arange(tk))[None, :] - (q0 + np.arange(tq))[:, None]
        b = table[_rel_bucket(jnp.asarray(rel, jnp.int32))].astype(F32)
        tiles.append(jnp.moveaxis(b, -1, 0)[:n_heads])
    return jnp.stack(tiles)


def _bias_far(table, n_heads):
    rel = jnp.full((1,), -(MAX_DISTANCE + 1), jnp.int32)
    return table[_rel_bucket(rel)][0, :n_heads].astype(F32)


def _even_attn_kernel(*refs, tq, tkd, n_far_static, has_past, lam_init):
    it = iter(refs)
    far_ref, lam_ref, q_ref, kn_ref, vn_ref = (next(it) for _ in range(5))
    cache = [next(it) for _ in range(4)] if has_past else None
    bd_ref, bp_ref, g_ref, tri_ref, o_ref, m_ref, l_ref, acc_ref, later_ref = (next(it) for _ in range(9))

    u = pl.program_id(1)
    i = pl.program_id(2)
    if has_past:
        n_far = n_far_static
        prev_start = n_far_static * TK
        diag_k = kn_ref[...]
        diag_v = vn_ref[...]
        has_prev = None
    else:
        n_far = jnp.maximum(i - 1, 0)
        prev_start = pl.multiple_of(jnp.maximum(i - 1, 0) * TK, TK)
        d0 = pl.multiple_of(i * TK, TK)
        diag_k = kn_ref[pl.ds(d0, tkd), :]
        diag_v = vn_ref[pl.ds(d0, tkd), :]
        has_prev = i >= 1

    def block_reader(kref, vref):
        if has_past:
            return lambda start: (kref[0, pl.ds(start, TK), :], vref[0, pl.ds(start, TK), :])
        return lambda start: (kn_ref[pl.ds(start, TK), :], vn_ref[pl.ds(start, TK), :])

    def run_prev(fn):
        if has_past:
            fn()
        else:
            pl.when(has_prev)(fn)

    qq = lax.broadcasted_iota(jnp.int32, (tq, tkd), 0)
    kk = lax.broadcasted_iota(jnp.int32, (tq, tkd), 1)
    q = q_ref[...]

    @pl.when(u < N_HEADS_A)
    def _diff():
        kv_block = block_reader(*(cache[0:2] if has_past else (None, None)))
        m_ref[...] = jnp.full_like(m_ref, NEG_INF)
        l_ref[...] = jnp.zeros_like(l_ref)
        acc_ref[...] = jnp.zeros_like(acc_ref)
        qr = [q[:, r * DA_QK:(r + 1) * DA_QK].astype(MXU) for r in range(2)]
        vis = (kk // CHUNK) <= (qq // CHUNK)

        def step(kb, vb, bias_of, mask):
            vb = vb.astype(MXU)
            for r in range(2):
                s = _dot_nt(qr[r], kb[:, r * DA_QK:(r + 1) * DA_QK].astype(MXU)) * DA_QK ** -0.5
                s = s + bias_of(r)
                if mask is not None:
                    s = jnp.where(mask, s, NEG_INF)
                m_prev = m_ref[r]
                m_new = jnp.maximum(m_prev, jnp.max(s, axis=-1, keepdims=True))
                alpha = jnp.exp(m_prev - m_new)
                p = jnp.exp(s - m_new)
                l_ref[r] = alpha * l_ref[r] + jnp.sum(p, axis=-1, keepdims=True)
                acc_ref[r] = alpha * acc_ref[r] + _dot(p.astype(MXU), vb)
                m_ref[r] = m_new

        def far_body(j, c):
            kb, vb = kv_block(pl.multiple_of(j * TK, TK))
            step(kb, vb, lambda r: far_ref[2 * u + r], None)
            return c

        lax.fori_loop(0, n_far, far_body, 0)

        def prev_blk():
            kb, vb = kv_block(prev_start)
            step(kb, vb, lambda r: bp_ref[0, r], None)

        run_prev(prev_blk)
        step(diag_k, diag_v, lambda r: bd_ref[0, r], vis)

        lam = lam_ref[0]
        o = acc_ref[0] / l_ref[0] - lam * (acc_ref[1] / l_ref[1])
        o = _rms(o, g_ref[...]) * (1.0 - lam_init)
        o_ref[...] = o.astype(o_ref.dtype)

    @pl.when(u >= N_HEADS_A)
    def _stick():
        kv_block = block_reader(*(cache[2:4] if has_past else (None, None)))
        later_ref[...] = jnp.zeros_like(later_ref)
        acc_ref[0] = jnp.zeros_like(acc_ref[0])
        qb = q.astype(MXU)
        earlier = kk < qq
        tri = tri_ref[...]

        def step(kb, vb, mask, tri_b):
            z = _dot_nt(qb, kb.astype(MXU)) * HEAD_DIM ** -0.5
            t = jnp.log1p(jnp.exp(-jnp.abs(z)))
            log_beta = -(jnp.maximum(-z, 0.0) + t)
            log_keep = -(jnp.maximum(z, 0.0) + t)
            if mask is not None:
                log_keep = jnp.where(mask, log_keep, 0.0)
            hi, lo = _split2(log_keep)
            later = _dot(hi, tri_b) + _dot(lo, tri_b) + later_ref[...]
            a = jnp.exp(log_beta + later)
            if mask is not None:
                a = jnp.where(mask, a, 0.0)
            acc_ref[0] += _dot(a.astype(MXU), vb.astype(MXU))
            later_ref[...] += jnp.sum(log_keep, axis=-1, keepdims=True)

        step(diag_k, diag_v, earlier, tri[:tkd, :tkd])

        def prev_blk():
            kb, vb = kv_block(prev_start)
            step(kb, vb, None, tri)

        run_prev(prev_blk)

        def far_body(jj, c):
            j = n_far - 1 - jj
            kb, vb = kv_block(pl.multiple_of(j * TK, TK))
            step(kb, vb, None, tri)
            return c

        lax.fori_loop(0, n_far, far_body, 0)
        o_ref[...] = acc_ref[0].astype(o_ref.dtype)


def _even_attn(qkv, past, table, lam, subln_g, lam_init, *, bsz, n):
    has_past = past is not None
    tq = min(n, TK)
    assert n % tq == 0 and (has_past or tq == TK) and (not has_past or n == tq)
    tkd = tq
    nq = n // tq
    far = _bias_far(table, N_MAPS_A)
    bd = _bias_tiles(table, [0], 0, tq, tkd, N_MAPS_A)
    bp = _bias_tiles(table, [0], -TK, tq, TK, N_MAPS_A)
    tri = jnp.asarray(np.arange(TK)[:, None] > np.arange(TK)[None, :], MXU)

    def qcol(u):
        return jnp.where(u < 8, u, 16 + u)

    def kcol(u):
        return jnp.where(u < 8, 8 + u, 24 + u)

    def vcol(u):
        return jnp.where(u < 8, 16 + u, 32 + u)

    smem = pl.BlockSpec(memory_space=pltpu.SMEM)
    in_specs = [smem, smem,
                pl.BlockSpec((tq, HEAD_DIM), lambda b, u, i: (b * nq + i, qcol(u))),
                pl.BlockSpec((n, HEAD_DIM), lambda b, u, i: (b, kcol(u))),
                pl.BlockSpec((n, HEAD_DIM), lambda b, u, i: (b, vcol(u)))]
    args = [far, lam.reshape(1), qkv, qkv, qkv]
    n_far_static = 0
    if has_past:
        p_len = past[0].shape[1]
        assert p_len % TK == 0 and p_len >= TK
        n_far_static = p_len // TK - 1
        lo = lambda b, u, i: (b, 0, jnp.minimum(u, 7))
        hi = lambda b, u, i: (b, 0, jnp.maximum(u, 8) - 8)
        in_specs += [pl.BlockSpec((1, p_len, HEAD_DIM), m) for m in (lo, lo, hi, hi)]
        args += [a.reshape(bsz, p_len, -1) for a in past]
    in_specs += [pl.BlockSpec((1, 2, tq, tkd), lambda b, u, i: (0, jnp.minimum(u, 7), 0, 0)),
                 pl.BlockSpec((1, 2, tq, TK), lambda b, u, i: (0, jnp.minimum(u, 7), 0, 0)),
                 pl.BlockSpec((1, HEAD_DIM), lambda b, u, i: (0, 0)),
                 pl.BlockSpec((TK, TK), lambda b, u, i: (0, 0))]
    args += [bd, bp, subln_g.reshape(1, HEAD_DIM), tri]
    kern = functools.partial(_even_attn_kernel, tq=tq, tkd=tkd, n_far_static=n_far_static,
                             has_past=has_past, lam_init=lam_init)
    return pl.pallas_call(
        kern, grid=(bsz, 16, nq), in_specs=in_specs,
        out_specs=pl.BlockSpec((tq, HEAD_DIM), lambda b, u, i: (b * nq + i, u)),
        out_shape=jax.ShapeDtypeStruct((bsz * n, 16 * HEAD_DIM), MXU),
        scratch_shapes=[pltpu.VMEM((2, tq, 1), F32), pltpu.VMEM((2, tq, 1), F32),
                        pltpu.VMEM((2, tq, HEAD_DIM), F32), pltpu.VMEM((tq, 1), F32)],
        compiler_params=_cparams(("arbitrary", "arbitrary", "arbitrary")),
    )(*args)


INT_MIN = int(np.iinfo(np.int32).min)
NEG_INF_KEY = int(np.array(-np.inf, np.float32).view(np.int32)) ^ 0x7FFFFFFF
INDEX_BITS = 15


def _sort_key(x):
    b = lax.bitcast_convert_type(x, jnp.int32)
    return b ^ ((b >> 31) & 0x7FFFFFFF)


def _indexer_kernel(*refs, tq, n, p_len, top_k, has_past, tkn):
    it = iter(refs)
    qi_ref, wi_ref, kin_ref = next(it), next(it), next(it)
    kip_ref = next(it) if has_past else None
    seln_ref = next(it)
    selp_ref = next(it) if has_past else None
    keyn_ref = next(it)
    keyp_ref = next(it) if has_past else None

    i = pl.program_id(1)
    qi = qi_ref[...]
    wi = wi_ref[...] * N_IDX_HEADS ** -0.5

    def scores(kmat):
        kb = kmat.astype(MXU)
        acc = None
        for ih in range(N_IDX_HEADS):
            d = _dot_nt(qi[:, ih * D_IDX:(ih + 1) * D_IDX], kb) * D_IDX ** -0.5
            term = wi[:, ih:ih + 1] * jnp.maximum(d, 0.0)
            acc = term if acc is None else acc + term
        return acc

    s_new = scores(kin_ref[...])
    qq = i * tq + lax.broadcasted_iota(jnp.int32, (tq, n), 0)
    kk = lax.broadcasted_iota(jnp.int32, (tq, n), 1)
    s_new = jnp.where((kk // CHUNK) <= (qq // CHUNK), s_new, NEG_INF)
    keyn_ref[...] = _sort_key(s_new)
    parts = [(keyn_ref, p_len, n)]
    if has_past:
        keyp_ref[...] = _sort_key(scores(kip_ref[0]))
        parts.append((keyp_ref, 0, p_len))

    def count(pred):
        tot = None
        for ref, base, width in parts:
            idx = base + lax.broadcasted_iota(jnp.int32, (tq, width), 1)
            c = jnp.sum(jnp.where(pred(ref[...], idx), 1.0, 0.0), axis=-1, keepdims=True)
            tot = c if tot is None else tot + c
        return tot

    def body(b, t):
        cand = t + jnp.left_shift(jnp.int32(1), 31 - b)
        c = count(lambda k, idx: k >= cand)
        return jnp.where(c >= top_k, cand, t)

    t = lax.fori_loop(0, 32, body, jnp.full((tq, 1), INT_MIN, jnp.int32))
    n_gt = count(lambda k, idx: k > t)
    n_eq = count(lambda k, idx: k == t)
    need = top_k - n_gt
    tie = jnp.where((n_eq != need) & (t > NEG_INF_KEY), 1.0, 0.0)

    outs = [(parts[0], seln_ref, tkn)] + ([(parts[1], selp_ref, TK)] if has_past else [])

    def write(sel_of):
        for (ref, base, width), out_ref, tk in outs:
            for jb in range(width // tk):
                k = ref[:, jb * tk:(jb + 1) * tk]
                idx = base + jb * tk + lax.broadcasted_iota(jnp.int32, (tq, tk), 1)
                out_ref[jb] = jnp.where(sel_of(k, idx) & (k > NEG_INF_KEY), 1.0, 0.0)

    write(lambda k, idx: k >= t)

    @pl.when(jnp.max(tie) > 0.0)
    def _():
        def jbody(b, jv):
            cand = jv + jnp.left_shift(jnp.int32(1), INDEX_BITS - 1 - b)
            c = count(lambda k, idx: (k == t) & (idx < cand))
            return jnp.where(c < need, cand, jv)

        jv = lax.fori_loop(0, INDEX_BITS, jbody, jnp.zeros((tq, 1), jnp.int32))
        write(lambda k, idx: (k > t) | ((k == t) & (idx <= jv)))


def _indexer(q_i, proj, k_idx_past, *, bsz, n, top_k):
    has_past = k_idx_past is not None
    tq = min(n, 128)
    nq = n // tq
    tkn = min(n, TK)
    p_len = k_idx_past.shape[1] if has_past else 0
    assert n + p_len < 2 ** (INDEX_BITS - 1)
    t = bsz * n
    in_specs = [pl.BlockSpec((tq, N_IDX_HEADS * D_IDX), lambda b, i: (b * nq + i, 0)),
                pl.BlockSpec((tq, LANES), lambda b, i: (b * nq + i, 9)),
                pl.BlockSpec((n, D_IDX), lambda b, i: (b, 8))]
    args = [q_i, proj, proj]
    out_specs = [pl.BlockSpec((n // tkn, tq, tkn), lambda b, i: (0, b * nq + i, 0))]
    out_shape = [jax.ShapeDtypeStruct((n // tkn, t, tkn), F32)]
    scratch = [pltpu.VMEM((tq, n), jnp.int32)]
    if has_past:
        in_specs.append(pl.BlockSpec((1, p_len, D_IDX), lambda b, i: (b, 0, 0)))
        args.append(k_idx_past)
        out_specs.append(pl.BlockSpec((p_len // TK, tq, TK), lambda b, i: (0, b * nq + i, 0)))
        out_shape.append(jax.ShapeDtypeStruct((p_len // TK, t, TK), F32))
        scratch.append(pltpu.VMEM((tq, p_len), jnp.int32))
    res = pl.pallas_call(
        functools.partial(_indexer_kernel, tq=tq, n=n, p_len=p_len, top_k=top_k, has_past=has_past,
                          tkn=tkn),
        grid=(bsz, nq), in_specs=in_specs, out_specs=out_specs, out_shape=out_shape,
        scratch_shapes=scratch,
        compiler_params=_cparams(("arbitrary", "arbitrary")),
    )(*args)
    return res[0], (res[1] if has_past else None)


def _sparse_attn_kernel(*refs, tq, tkd, n_far_static, has_past):
    it = iter(refs)
    far_ref, q_ref, kvn_ref, seln_ref = (next(it) for _ in range(4))
    kvp_ref, selp_ref = (next(it), next(it)) if has_past else (None, None)
    bd_ref, bp_ref, o_ref, m_ref, l_ref, acc_ref = (next(it) for _ in range(6))
    nh = N_HEADS_C
    i = pl.program_id(1)
    q = q_ref[...].reshape(nh * tq, KV_LORA)
    m_ref[...] = jnp.full_like(m_ref, NEG_INF)
    l_ref[...] = jnp.zeros_like(l_ref)
    acc_ref[...] = jnp.zeros_like(acc_ref)

    def step(kvb, sel, bias):
        kvb = kvb.astype(MXU)
        tk = kvb.shape[0]
        s = _dot_nt(q, kvb).reshape(nh, tq, tk) * HEAD_DIM ** -0.5 + bias
        s = jnp.where(sel[None] > 0.0, s, NEG_INF)
        m_prev = m_ref[...]
        m_new = jnp.maximum(m_prev, jnp.max(s, axis=-1, keepdims=True))
        m_safe = jnp.where(m_new == NEG_INF, 0.0, m_new)
        alpha = jnp.exp(m_prev - m_safe)
        p = jnp.exp(s - m_safe)
        l_ref[...] = alpha * l_ref[...] + jnp.sum(p, axis=-1, keepdims=True)
        pv = _dot(p.reshape(nh * tq, tk).astype(MXU), kvb).reshape(nh, tq, KV_LORA)
        acc_ref[...] = alpha * acc_ref[...] + pv
        m_ref[...] = m_new

    if has_past:
        def far_body(j, c):
            step(kvp_ref[0, pl.ds(pl.multiple_of(j * TK, TK), TK), :], selp_ref[j], far_ref[...])
            return c

        lax.fori_loop(0, n_far_static, far_body, 0)
        step(kvp_ref[0, pl.ds(n_far_static * TK, TK), :], selp_ref[n_far_static], bp_ref[0])
        step(kvn_ref[...], seln_ref[0], bd_ref[0])
    else:
        jd = (i * tq) // TK

        def far_body(j, c):
            step(kvn_ref[pl.ds(pl.multiple_of(j * TK, TK), TK), :], seln_ref[j], far_ref[...])
            return c

        lax.fori_loop(0, jnp.maximum(jd - 1, 0), far_body, 0)

        @pl.when(jd >= 1)
        def _():
            jp = jnp.maximum(jd - 1, 0)
            step(kvn_ref[pl.ds(pl.multiple_of(jp * TK, TK), TK), :], seln_ref[jp], bp_ref[0])

        step(kvn_ref[pl.ds(pl.multiple_of(jd * TK, TK), TK), :], seln_ref[jd], bd_ref[0])

    o_ref[...] = (acc_ref[...] / l_ref[...]).astype(o_ref.dtype)


def _sparse_attn(q_lat, kv_lat, sel_new, kv_past, sel_past, table, *, bsz, n):
    has_past = kv_past is not None
    nh = N_HEADS_C
    t = bsz * n
    if has_past:
        tq = tkd = n
        q_offsets = [0]
        p_len = kv_past.shape[1]
        n_far_static = p_len // TK - 1
    else:
        tq, tkd = min(n, 128), TK
        assert n % TK == 0
        q_offsets = list(range(0, TK, tq))
        n_far_static = 0
    nq = n // tq
    npar = len(q_offsets)
    far = _bias_far(table, nh).reshape(nh, 1, 1)
    bd = _bias_tiles(table, q_offsets, 0, tq, tkd, nh)
    bp = _bias_tiles(table, q_offsets, -TK, tq, TK, nh)
    in_specs = [pl.BlockSpec((nh, 1, 1), lambda b, i: (0, 0, 0)),
                pl.BlockSpec((nh, tq, KV_LORA), lambda b, i: (0, b * nq + i, 0)),
                pl.BlockSpec((n, KV_LORA), lambda b, i: (b, 0)),
                pl.BlockSpec((n // tkd, tq, tkd), lambda b, i: (0, b * nq + i, 0))]
    args = [far, q_lat, kv_lat, sel_new]
    if has_past:
        in_specs += [pl.BlockSpec((1, p_len, KV_LORA), lambda b, i: (b, 0, 0)),
                     pl.BlockSpec((p_len // TK, tq, TK), lambda b, i: (0, b * nq + i, 0))]
        args += [kv_past, sel_past]
    in_specs += [pl.BlockSpec((1, nh, tq, tkd), lambda b, i: (i % npar, 0, 0, 0)),
                 pl.BlockSpec((1, nh, tq, TK), lambda b, i: (i % npar, 0, 0, 0))]
    args += [bd, bp]
    return pl.pallas_call(
        functools.partial(_sparse_attn_kernel, tq=tq, tkd=tkd, n_far_static=n_far_static,
                          has_past=has_past),
        grid=(bsz, nq), in_specs=in_specs,
        out_specs=pl.BlockSpec((nh, tq, KV_LORA), lambda b, i: (0, b * nq + i, 0)),
        out_shape=jax.ShapeDtypeStruct((nh, t, KV_LORA), MXU),
        scratch_shapes=[pltpu.VMEM((nh, tq, 1), F32), pltpu.VMEM((nh, tq, 1), F32),
                        pltpu.VMEM((nh, tq, KV_LORA), F32)],
        compiler_params=_cparams(("arbitrary", "arbitrary")),
    )(*args)


def _tile(n, cap, mult):
    best = None
    for t in range(mult, min(n, cap) + 1, mult):
        if n % t == 0:
            best = t
    assert best is not None, (n, cap, mult)
    return best


def _per_chunk(v, reps):
    return jnp.repeat(v, reps, axis=0)[:, None, :]


def _trunk(x, mod, past, p, w):
    bsz, n, d = x.shape
    t = bsz * n
    x = x.reshape(t, d)
    tm = _tile(t, 512, CHUNK)
    tmn = _tile(t, 256, CHUNK)
    tn_d = _tile(d, 1024, LANES)
    depth = p['w_ada'].shape[0]
    p_len = 0 if past is None else past[0].shape[2]
    top_k = min(TOPK_MAX, (p_len + n) // 4)
    even_rows, odd_rows = [], []
    for l in range(depth):
        m6 = mod[l].reshape(bsz, 6, d)
        sh1, sc1, g1, sh2, sc2, g2 = (_per_chunk(m6[:, k], n // CHUNK) for k in range(6))
        h = _norm(x, p['norm_mix_g'][l], tm=tmn, sc=sc1, sh=sh1, out_dtype=MXU)
        if l % 2 == 0:
            i = l // 2
            lam_init = 0.8 - 0.6 * math.exp(-0.3 * l)
            qkv = _mm(h, p['w_in_even'][i], n_out=48 * LANES, tm=tm, tn=1024)
            past_i = None if past is None else tuple(a[i] for a in past[:4])
            lam = (jnp.exp(jnp.sum(p['lam_q1'][i].astype(F32) * p['lam_k1'][i].astype(F32)))
                   - jnp.exp(jnp.sum(p['lam_q2'][i].astype(F32) * p['lam_k2'][i].astype(F32)))
                   + lam_init)
            o = _even_attn(qkv, past_i, p['rel_bias_table'], lam, p['subln_g'][i], lam_init,
                           bsz=bsz, n=n)
            even_rows.append((qkv[:, 1024:2048].reshape(bsz, n, N_MAPS_A, DA_QK),
                              qkv[:, 2048:3072].reshape(bsz, n, N_HEADS_A, HEAD_DIM),
                              qkv[:, 4096:5120].reshape(bsz, n, N_HEADS_B, HEAD_DIM),
                              qkv[:, 5120:6144].reshape(bsz, n, N_HEADS_B, HEAD_DIM)))
            x = _mm(o, p['w_out_even'][i], n_out=d, tm=tm, tn=tn_d, gate=g1, res=x)
        else:
            j = l // 2
            proj = _mm(h, w['w_in_odd'][j], n_out=10 * LANES, tm=tm, tn=10 * LANES)
            c_q = _norm(proj, p['g_q'][j], tm=tmn, width=Q_LORA, col_block=0, out_dtype=MXU)
            kv_lat = _norm(proj, p['g_kv'][j], tm=tmn, width=KV_LORA, col_block=1)
            k_i = proj[:, 1024:1152]
            q = _mm(c_q, p['w_uq'][j], n_out=N_HEADS_C * HEAD_DIM, tm=tm, tn=1024, out_dtype=MXU)
            q_lat = _mm_heads_out(q, w['w_uk_t'][j], tm=tm)
            q_i = _mm(c_q, p['w_qidx'][j], n_out=N_IDX_HEADS * D_IDX, tm=tm, tn=1024, out_dtype=MXU)
            kv_past = None if past is None else past[4][j]
            ki_past = None if past is None else past[5][j]
            sel_new, sel_past = _indexer(q_i, proj, ki_past, bsz=bsz, n=n, top_k=top_k)
            o_lat = _sparse_attn(q_lat, kv_lat, sel_new, kv_past, sel_past, p['rel_bias_table'],
                                 bsz=bsz, n=n)
            o = _mm_heads_in(o_lat, w['w_uv_h'][j], tm=tm)
            odd_rows.append((kv_lat.reshape(bsz, n, KV_LORA), k_i.reshape(bsz, n, D_IDX)))
            x = _mm(o, p['w_out_odd'][j], n_out=d, tm=tm, tn=tn_d, gate=g1, res=x)
        h2, gates = _norm_router(x, p['norm_ffn_g'][l], sc2, sh2, w['w_router'], p['router_bias'], tm=tmn)
        x = _moe(h2, gates, w['w_gate'][l], w['w_up'][l], w['w_down'][l], g2, x, tm=tm)
    y = _norm(x, p['final_norm_g'], tm=tmn).reshape(bsz, n, d)
    ev = tuple(jnp.stack([r[m] for r in even_rows]) for m in range(4))
    od = tuple(jnp.stack([r[m] for r in odd_rows]) for m in range(2))
    return y, ev, od


def kernel(x_prompt, x_sample, cache_a_k, cache_a_v, cache_b_k, cache_b_v, cache_c_kv, cache_c_idx,
           c_prompt, c_sample, rel_bias_table, norm_mix_g, norm_ffn_g, final_norm_g, w_ada, b_ada,
           w_in_even, lam_q1, lam_k1, lam_q2, lam_k2, subln_g, w_out_even, w_in_odd, g_q, g_kv,
           w_uq, w_qidx, w_uk, w_uv, w_out_odd, w_router, router_bias, w_gate, w_up, w_down):
    p = dict(rel_bias_table=rel_bias_table, norm_mix_g=norm_mix_g, norm_ffn_g=norm_ffn_g,
             final_norm_g=final_norm_g, w_ada=w_ada, b_ada=b_ada, w_in_even=w_in_even,
             lam_q1=lam_q1, lam_k1=lam_k1, lam_q2=lam_q2, lam_k2=lam_k2, subln_g=subln_g,
             w_out_even=w_out_even, g_q=g_q, g_kv=g_kv, w_uq=w_uq, w_qidx=w_qidx,
             w_out_odd=w_out_odd, router_bias=router_bias)
    d = x_prompt.shape[-1]
    depth = w_ada.shape[0]
    w = dict(
        w_in_odd=jnp.pad(w_in_odd, ((0, 0), (0, 0), (0, 10 * LANES - w_in_odd.shape[-1]))),
        w_uk_t=jnp.transpose(w_uk, (0, 2, 3, 1)),
        w_uv_h=jnp.transpose(w_uv, (0, 2, 1, 3)),
        w_router=jnp.pad(w_router, ((0, 0), (0, LANES - N_EXPERTS))),
        w_gate=w_gate.astype(MXU), w_up=w_up.astype(MXU), w_down=w_down.astype(MXU))
    nb_p, nb_s = c_prompt.shape[0], c_sample.shape[0]
    c_all = jnp.concatenate([c_prompt, c_sample], axis=0)
    rows = -(-(nb_p + nb_s) // 16) * 16
    c_act = jnp.pad(c_all * (1.0 / (1.0 + jnp.exp(-c_all))), ((0, rows - nb_p - nb_s), (0, 0)))
    tn_ada = _tile(6 * d, 1024, LANES)
    mods = [_mm(c_act, w_ada[l], n_out=6 * d, tm=rows, tn=tn_ada, bias=b_ada[l]) for l in range(depth)]
    mod_p = [m[:nb_p] for m in mods]
    mod_s = [m[nb_p:nb_p + nb_s] for m in mods]

    y_prompt, ev_p, od_p = _trunk(x_prompt, mod_p, None, p, w)
    past = (cache_a_k, cache_a_v, cache_b_k, cache_b_v, cache_c_kv, cache_c_idx)
    y_sample, ev_s, od_s = _trunk(x_sample, mod_s, past, p, w)
    return (y_prompt, y_sample) + ev_p + od_p + ev_s + od_s
```

```python
import functools
import math

import jax
import jax.numpy as jnp
import numpy as np
from jax import lax
from jax.experimental import pallas as pl
from jax.experimental.pallas import tpu as pltpu

F32 = jnp.float32
MXU = jnp.bfloat16

CHUNK = 64
HEAD_DIM = 128
N_HEADS_A = 8
N_MAPS_A = 16
DA_QK = 64
N_HEADS_B = 8
N_HEADS_C = 16
Q_LORA = 512
KV_LORA = 512
N_IDX_HEADS = 16
D_IDX = 128
TOPK_MAX = 256
N_BUCKETS = 32
MAX_DISTANCE = 128
N_EXPERTS = 16
N_GROUPS = 4
EXPERTS_PER_GROUP = 4
EPS = 1e-6
LANES = 128
TK = 256
VMEM_LIMIT = 56 * 1024 * 1024
NEG_INF = float("-inf")


def _cparams(sem):
    return pltpu.CompilerParams(dimension_semantics=sem, vmem_limit_bytes=VMEM_LIMIT)


def _dot(a, b):
    return jnp.dot(a, b, preferred_element_type=F32)


def _dot_nt(a, b):
    return lax.dot_general(a, b, (((1,), (1,)), ((), ())), preferred_element_type=F32)


def _mm_kernel(*refs, has_bias, has_res, tm):
    it = iter(refs)
    a_ref, b_ref = next(it), next(it)
    bias_ref = next(it) if has_bias else None
    gate_ref, res_ref = (next(it), next(it)) if has_res else (None, None)
    o_ref, bsc = next(it), next(it)

    @pl.when(pl.program_id(1) == 0)
    def _():
        bsc[...] = b_ref[...].reshape(bsc.shape).astype(bsc.dtype)

    a = a_ref[...]
    a = a.reshape(a.shape[-2:]).astype(MXU)
    acc = _dot(a, bsc[...])
    if has_bias:
        acc = acc + bias_ref[...]
    if has_res:
        tn = acc.shape[-1]
        acc = (acc.reshape(tm // CHUNK, CHUNK, tn) * gate_ref[...]).reshape(tm, tn)
        acc = acc + res_ref[...]
    o_ref[...] = acc.reshape(o_ref.shape).astype(o_ref.dtype)


def _mm(a, b, *, n_out, tm, tn, bias=None, gate=None, res=None, out_dtype=F32, name="mm"):
    m, k = a.shape
    assert m % tm == 0 and n_out % tn == 0 and b.shape[0] == k
    in_specs = [pl.BlockSpec((tm, k), lambda j, i: (i, 0)),
                pl.BlockSpec((k, tn), lambda j, i: (0, j))]
    args = [a, b]
    if bias is not None:
        in_specs.append(pl.BlockSpec((1, tn), lambda j, i: (0, j)))
        args.append(bias.reshape(1, n_out))
    if res is not None:
        in_specs.append(pl.BlockSpec((tm // CHUNK, 1, tn), lambda j, i: (i, 0, j)))
        in_specs.append(pl.BlockSpec((tm, tn), lambda j, i: (i, j)))
        args += [gate, res]
    return pl.pallas_call(
        functools.partial(_mm_kernel, has_bias=bias is not None, has_res=res is not None, tm=tm),
        grid=(n_out // tn, m // tm), in_specs=in_specs,
        out_specs=pl.BlockSpec((tm, tn), lambda j, i: (i, j)),
        out_shape=jax.ShapeDtypeStruct((m, n_out), out_dtype),
        scratch_shapes=[pltpu.VMEM((k, tn), MXU)],
        compiler_params=_cparams(("arbitrary", "arbitrary")), name=name,
    )(*args)


def _mm_heads_out(a, b_h, *, tm):
    m = a.shape[0]
    nh, ka, n = b_h.shape
    return pl.pallas_call(
        functools.partial(_mm_kernel, has_bias=False, has_res=False, tm=tm),
        grid=(nh, m // tm),
        in_specs=[pl.BlockSpec((tm, ka), lambda h, i: (i, h)),
                  pl.BlockSpec((1, ka, n), lambda h, i: (h, 0, 0))],
        out_specs=pl.BlockSpec((1, tm, n), lambda h, i: (h, i, 0)),
        out_shape=jax.ShapeDtypeStruct((nh, m, n), MXU),
        scratch_shapes=[pltpu.VMEM((ka, n), MXU)],
        compiler_params=_cparams(("arbitrary", "arbitrary")), name="mm_heads_out",
    )(a, b_h)


def _mm_heads_in(a_h, b_h, *, tm):
    nh, m, ka = a_h.shape
    n = b_h.shape[2]
    return pl.pallas_call(
        functools.partial(_mm_kernel, has_bias=False, has_res=False, tm=tm),
        grid=(nh, m // tm),
        in_specs=[pl.BlockSpec((1, tm, ka), lambda h, i: (h, i, 0)),
                  pl.BlockSpec((1, ka, n), lambda h, i: (h, 0, 0))],
        out_specs=pl.BlockSpec((tm, n), lambda h, i: (i, h)),
        out_shape=jax.ShapeDtypeStruct((m, nh * n), MXU),
        scratch_shapes=[pltpu.VMEM((ka, n), MXU)],
        compiler_params=_cparams(("arbitrary", "arbitrary")), name="mm_heads_in",
    )(a_h, b_h)


def _rms(x, g):
    return x * lax.rsqrt(jnp.mean(x * x, axis=-1, keepdims=True) + EPS) * g


def _modulate(y, sc_ref, sh_ref, tm):
    d = y.shape[-1]
    y3 = y.reshape(tm // CHUNK, CHUNK, d)
    y3 = y3 * (1.0 + sc_ref[...]) + sh_ref[...]
    return y3.reshape(tm, d)


def _norm_kernel(*refs, has_mod, tm):
    if has_mod:
        x_ref, g_ref, sc_ref, sh_ref, o_ref = refs
    else:
        x_ref, g_ref, o_ref = refs
    y = _rms(x_ref[...], g_ref[...])
    if has_mod:
        y = _modulate(y, sc_ref, sh_ref, tm)
    o_ref[...] = y.astype(o_ref.dtype)


def _norm(x, g, *, tm, width=None, col_block=0, sc=None, sh=None, out_dtype=F32):
    width = x.shape[1] if width is None else width
    rows = x.shape[0]
    assert rows % tm == 0
    has_mod = sc is not None
    in_specs = [pl.BlockSpec((tm, width), lambda i: (i, col_block)),
                pl.BlockSpec((1, width), lambda i: (0, 0))]
    args = [x, g.reshape(1, width)]
    if has_mod:
        in_specs += [pl.BlockSpec((tm // CHUNK, 1, width), lambda i: (i, 0, 0))] * 2
        args += [sc, sh]
    return pl.pallas_call(
        functools.partial(_norm_kernel, has_mod=has_mod, tm=tm),
        grid=(rows // tm,), in_specs=in_specs,
        out_specs=pl.BlockSpec((tm, width), lambda i: (i, 0)),
        out_shape=jax.ShapeDtypeStruct((rows, width), out_dtype),
        compiler_params=_cparams(("arbitrary",)), name="norm",
    )(*args)


def _split2(x):
    hi = x.astype(MXU)
    lo = (x - hi.astype(F32)).astype(MXU)
    return hi, lo


def _route(logits, rb_ref):
    tm = logits.shape[0]
    aff = [1.0 / (1.0 + jnp.exp(-logits[:, e:e + 1])) for e in range(N_EXPERTS)]
    sc = [aff[e] + rb_ref[e] for e in range(N_EXPERTS)]
    npg = EXPERTS_PER_GROUP
    gscore = []
    for g in range(N_GROUPS):
        v = sc[g * npg:(g + 1) * npg]
        best = None
        for a in range(npg):
            for b in range(a + 1, npg):
                s = v[a] + v[b]
                best = s if best is None else jnp.maximum(best, s)
        gscore.append(best)
    gb = jnp.zeros((tm, 1), jnp.int32)
    gv = gscore[0]
    for g in range(1, N_GROUPS):
        better = gscore[g] > gv
        gb = jnp.where(better, g, gb)
        gv = jnp.where(better, gscore[g], gv)
    u = [sc[j] for j in range(npg)]
    a4 = [aff[j] for j in range(npg)]
    for g in range(1, N_GROUPS):
        pick = gb == g
        u = [jnp.where(pick, sc[g * npg + j], u[j]) for j in range(npg)]
        a4 = [jnp.where(pick, aff[g * npg + j], a4[j]) for j in range(npg)]
    i1 = jnp.zeros((tm, 1), jnp.int32)
    v1 = u[0]
    for j in range(1, npg):
        better = u[j] > v1
        i1 = jnp.where(better, j, i1)
        v1 = jnp.where(better, u[j], v1)
    i2 = jnp.full((tm, 1), -1, jnp.int32)
    v2 = jnp.full((tm, 1), NEG_INF, F32)
    for j in range(npg):
        better = (i1 != j) & ((u[j] > v2) | (i2 < 0))
        i2 = jnp.where(better, j, i2)
        v2 = jnp.where(better, u[j], v2)
    w1 = a4[0]
    w2 = a4[0]
    for j in range(1, npg):
        w1 = jnp.where(i1 == j, a4[j], w1)
        w2 = jnp.where(i2 == j, a4[j], w2)
    tot = w1 + w2
    w1 = w1 / tot
    w2 = w2 / tot
    e1 = gb * npg + i1
    e2 = gb * npg + i2
    lane = lax.broadcasted_iota(jnp.int32, (tm, LANES), 1)
    out = jnp.where(lane == 0, e1.astype(F32), 0.0)
    out = jnp.where(lane == 1, e2.astype(F32), out)
    out = jnp.where(lane == 2, w1, out)
    return jnp.where(lane == 3, w2, out)


def _norm_router_kernel(x_ref, g_ref, sc_ref, sh_ref, wr_ref, rb_ref, h_ref, route_ref, *, tm):
    h = _modulate(_rms(x_ref[...], g_ref[...]), sc_ref, sh_ref, tm)
    h_ref[...] = h
    hh, hl = _split2(h)
    wh, wl = _split2(wr_ref[...])
    logits = _dot(hh, wh) + (_dot(hl, wh) + _dot(hh, wl))
    route_ref[...] = _route(logits, rb_ref)


def _norm_router(x, g, sc, sh, w_router_pad, router_bias, *, tm):
    t, d = x.shape
    return pl.pallas_call(
        functools.partial(_norm_router_kernel, tm=tm),
        grid=(t // tm,),
        in_specs=[pl.BlockSpec((tm, d), lambda i: (i, 0)),
                  pl.BlockSpec((1, d), lambda i: (0, 0)),
                  pl.BlockSpec((tm // CHUNK, 1, d), lambda i: (i, 0, 0)),
                  pl.BlockSpec((tm // CHUNK, 1, d), lambda i: (i, 0, 0)),
                  pl.BlockSpec((d, LANES), lambda i: (0, 0)),
                  pl.BlockSpec(memory_space=pltpu.SMEM)],
        out_specs=[pl.BlockSpec((tm, d), lambda i: (i, 0)),
                   pl.BlockSpec((tm, LANES), lambda i: (i, 0))],
        out_shape=[jax.ShapeDtypeStruct((t, d), F32), jax.ShapeDtypeStruct((t, LANES), F32)],
        compiler_params=_cparams(("arbitrary",)), name="norm_router",
    )(x, g.reshape(1, d), sc, sh, w_router_pad, router_bias)


TM_EXPERT = 256


def _row_copy(src_hbm, row, dst_ref, r, sem):
    return pltpu.make_async_copy(src_hbm.at[pl.ds(row, 1), :], dst_ref.at[pl.ds(r, 1), :], sem)


def _gather_start(src_hbm, idx_ref, base, dst_ref, sem, n):
    def body(r, c):
        _row_copy(src_hbm, idx_ref[base + r], dst_ref, r, sem).start()
        return c

    lax.fori_loop(0, n, body, 0, unroll=8)


def _gather_wait(src_hbm, dst_ref, sem, n):
    pltpu.make_async_copy(src_hbm.at[pl.ds(0, n), :], dst_ref.at[pl.ds(0, n), :], sem).wait()


def _moe_kernel(src_ref, te_ref, nv_ref, h_hbm, roww_ref, wg_ref, wu_ref, wd_ref, y_ref, xbuf, sem, *, tm):
    del te_ref
    i = pl.program_id(0)
    slot = i % 2
    nv = nv_ref[0]

    @pl.when(i == 0)
    def _():
        _gather_start(h_hbm, src_ref, 0, xbuf.at[0], sem.at[0], tm)

    @pl.when(i + 1 < nv)
    def _():
        _gather_start(h_hbm, src_ref, (i + 1) * tm, xbuf.at[1 - slot], sem.at[1 - slot], tm)

    @pl.when(i < nv)
    def _():
        _gather_wait(h_hbm, xbuf.at[slot], sem.at[slot], tm)
        x = xbuf[slot].astype(MXU)
        a = _dot(x, wg_ref[0])
        b = _dot(x, wu_ref[0])
        he = (a / (1.0 + jnp.exp(-a))) * b
        y_ref[...] = roww_ref[...] * _dot(he.astype(MXU), wd_ref[0])

    @pl.when(i >= nv)
    def _():
        y_ref[...] = jnp.zeros_like(y_ref)


def _combine_kernel(dest_ref, y_hbm, res_ref, g2_ref, o_ref, ybuf, sem, *, tm, t):
    i = pl.program_id(0)
    n = pl.num_programs(0)
    slot = i % 2

    def start(step, s):
        for j in range(2):
            _gather_start(y_hbm, dest_ref, j * t + step * tm, ybuf.at[s, j], sem.at[s], tm)

    @pl.when(i == 0)
    def _():
        start(0, 0)

    @pl.when(i + 1 < n)
    def _():
        start(i + 1, 1 - slot)

    for j in range(2):
        _gather_wait(y_hbm, ybuf.at[slot, j], sem.at[slot], tm)
    d = o_ref.shape[-1]
    y = ybuf[slot, 0] + ybuf[slot, 1]
    o_ref[...] = res_ref[...] + (y.reshape(tm // CHUNK, CHUNK, d) * g2_ref[...]).reshape(tm, d)


def _moe(h, route, w_gate, w_up, w_down, g2, res, *, tm):
    t, d = h.shape
    de = w_gate.shape[-1]
    te = _tile(2 * t, TM_EXPERT, CHUNK)
    n_tiles = 2 * t // te + N_EXPERTS
    r_pad = n_tiles * te
    e_flat = route[:, 0:2].astype(jnp.int32).reshape(-1)
    w_flat = route[:, 2:4].reshape(-1)
    onehot = (e_flat[:, None] == jnp.arange(N_EXPERTS, dtype=jnp.int32)[None, :]).astype(jnp.int32)
    csum = jnp.cumsum(onehot, axis=0)
    rank = jnp.take_along_axis(csum - onehot, e_flat[:, None], axis=1)[:, 0]
    counts = csum[-1]
    tiles_e = (counts + te - 1) // te
    tile_end = jnp.cumsum(tiles_e)
    seg_start = (tile_end - tiles_e) * te
    dest = seg_start[e_flat] + rank
    row_src = jnp.zeros((r_pad,), jnp.int32).at[dest].set(jnp.arange(2 * t, dtype=jnp.int32) // 2)
    row_w = jnp.zeros((r_pad,), F32).at[dest].set(w_flat).reshape(r_pad, 1)
    tile_e = jnp.minimum(jnp.searchsorted(tile_end, jnp.arange(n_tiles, dtype=jnp.int32), side='right'),
                         N_EXPERTS - 1).astype(jnp.int32)
    n_valid = tile_end[-1:].astype(jnp.int32)
    dest_jm = dest.reshape(t, 2).T.reshape(-1)

    y = pl.pallas_call(
        functools.partial(_moe_kernel, tm=te),
        grid_spec=pltpu.PrefetchScalarGridSpec(
            num_scalar_prefetch=3, grid=(n_tiles,),
            in_specs=[pl.BlockSpec(memory_space=pl.ANY),
                      pl.BlockSpec((te, 1), lambda i, s, e, n: (i, 0)),
                      pl.BlockSpec((1, d, de), lambda i, s, e, n: (e[i], 0, 0)),
                      pl.BlockSpec((1, d, de), lambda i, s, e, n: (e[i], 0, 0)),
                      pl.BlockSpec((1, de, d), lambda i, s, e, n: (e[i], 0, 0))],
            out_specs=pl.BlockSpec((te, d), lambda i, s, e, n: (i, 0)),
            scratch_shapes=[pltpu.VMEM((2, te, d), F32), pltpu.SemaphoreType.DMA((2,))]),
        out_shape=jax.ShapeDtypeStruct((r_pad, d), F32),
        compiler_params=_cparams(("arbitrary",)), name="moe_experts",
    )(row_src, tile_e, n_valid, h, row_w, w_gate, w_up, w_down)

    return pl.pallas_call(
        functools.partial(_combine_kernel, tm=tm, t=t),
        grid_spec=pltpu.PrefetchScalarGridSpec(
            num_scalar_prefetch=1, grid=(t // tm,),
            in_specs=[pl.BlockSpec(memory_space=pl.ANY),
                      pl.BlockSpec((tm, d), lambda i, dst: (i, 0)),
                      pl.BlockSpec((tm // CHUNK, 1, d), lambda i, dst: (i, 0, 0))],
            out_specs=pl.BlockSpec((tm, d), lambda i, dst: (i, 0)),
            scratch_shapes=[pltpu.VMEM((2, 2, tm, d), F32), pltpu.SemaphoreType.DMA((2,))]),
        out_shape=jax.ShapeDtypeStruct((t, d), F32),
        compiler_params=_cparams(("arbitrary",)), name="moe_combine",
    )(dest_jm, y, res, g2)


def _rel_bucket(rel):
    half = N_BUCKETS // 2
    max_exact = half // 2
    n = jnp.abs(rel)
    nf = jnp.maximum(n, 1).astype(F32)
    large = max_exact + (jnp.log(nf / max_exact) / math.log(MAX_DISTANCE / max_exact)
                         * (half - max_exact)).astype(jnp.int32)
    large = jnp.minimum(large, half - 1)
    return jnp.where(rel > 0, half, 0) + jnp.where(n < max_exact, n, large)


def _bias_tiles(table, q_offsets, k_offset, tq, tk, n_heads):
    tiles = []
    for q0 in q_offsets:
        rel = (k_offset + np.arange(tk))[None, :] - (q0 + np.arange(tq))[:, None]
        b = table[_rel_bucket(jnp.asarray(rel, jnp.int32))].astype(F32)
        tiles.append(jnp.moveaxis(b, -1, 0)[:n_heads])
    return jnp.stack(tiles)


def _bias_far(table, n_heads):
    rel = jnp.full((1,), -(MAX_DISTANCE + 1), jnp.int32)
    return table[_rel_bucket(rel)][0, :n_heads].astype(F32)


def _even_attn_kernel(*refs, tq, tkd, n_far_static, has_past, lam_init):
    it = iter(refs)
    far_ref, lam_ref, q_ref, kn_ref, vn_ref = (next(it) for _ in range(5))
    cache = [next(it) for _ in range(4)] if has_past else None
    bd_ref, bp_ref, g_ref, tri_ref, o_ref, m_ref, l_ref, acc_ref, later_ref = (next(it) for _ in range(9))

    u = pl.program_id(1)
    i = pl.program_id(2)
    if has_past:
        n_far = n_far_static
        prev_start = n_far_static * TK
        diag_k = kn_ref[...]
        diag_v = vn_ref[...]
        has_prev = None
    else:
        n_far = jnp.maximum(i - 1, 0)
        prev_start = pl.multiple_of(jnp.maximum(i - 1, 0) * TK, TK)
        d0 = pl.multiple_of(i * TK, TK)
        diag_k = kn_ref[pl.ds(d0, tkd), :]
        diag_v = vn_ref[pl.ds(d0, tkd), :]
        has_prev = i >= 1

    def block_reader(kref, vref):
        if has_past:
            return lambda start: (kref[0, pl.ds(start, TK), :], vref[0, pl.ds(start, TK), :])
        return lambda start: (kn_ref[pl.ds(start, TK), :], vn_ref[pl.ds(start, TK), :])

    def run_prev(fn):
        if has_past:
            fn()
        else:
            pl.when(has_prev)(fn)

    qq = lax.broadcasted_iota(jnp.int32, (tq, tkd), 0)
    kk = lax.broadcasted_iota(jnp.int32, (tq, tkd), 1)
    q = q_ref[...]

    @pl.when(u < N_HEADS_A)
    def _diff():
        kv_block = block_reader(*(cache[0:2] if has_past else (None, None)))
        m_ref[...] = jnp.full_like(m_ref, NEG_INF)
        l_ref[...] = jnp.zeros_like(l_ref)
        acc_ref[...] = jnp.zeros_like(acc_ref)
        qr = [q[:, r * DA_QK:(r + 1) * DA_QK].astype(MXU) for r in range(2)]
        vis = (kk // CHUNK) <= (qq // CHUNK)

        def step(kb, vb, bias_of, mask):
            vb = vb.astype(MXU)
            for r in range(2):
                s = _dot_nt(qr[r], kb[:, r * DA_QK:(r + 1) * DA_QK].astype(MXU)) * DA_QK ** -0.5
                s = s + bias_of(r)
                if mask is not None:
                    s = jnp.where(mask, s, NEG_INF)
                m_prev = m_ref[r]
                m_new = jnp.maximum(m_prev, jnp.max(s, axis=-1, keepdims=True))
                alpha = jnp.exp(m_prev - m_new)
                p = jnp.exp(s - m_new)
                l_ref[r] = alpha * l_ref[r] + jnp.sum(p, axis=-1, keepdims=True)
                acc_ref[r] = alpha * acc_ref[r] + _dot(p.astype(MXU), vb)
                m_ref[r] = m_new

        def far_body(j, c):
            kb, vb = kv_block(pl.multiple_of(j * TK, TK))
            step(kb, vb, lambda r: far_ref[2 * u + r], None)
            return c

        lax.fori_loop(0, n_far, far_body, 0)

        def prev_blk():
            kb, vb = kv_block(prev_start)
            step(kb, vb, lambda r: bp_ref[0, r], None)

        run_prev(prev_blk)
        step(diag_k, diag_v, lambda r: bd_ref[0, r], vis)

        lam = lam_ref[0]
        o = acc_ref[0] / l_ref[0] - lam * (acc_ref[1] / l_ref[1])
        o = _rms(o, g_ref[...]) * (1.0 - lam_init)
        o_ref[...] = o.astype(o_ref.dtype)

    @pl.when(u >= N_HEADS_A)
    def _stick():
        kv_block = block_reader(*(cache[2:4] if has_past else (None, None)))
        later_ref[...] = jnp.zeros_like(later_ref)
        acc_ref[0] = jnp.zeros_like(acc_ref[0])
        qb = q.astype(MXU)
        earlier = kk < qq
        tri = tri_ref[...]

        def step(kb, vb, mask, tri_b):
            z = _dot_nt(qb, kb.astype(MXU)) * HEAD_DIM ** -0.5
            t = jnp.log1p(jnp.exp(-jnp.abs(z)))
            log_beta = -(jnp.maximum(-z, 0.0) + t)
            log_keep = -(jnp.maximum(z, 0.0) + t)
            if mask is not None:
                log_keep = jnp.where(mask, log_keep, 0.0)
            hi, lo = _split2(log_keep)
            later = _dot(hi, tri_b) + _dot(lo, tri_b) + later_ref[...]
            a = jnp.exp(log_beta + later)
            if mask is not None:
                a = jnp.where(mask, a, 0.0)
            acc_ref[0] += _dot(a.astype(MXU), vb.astype(MXU))
            later_ref[...] += jnp.sum(log_keep, axis=-1, keepdims=True)

        step(diag_k, diag_v, earlier, tri[:tkd, :tkd])

        def prev_blk():
            kb, vb = kv_block(prev_start)
            step(kb, vb, None, tri)

        run_prev(prev_blk)

        def far_body(jj, c):
            j = n_far - 1 - jj
            kb, vb = kv_block(pl.multiple_of(j * TK, TK))
            step(kb, vb, None, tri)
            return c

        lax.fori_loop(0, n_far, far_body, 0)
        o_ref[...] = acc_ref[0].astype(o_ref.dtype)


def _even_attn(qkv, past, table, lam, subln_g, lam_init, *, bsz, n):
    has_past = past is not None
    tq = min(n, TK)
    assert n % tq == 0 and (has_past or tq == TK) and (not has_past or n == tq)
    tkd = tq
    nq = n // tq
    far = _bias_far(table, N_MAPS_A)
    bd = _bias_tiles(table, [0], 0, tq, tkd, N_MAPS_A)
    bp = _bias_tiles(table, [0], -TK, tq, TK, N_MAPS_A)
    tri = jnp.asarray(np.arange(TK)[:, None] > np.arange(TK)[None, :], MXU)

    def qcol(u):
        return jnp.where(u < 8, u, 16 + u)

    def kcol(u):
        return jnp.where(u < 8, 8 + u, 24 + u)

    def vcol(u):
        return jnp.where(u < 8, 16 + u, 32 + u)

    smem = pl.BlockSpec(memory_space=pltpu.SMEM)
    in_specs = [smem, smem,
                pl.BlockSpec((tq, HEAD_DIM), lambda b, u, i: (b * nq + i, qcol(u))),
                pl.BlockSpec((n, HEAD_DIM), lambda b, u, i: (b, kcol(u))),
                pl.BlockSpec((n, HEAD_DIM), lambda b, u, i: (b, vcol(u)))]
    args = [far, lam.reshape(1), qkv, qkv, qkv]
    n_far_static = 0
    if has_past:
        p_len = past[0].shape[1]
        assert p_len % TK == 0 and p_len >= TK
        n_far_static = p_len // TK - 1
        lo = lambda b, u, i: (b, 0, jnp.minimum(u, 7))
        hi = lambda b, u, i: (b, 0, jnp.maximum(u, 8) - 8)
        in_specs += [pl.BlockSpec((1, p_len, HEAD_DIM), m) for m in (lo, lo, hi, hi)]
        args += [a.reshape(bsz, p_len, -1) for a in past]
    in_specs += [pl.BlockSpec((1, 2, tq, tkd), lambda b, u, i: (0, jnp.minimum(u, 7), 0, 0)),
                 pl.BlockSpec((1, 2, tq, TK), lambda b, u, i: (0, jnp.minimum(u, 7), 0, 0)),
                 pl.BlockSpec((1, HEAD_DIM), lambda b, u, i: (0, 0)),
                 pl.BlockSpec((TK, TK), lambda b, u, i: (0, 0))]
    args += [bd, bp, subln_g.reshape(1, HEAD_DIM), tri]
    kern = functools.partial(_even_attn_kernel, tq=tq, tkd=tkd, n_far_static=n_far_static,
                             has_past=has_past, lam_init=lam_init)
    return pl.pallas_call(
        kern, grid=(bsz, 16, nq), in_specs=in_specs,
        out_specs=pl.BlockSpec((tq, HEAD_DIM), lambda b, u, i: (b * nq + i, u)),
        out_shape=jax.ShapeDtypeStruct((bsz * n, 16 * HEAD_DIM), MXU),
        scratch_shapes=[pltpu.VMEM((2, tq, 1), F32), pltpu.VMEM((2, tq, 1), F32),
                        pltpu.VMEM((2, tq, HEAD_DIM), F32), pltpu.VMEM((tq, 1), F32)],
        compiler_params=_cparams(("arbitrary", "arbitrary", "arbitrary")),
        name="even_attn_past" if has_past else "even_attn",
    )(*args)


INT_MIN = int(np.iinfo(np.int32).min)
NEG_INF_KEY = int(np.array(-np.inf, np.float32).view(np.int32)) ^ 0x7FFFFFFF
INDEX_BITS = 15


def _sort_key(x):
    b = lax.bitcast_convert_type(x, jnp.int32)
    return b ^ ((b >> 31) & 0x7FFFFFFF)


def _indexer_kernel(*refs, tq, n, p_len, top_k, has_past, tkn):
    it = iter(refs)
    qi_ref, wi_ref, kin_ref = next(it), next(it), next(it)
    kip_ref = next(it) if has_past else None
    seln_ref = next(it)
    selp_ref = next(it) if has_past else None
    keyn_ref = next(it)
    keyp_ref = next(it) if has_past else None

    i = pl.program_id(1)
    qi = qi_ref[...]
    wi = wi_ref[...] * N_IDX_HEADS ** -0.5

    def scores(kmat):
        kb = kmat.astype(MXU)
        acc = None
        for ih in range(N_IDX_HEADS):
            d = _dot_nt(qi[:, ih * D_IDX:(ih + 1) * D_IDX], kb) * D_IDX ** -0.5
            term = wi[:, ih:ih + 1] * jnp.maximum(d, 0.0)
            acc = term if acc is None else acc + term
        return acc

    s_new = scores(kin_ref[...])
    qq = i * tq + lax.broadcasted_iota(jnp.int32, (tq, n), 0)
    kk = lax.broadcasted_iota(jnp.int32, (tq, n), 1)
    s_new = jnp.where((kk // CHUNK) <= (qq // CHUNK), s_new, NEG_INF)
    keyn_ref[...] = _sort_key(s_new)
    parts = [(keyn_ref, p_len, n)]
    if has_past:
        keyp_ref[...] = _sort_key(scores(kip_ref[0]))
        parts.append((keyp_ref, 0, p_len))

    def count(pred):
        tot = None
        for ref, base, width in parts:
            idx = base + lax.broadcasted_iota(jnp.int32, (tq, width), 1)
            c = jnp.sum(jnp.where(pred(ref[...], idx), 1.0, 0.0), axis=-1, keepdims=True)
            tot = c if tot is None else tot + c
        return tot

    def body(b, t):
        cand = t + jnp.left_shift(jnp.int32(1), 31 - b)
        c = count(lambda k, idx: k >= cand)
        return jnp.where(c >= top_k, cand, t)

    t = lax.fori_loop(0, 32, body, jnp.full((tq, 1), INT_MIN, jnp.int32))
    n_gt = count(lambda k, idx: k > t)
    n_eq = count(lambda k, idx: k == t)
    need = top_k - n_gt
    tie = jnp.where((n_eq != need) & (t > NEG_INF_KEY), 1.0, 0.0)

    outs = [(parts[0], seln_ref, tkn)] + ([(parts[1], selp_ref, TK)] if has_past else [])

    def write(sel_of):
        for (ref, base, width), out_ref, tk in outs:
            for jb in range(width // tk):
                k = ref[:, jb * tk:(jb + 1) * tk]
                idx = base + jb * tk + lax.broadcasted_iota(jnp.int32, (tq, tk), 1)
                out_ref[jb] = jnp.where(sel_of(k, idx) & (k > NEG_INF_KEY), 1.0, 0.0)

    write(lambda k, idx: k >= t)

    @pl.when(jnp.max(tie) > 0.0)
    def _():
        def jbody(b, jv):
            cand = jv + jnp.left_shift(jnp.int32(1), INDEX_BITS - 1 - b)
            c = count(lambda k, idx: (k == t) & (idx < cand))
            return jnp.where(c < need, cand, jv)

        jv = lax.fori_loop(0, INDEX_BITS, jbody, jnp.zeros((tq, 1), jnp.int32))
        write(lambda k, idx: (k > t) | ((k == t) & (idx <= jv)))


def _indexer(q_i, proj, k_idx_past, *, bsz, n, top_k):
    has_past = k_idx_past is not None
    tq = min(n, 128)
    nq = n // tq
    tkn = min(n, TK)
    p_len = k_idx_past.shape[1] if has_past else 0
    assert n + p_len < 2 ** (INDEX_BITS - 1)
    t = bsz * n
    in_specs = [pl.BlockSpec((tq, N_IDX_HEADS * D_IDX), lambda b, i: (b * nq + i, 0)),
                pl.BlockSpec((tq, LANES), lambda b, i: (b * nq + i, 9)),
                pl.BlockSpec((n, D_IDX), lambda b, i: (b, 8))]
    args = [q_i, proj, proj]
    out_specs = [pl.BlockSpec((n // tkn, tq, tkn), lambda b, i: (0, b * nq + i, 0))]
    out_shape = [jax.ShapeDtypeStruct((n // tkn, t, tkn), F32)]
    scratch = [pltpu.VMEM((tq, n), jnp.int32)]
    if has_past:
        in_specs.append(pl.BlockSpec((1, p_len, D_IDX), lambda b, i: (b, 0, 0)))
        args.append(k_idx_past)
        out_specs.append(pl.BlockSpec((p_len // TK, tq, TK), lambda b, i: (0, b * nq + i, 0)))
        out_shape.append(jax.ShapeDtypeStruct((p_len // TK, t, TK), F32))
        scratch.append(pltpu.VMEM((tq, p_len), jnp.int32))
    res = pl.pallas_call(
        functools.partial(_indexer_kernel, tq=tq, n=n, p_len=p_len, top_k=top_k, has_past=has_past,
                          tkn=tkn),
        grid=(bsz, nq), in_specs=in_specs, out_specs=out_specs, out_shape=out_shape,
        scratch_shapes=scratch,
        compiler_params=_cparams(("arbitrary", "arbitrary")),
        name="indexer_past" if has_past else "indexer",
    )(*args)
    return res[0], (res[1] if has_past else None)


def _sparse_attn_kernel(*refs, tq, tkd, n_far_static, has_past):
    it = iter(refs)
    far_ref, q_ref, kvn_ref, seln_ref = (next(it) for _ in range(4))
    kvp_ref, selp_ref = (next(it), next(it)) if has_past else (None, None)
    bd_ref, bp_ref, o_ref, m_ref, l_ref, acc_ref = (next(it) for _ in range(6))
    nh = N_HEADS_C
    i = pl.program_id(1)
    q = q_ref[...].reshape(nh * tq, KV_LORA)
    m_ref[...] = jnp.full_like(m_ref, NEG_INF)
    l_ref[...] = jnp.zeros_like(l_ref)
    acc_ref[...] = jnp.zeros_like(acc_ref)

    def step(kvb, sel, bias):
        kvb = kvb.astype(MXU)
        tk = kvb.shape[0]
        s = _dot_nt(q, kvb).reshape(nh, tq, tk) * HEAD_DIM ** -0.5 + bias
        s = jnp.where(sel[None] > 0.0, s, NEG_INF)
        m_prev = m_ref[...]
        m_new = jnp.maximum(m_prev, jnp.max(s, axis=-1, keepdims=True))
        m_safe = jnp.where(m_new == NEG_INF, 0.0, m_new)
        alpha = jnp.exp(m_prev - m_safe)
        p = jnp.exp(s - m_safe)
        l_ref[...] = alpha * l_ref[...] + jnp.sum(p, axis=-1, keepdims=True)
        pv = _dot(p.reshape(nh * tq, tk).astype(MXU), kvb).reshape(nh, tq, KV_LORA)
        acc_ref[...] = alpha * acc_ref[...] + pv
        m_ref[...] = m_new

    if has_past:
        def far_body(j, c):
            step(kvp_ref[0, pl.ds(pl.multiple_of(j * TK, TK), TK), :], selp_ref[j], far_ref[...])
            return c

        lax.fori_loop(0, n_far_static, far_body, 0)
        step(kvp_ref[0, pl.ds(n_far_static * TK, TK), :], selp_ref[n_far_static], bp_ref[0])
        step(kvn_ref[...], seln_ref[0], bd_ref[0])
    else:
        jd = (i * tq) // TK

        def far_body(j, c):
            step(kvn_ref[pl.ds(pl.multiple_of(j * TK, TK), TK), :], seln_ref[j], far_ref[...])
            return c

        lax.fori_loop(0, jnp.maximum(jd - 1, 0), far_body, 0)

        @pl.when(jd >= 1)
        def _():
            jp = jnp.maximum(jd - 1, 0)
            step(kvn_ref[pl.ds(pl.multiple_of(jp * TK, TK), TK), :], seln_ref[jp], bp_ref[0])

        step(kvn_ref[pl.ds(pl.multiple_of(jd * TK, TK), TK), :], seln_ref[jd], bd_ref[0])

    o_ref[...] = (acc_ref[...] / l_ref[...]).astype(o_ref.dtype)


def _sparse_attn(q_lat, kv_lat, sel_new, kv_past, sel_past, table, *, bsz, n):
    has_past = kv_past is not None
    nh = N_HEADS_C
    t = bsz * n
    if has_past:
        tq = tkd = n
        q_offsets = [0]
        p_len = kv_past.shape[1]
        n_far_static = p_len // TK - 1
    else:
        tq, tkd = min(n, 128), TK
        assert n % TK == 0
        q_offsets = list(range(0, TK, tq))
        n_far_static = 0
    nq = n // tq
    npar = len(q_offsets)
    far = _bias_far(table, nh).reshape(nh, 1, 1)
    bd = _bias_tiles(table, q_offsets, 0, tq, tkd, nh)
    bp = _bias_tiles(table, q_offsets, -TK, tq, TK, nh)
    in_specs = [pl.BlockSpec((nh, 1, 1), lambda b, i: (0, 0, 0)),
                pl.BlockSpec((nh, tq, KV_LORA), lambda b, i: (0, b * nq + i, 0)),
                pl.BlockSpec((n, KV_LORA), lambda b, i: (b, 0)),
                pl.BlockSpec((n // tkd, tq, tkd), lambda b, i: (0, b * nq + i, 0))]
    args = [far, q_lat, kv_lat, sel_new]
    if has_past:
        in_specs += [pl.BlockSpec((1, p_len, KV_LORA), lambda b, i: (b, 0, 0)),
                     pl.BlockSpec((p_len // TK, tq, TK), lambda b, i: (0, b * nq + i, 0))]
        args += [kv_past, sel_past]
    in_specs += [pl.BlockSpec((1, nh, tq, tkd), lambda b, i: (i % npar, 0, 0, 0)),
                 pl.BlockSpec((1, nh, tq, TK), lambda b, i: (i % npar, 0, 0, 0))]
    args += [bd, bp]
    return pl.pallas_call(
        functools.partial(_sparse_attn_kernel, tq=tq, tkd=tkd, n_far_static=n_far_static,
                          has_past=has_past),
        grid=(bsz, nq), in_specs=in_specs,
        out_specs=pl.BlockSpec((nh, tq, KV_LORA), lambda b, i: (0, b * nq + i, 0)),
        out_shape=jax.ShapeDtypeStruct((nh, t, KV_LORA), MXU),
        scratch_shapes=[pltpu.VMEM((nh, tq, 1), F32), pltpu.VMEM((nh, tq, 1), F32),
                        pltpu.VMEM((nh, tq, KV_LORA), F32)],
        compiler_params=_cparams(("arbitrary", "arbitrary")),
        name="sparse_attn_past" if has_past else "sparse_attn",
    )(*args)


def _tile(n, cap, mult):
    best = None
    for t in range(mult, min(n, cap) + 1, mult):
        if n % t == 0:
            best = t
    assert best is not None, (n, cap, mult)
    return best


def _per_chunk(v, reps):
    return jnp.repeat(v, reps, axis=0)[:, None, :]


def _trunk(x, mod, past, p, w):
    bsz, n, d = x.shape
    t = bsz * n
    x = x.reshape(t, d)
    tm = _tile(t, 512, CHUNK)
    tmn = _tile(t, 256, CHUNK)
    tn_d = _tile(d, 1024, LANES)
    depth = p['w_ada'].shape[0]
    p_len = 0 if past is None else past[0].shape[2]
    top_k = min(TOPK_MAX, (p_len + n) // 4)
    even_rows, odd_rows = [], []
    for l in range(depth):
        m6 = mod[l].reshape(bsz, 6, d)
        sh1, sc1, g1, sh2, sc2, g2 = (_per_chunk(m6[:, k], n // CHUNK) for k in range(6))
        h = _norm(x, p['norm_mix_g'][l], tm=tmn, sc=sc1, sh=sh1, out_dtype=MXU)
        if l % 2 == 0:
            i = l // 2
            lam_init = 0.8 - 0.6 * math.exp(-0.3 * l)
            qkv = _mm(h, p['w_in_even'][i], n_out=48 * LANES, tm=tm, tn=1024)
            past_i = None if past is None else tuple(a[i] for a in past[:4])
            lam = (jnp.exp(jnp.sum(p['lam_q1'][i].astype(F32) * p['lam_k1'][i].astype(F32)))
                   - jnp.exp(jnp.sum(p['lam_q2'][i].astype(F32) * p['lam_k2'][i].astype(F32)))
                   + lam_init)
            o = _even_attn(qkv, past_i, p['rel_bias_table'], lam, p['subln_g'][i], lam_init,
                           bsz=bsz, n=n)
            even_rows.append((qkv[:, 1024:2048].reshape(bsz, n, N_MAPS_A, DA_QK),
                              qkv[:, 2048:3072].reshape(bsz, n, N_HEADS_A, HEAD_DIM),
                              qkv[:, 4096:5120].reshape(bsz, n, N_HEADS_B, HEAD_DIM),
                              qkv[:, 5120:6144].reshape(bsz, n, N_HEADS_B, HEAD_DIM)))
            x = _mm(o, p['w_out_even'][i], n_out=d, tm=tm, tn=tn_d, gate=g1, res=x)
        else:
            j = l // 2
            proj = _mm(h, w['w_in_odd'][j], n_out=10 * LANES, tm=tm, tn=10 * LANES)
            c_q = _norm(proj, p['g_q'][j], tm=tmn, width=Q_LORA, col_block=0, out_dtype=MXU)
            kv_lat = _norm(proj, p['g_kv'][j], tm=tmn, width=KV_LORA, col_block=1)
            k_i = proj[:, 1024:1152]
            q = _mm(c_q, p['w_uq'][j], n_out=N_HEADS_C * HEAD_DIM, tm=tm, tn=1024, out_dtype=MXU)
            q_lat = _mm_heads_out(q, w['w_uk_t'][j], tm=tm)
            q_i = _mm(c_q, p['w_qidx'][j], n_out=N_IDX_HEADS * D_IDX, tm=tm, tn=1024, out_dtype=MXU)
            kv_past = None if past is None else past[4][j]
            ki_past = None if past is None else past[5][j]
            sel_new, sel_past = _indexer(q_i, proj, ki_past, bsz=bsz, n=n, top_k=top_k)
            o_lat = _sparse_attn(q_lat, kv_lat, sel_new, kv_past, sel_past, p['rel_bias_table'],
                                 bsz=bsz, n=n)
            o = _mm_heads_in(o_lat, w['w_uv_h'][j], tm=tm)
            odd_rows.append((kv_lat.reshape(bsz, n, KV_LORA), k_i.reshape(bsz, n, D_IDX)))
            x = _mm(o, p['w_out_odd'][j], n_out=d, tm=tm, tn=tn_d, gate=g1, res=x)
        h2, route = _norm_router(x, p['norm_ffn_g'][l], sc2, sh2, w['w_router'], p['router_bias'], tm=tmn)
        x = _moe(h2, route, w['w_gate'][l], w['w_up'][l], w['w_down'][l], g2, x, tm=tmn)
    y = _norm(x, p['final_norm_g'], tm=tmn).reshape(bsz, n, d)
    ev = tuple(jnp.stack([r[m] for r in even_rows]) for m in range(4))
    od = tuple(jnp.stack([r[m] for r in odd_rows]) for m in range(2))
    return y, ev, od


def kernel(x_prompt, x_sample, cache_a_k, cache_a_v, cache_b_k, cache_b_v, cache_c_kv, cache_c_idx,
           c_prompt, c_sample, rel_bias_table, norm_mix_g, norm_ffn_g, final_norm_g, w_ada, b_ada,
           w_in_even, lam_q1, lam_k1, lam_q2, lam_k2, subln_g, w_out_even, w_in_odd, g_q, g_kv,
           w_uq, w_qidx, w_uk, w_uv, w_out_odd, w_router, router_bias, w_gate, w_up, w_down):
    p = dict(rel_bias_table=rel_bias_table, norm_mix_g=norm_mix_g, norm_ffn_g=norm_ffn_g,
             final_norm_g=final_norm_g, w_ada=w_ada, b_ada=b_ada, w_in_even=w_in_even,
             lam_q1=lam_q1, lam_k1=lam_k1, lam_q2=lam_q2, lam_k2=lam_k2, subln_g=subln_g,
             w_out_even=w_out_even, g_q=g_q, g_kv=g_kv, w_uq=w_uq, w_qidx=w_qidx,
             w_out_odd=w_out_odd, router_bias=router_bias)
    d = x_prompt.shape[-1]
    depth = w_ada.shape[0]
    w = dict(
        w_in_odd=jnp.pad(w_in_odd, ((0, 0), (0, 0), (0, 10 * LANES - w_in_odd.shape[-1]))),
        w_uk_t=jnp.transpose(w_uk, (0, 2, 3, 1)),
        w_uv_h=jnp.transpose(w_uv, (0, 2, 1, 3)),
        w_router=jnp.pad(w_router, ((0, 0), (0, LANES - N_EXPERTS))),
        w_gate=w_gate.astype(MXU), w_up=w_up.astype(MXU), w_down=w_down.astype(MXU))
    nb_p, nb_s = c_prompt.shape[0], c_sample.shape[0]
    c_all = jnp.concatenate([c_prompt, c_sample], axis=0)
    rows = -(-(nb_p + nb_s) // 16) * 16
    c_act = jnp.pad(c_all * (1.0 / (1.0 + jnp.exp(-c_all))), ((0, rows - nb_p - nb_s), (0, 0)))
    tn_ada = _tile(6 * d, 1024, LANES)
    mods = [_mm(c_act, w_ada[l], n_out=6 * d, tm=rows, tn=tn_ada, bias=b_ada[l]) for l in range(depth)]
    mod_p = [m[:nb_p] for m in mods]
    mod_s = [m[nb_p:nb_p + nb_s] for m in mods]

    y_prompt, ev_p, od_p = _trunk(x_prompt, mod_p, None, p, w)
    past = (cache_a_k, cache_a_v, cache_b_k, cache_b_v, cache_c_kv, cache_c_idx)
    y_sample, ev_s, od_s = _trunk(x_sample, mod_s, past, p, w)
    return (y_prompt, y_sample) + ev_p + od_p + ev_s + od_s
```

```python
import functools
import math

import jax
import jax.numpy as jnp
import numpy as np
from jax import lax
from jax.experimental import pallas as pl
from jax.experimental.pallas import tpu as pltpu

F32 = jnp.float32
MXU = jnp.bfloat16

CHUNK = 64
HEAD_DIM = 128
N_HEADS_A = 8
N_MAPS_A = 16
DA_QK = 64
N_HEADS_B = 8
N_HEADS_C = 16
Q_LORA = 512
KV_LORA = 512
N_IDX_HEADS = 16
D_IDX = 128
TOPK_MAX = 256
N_BUCKETS = 32
MAX_DISTANCE = 128
N_EXPERTS = 16
N_GROUPS = 4
EXPERTS_PER_GROUP = 4
EPS = 1e-6
LANES = 128
TK = 256
TQ_SPARSE = 64
TQ_SPARSE_PAST = 32
VMEM_LIMIT = 56 * 1024 * 1024
NEG_INF = float("-inf")


def _cparams(sem):
    return pltpu.CompilerParams(dimension_semantics=sem, vmem_limit_bytes=VMEM_LIMIT)


def _dot(a, b):
    return jnp.dot(a, b, preferred_element_type=F32)


def _dot_nt(a, b):
    return lax.dot_general(a, b, (((1,), (1,)), ((), ())), preferred_element_type=F32)


def _mm_kernel(*refs, has_bias, has_res, tm):
    it = iter(refs)
    a_ref, b_ref = next(it), next(it)
    bias_ref = next(it) if has_bias else None
    gate_ref, res_ref = (next(it), next(it)) if has_res else (None, None)
    o_ref, bsc = next(it), next(it)

    @pl.when(pl.program_id(1) == 0)
    def _():
        bsc[...] = b_ref[...].reshape(bsc.shape).astype(bsc.dtype)

    a = a_ref[...]
    a = a.reshape(a.shape[-2:]).astype(MXU)
    acc = _dot(a, bsc[...])
    if has_bias:
        acc = acc + bias_ref[...]
    if has_res:
        tn = acc.shape[-1]
        acc = (acc.reshape(tm // CHUNK, CHUNK, tn) * gate_ref[...]).reshape(tm, tn)
        acc = acc + res_ref[...]
    o_ref[...] = acc.reshape(o_ref.shape).astype(o_ref.dtype)


def _mm(a, b, *, n_out, tm, tn, bias=None, gate=None, res=None, out_dtype=F32, seg_out=False, name="mm"):
    m, k = a.shape
    assert m % tm == 0 and n_out % tn == 0 and b.shape[0] == k
    in_specs = [pl.BlockSpec((tm, k), lambda j, i: (i, 0)),
                pl.BlockSpec((k, tn), lambda j, i: (0, j))]
    args = [a, b]
    if bias is not None:
        in_specs.append(pl.BlockSpec((1, tn), lambda j, i: (0, j)))
        args.append(bias.reshape(1, n_out))
    if res is not None:
        in_specs.append(pl.BlockSpec((tm // CHUNK, 1, tn), lambda j, i: (i, 0, j)))
        in_specs.append(pl.BlockSpec((tm, tn), lambda j, i: (i, j)))
        args += [gate, res]
    return pl.pallas_call(
        functools.partial(_mm_kernel, has_bias=bias is not None, has_res=res is not None, tm=tm),
        grid=(n_out // tn, m // tm), in_specs=in_specs,
        out_specs=(pl.BlockSpec((1, tm, tn), lambda j, i: (j, i, 0)) if seg_out
                   else pl.BlockSpec((tm, tn), lambda j, i: (i, j))),
        out_shape=jax.ShapeDtypeStruct((n_out // tn, m, tn) if seg_out else (m, n_out), out_dtype),
        scratch_shapes=[pltpu.VMEM((k, tn), MXU)],
        compiler_params=_cparams(("arbitrary", "arbitrary")), name=name,
    )(*args)


def _mm_heads_out(a, b_h, *, tm):
    m = a.shape[0]
    nh, ka, n = b_h.shape
    return pl.pallas_call(
        functools.partial(_mm_kernel, has_bias=False, has_res=False, tm=tm),
        grid=(nh, m // tm),
        in_specs=[pl.BlockSpec((tm, ka), lambda h, i: (i, h)),
                  pl.BlockSpec((1, ka, n), lambda h, i: (h, 0, 0))],
        out_specs=pl.BlockSpec((1, tm, n), lambda h, i: (h, i, 0)),
        out_shape=jax.ShapeDtypeStruct((nh, m, n), MXU),
        scratch_shapes=[pltpu.VMEM((ka, n), MXU)],
        compiler_params=_cparams(("arbitrary", "arbitrary")), name="mm_heads_out",
    )(a, b_h)


def _mm_heads_in(a_h, b_h, *, tm):
    nh, m, ka = a_h.shape
    n = b_h.shape[2]
    return pl.pallas_call(
        functools.partial(_mm_kernel, has_bias=False, has_res=False, tm=tm),
        grid=(nh, m // tm),
        in_specs=[pl.BlockSpec((1, tm, ka), lambda h, i: (h, i, 0)),
                  pl.BlockSpec((1, ka, n), lambda h, i: (h, 0, 0))],
        out_specs=pl.BlockSpec((tm, n), lambda h, i: (i, h)),
        out_shape=jax.ShapeDtypeStruct((m, nh * n), MXU),
        scratch_shapes=[pltpu.VMEM((ka, n), MXU)],
        compiler_params=_cparams(("arbitrary", "arbitrary")), name="mm_heads_in",
    )(a_h, b_h)


def _rms(x, g):
    return x * lax.rsqrt(jnp.mean(x * x, axis=-1, keepdims=True) + EPS) * g


def _modulate(y, sc_ref, sh_ref, tm):
    d = y.shape[-1]
    y3 = y.reshape(tm // CHUNK, CHUNK, d)
    y3 = y3 * (1.0 + sc_ref[...]) + sh_ref[...]
    return y3.reshape(tm, d)


def _norm_kernel(*refs, has_mod, tm):
    if has_mod:
        x_ref, g_ref, sc_ref, sh_ref, o_ref = refs
    else:
        x_ref, g_ref, o_ref = refs
    y = _rms(x_ref[...], g_ref[...])
    if has_mod:
        y = _modulate(y, sc_ref, sh_ref, tm)
    o_ref[...] = y.astype(o_ref.dtype)


def _norm(x, g, *, tm, width=None, col_block=0, sc=None, sh=None, out_dtype=F32):
    width = x.shape[1] if width is None else width
    rows = x.shape[0]
    assert rows % tm == 0
    has_mod = sc is not None
    in_specs = [pl.BlockSpec((tm, width), lambda i: (i, col_block)),
                pl.BlockSpec((1, width), lambda i: (0, 0))]
    args = [x, g.reshape(1, width)]
    if has_mod:
        in_specs += [pl.BlockSpec((tm // CHUNK, 1, width), lambda i: (i, 0, 0))] * 2
        args += [sc, sh]
    return pl.pallas_call(
        functools.partial(_norm_kernel, has_mod=has_mod, tm=tm),
        grid=(rows // tm,), in_specs=in_specs,
        out_specs=pl.BlockSpec((tm, width), lambda i: (i, 0)),
        out_shape=jax.ShapeDtypeStruct((rows, width), out_dtype),
        compiler_params=_cparams(("arbitrary",)), name="norm",
    )(*args)


def _split2(x):
    hi = x.astype(MXU)
    lo = (x - hi.astype(F32)).astype(MXU)
    return hi, lo


def _route(logits, rb_ref):
    tm = logits.shape[0]
    aff = [1.0 / (1.0 + jnp.exp(-logits[:, e:e + 1])) for e in range(N_EXPERTS)]
    sc = [aff[e] + rb_ref[e] for e in range(N_EXPERTS)]
    npg = EXPERTS_PER_GROUP
    gscore = []
    for g in range(N_GROUPS):
        v = sc[g * npg:(g + 1) * npg]
        best = None
        for a in range(npg):
            for b in range(a + 1, npg):
                s = v[a] + v[b]
                best = s if best is None else jnp.maximum(best, s)
        gscore.append(best)
    gb = jnp.zeros((tm, 1), jnp.int32)
    gv = gscore[0]
    for g in range(1, N_GROUPS):
        better = gscore[g] > gv
        gb = jnp.where(better, g, gb)
        gv = jnp.where(better, gscore[g], gv)
    u = [sc[j] for j in range(npg)]
    a4 = [aff[j] for j in range(npg)]
    for g in range(1, N_GROUPS):
        pick = gb == g
        u = [jnp.where(pick, sc[g * npg + j], u[j]) for j in range(npg)]
        a4 = [jnp.where(pick, aff[g * npg + j], a4[j]) for j in range(npg)]
    i1 = jnp.zeros((tm, 1), jnp.int32)
    v1 = u[0]
    for j in range(1, npg):
        better = u[j] > v1
        i1 = jnp.where(better, j, i1)
        v1 = jnp.where(better, u[j], v1)
    i2 = jnp.full((tm, 1), -1, jnp.int32)
    v2 = jnp.full((tm, 1), NEG_INF, F32)
    for j in range(npg):
        better = (i1 != j) & ((u[j] > v2) | (i2 < 0))
        i2 = jnp.where(better, j, i2)
        v2 = jnp.where(better, u[j], v2)
    w1 = a4[0]
    w2 = a4[0]
    for j in range(1, npg):
        w1 = jnp.where(i1 == j, a4[j], w1)
        w2 = jnp.where(i2 == j, a4[j], w2)
    tot = w1 + w2
    w1 = w1 / tot
    w2 = w2 / tot
    e1 = gb * npg + i1
    e2 = gb * npg + i2
    lane = lax.broadcasted_iota(jnp.int32, (tm, LANES), 1)
    out = jnp.where(lane == 0, e1.astype(F32), 0.0)
    out = jnp.where(lane == 1, e2.astype(F32), out)
    out = jnp.where(lane == 2, w1, out)
    return jnp.where(lane == 3, w2, out)


def _norm_router_kernel(x_ref, g_ref, sc_ref, sh_ref, wr_ref, rb_ref, h_ref, route_ref, *, tm):
    h = _modulate(_rms(x_ref[...], g_ref[...]), sc_ref, sh_ref, tm)
    h_ref[...] = h
    hh, hl = _split2(h)
    wh, wl = _split2(wr_ref[...])
    logits = _dot(hh, wh) + (_dot(hl, wh) + _dot(hh, wl))
    route_ref[...] = _route(logits, rb_ref)


def _norm_router(x, g, sc, sh, w_router_pad, router_bias, *, tm):
    t, d = x.shape
    return pl.pallas_call(
        functools.partial(_norm_router_kernel, tm=tm),
        grid=(t // tm,),
        in_specs=[pl.BlockSpec((tm, d), lambda i: (i, 0)),
                  pl.BlockSpec((1, d), lambda i: (0, 0)),
                  pl.BlockSpec((tm // CHUNK, 1, d), lambda i: (i, 0, 0)),
                  pl.BlockSpec((tm // CHUNK, 1, d), lambda i: (i, 0, 0)),
                  pl.BlockSpec((d, LANES), lambda i: (0, 0)),
                  pl.BlockSpec(memory_space=pltpu.SMEM)],
        out_specs=[pl.BlockSpec((tm, d), lambda i: (i, 0)),
                   pl.BlockSpec((tm, LANES), lambda i: (i, 0))],
        out_shape=[jax.ShapeDtypeStruct((t, d), F32), jax.ShapeDtypeStruct((t, LANES), F32)],
        compiler_params=_cparams(("arbitrary",)), name="norm_router",
    )(x, g.reshape(1, d), sc, sh, w_router_pad, router_bias)


TM_EXPERT = 256
TM_DISPATCH = 256


def _row_copy(src, src_row, dst, dst_row, sem):
    return pltpu.make_async_copy(src.at[pl.ds(src_row, 1), :], dst.at[pl.ds(dst_row, 1), :], sem)


def _rows_wait(src, dst, sem, n):
    pltpu.make_async_copy(src.at[pl.ds(0, n), :], dst.at[pl.ds(0, n), :], sem).wait()


def _dispatch_kernel(dest_ref, pad_ref, h_hbm, xs_hbm, sems, *, tb, t, n_pad):
    i = pl.program_id(0)
    sem = sems.at[0]

    def body(r, c):
        tok = i * tb + r
        _row_copy(h_hbm, tok, xs_hbm, dest_ref[tok], sem).start()
        _row_copy(h_hbm, tok, xs_hbm, dest_ref[t + tok], sem).start()
        return c

    lax.fori_loop(0, tb, body, 0, unroll=8)
    for _ in range(2):
        _rows_wait(h_hbm, xs_hbm, sem, tb)

    @pl.when(i == 0)
    def _():
        for c0 in range(0, n_pad, tb):
            nc = min(tb, n_pad - c0)

            def pad_body(r, c, c0=c0):
                _row_copy(h_hbm, 0, xs_hbm, pad_ref[c0 + r], sem).start()
                return c

            lax.fori_loop(0, nc, pad_body, 0, unroll=8)
            _rows_wait(h_hbm, xs_hbm, sem, nc)


def _expert_kernel(te_ref, nv_ref, x_ref, wg_ref, wu_ref, wd_ref, y_ref):
    del te_ref
    i = pl.program_id(0)

    @pl.when(i < nv_ref[0])
    def _():
        x = x_ref[...].astype(MXU)
        a = _dot(x, wg_ref[0])
        b = _dot(x, wu_ref[0])
        he = (a / (1.0 + jnp.exp(-a))) * b
        y_ref[...] = _dot(he.astype(MXU), wd_ref[0])

    @pl.when(i >= nv_ref[0])
    def _():
        y_ref[...] = jnp.zeros_like(y_ref)


def _combine_kernel(dest_ref, y_hbm, route_ref, res_ref, g2_ref, o_ref, ybuf, sem, *, tm, t):
    i = pl.program_id(0)
    n = pl.num_programs(0)
    slot = i % 2

    def start(step, s):
        for j in range(2):
            def body(r, c):
                _row_copy(y_hbm, dest_ref[j * t + step * tm + r], ybuf.at[s, j], r, sem.at[s]).start()
                return c

            lax.fori_loop(0, tm, body, 0, unroll=8)

    @pl.when(i == 0)
    def _():
        start(0, 0)

    @pl.when(i + 1 < n)
    def _():
        start(i + 1, 1 - slot)

    for j in range(2):
        _rows_wait(y_hbm, ybuf.at[slot, j], sem.at[slot], tm)
    d = o_ref.shape[-1]
    route = route_ref[...]
    y = route[:, 2:3] * ybuf[slot, 0] + route[:, 3:4] * ybuf[slot, 1]
    o_ref[...] = res_ref[...] + (y.reshape(tm // CHUNK, CHUNK, d) * g2_ref[...]).reshape(tm, d)


def _cumsum_rows(x, blk):
    n, e = x.shape
    xb = x.reshape(n // blk, blk, e)
    tri = jnp.asarray(np.tril(np.ones((blk, blk), np.float32)))
    local = jnp.einsum('ij,bje->bie', tri, xb, precision=lax.Precision.HIGHEST)
    tot = local[:, -1, :]
    return (local + (jnp.cumsum(tot, axis=0) - tot)[:, None, :]).reshape(n, e)


def _moe(h, route, w_gate, w_up, w_down, g2, res, *, tm):
    t, d = h.shape
    de = w_gate.shape[-1]
    te = _tile(2 * t, TM_EXPERT, CHUNK)
    n_tiles = 2 * t // te + N_EXPERTS
    r_pad = n_tiles * te
    n_pad = r_pad - 2 * t
    e_flat = route[:, 0:2].T.reshape(-1)
    experts = jnp.arange(N_EXPERTS, dtype=F32)
    onehot = (e_flat[:, None] == experts[None, :]).astype(F32)
    csum = _cumsum_rows(onehot, _tile(2 * t, 256, 8))
    counts = csum[-1]
    tiles_e = jnp.ceil(counts / te)
    tile_end = jnp.cumsum(tiles_e)
    seg_start = (tile_end - tiles_e) * te
    dest = jnp.sum(onehot * (seg_start[None, :] + csum - 1.0), axis=1).astype(jnp.int32)
    gap_start = jnp.concatenate([seg_start + counts, tile_end[-1:] * te])
    gap_len = jnp.concatenate([tiles_e * te - counts, r_pad - tile_end[-1:] * te])
    gap_first = jnp.cumsum(gap_len) - gap_len
    k = jnp.arange(n_pad, dtype=F32)[:, None]
    in_gap = (k >= gap_first[None, :]) & (k < (gap_first + gap_len)[None, :])
    pad_rows = jnp.sum(jnp.where(in_gap, gap_start[None, :] + k - gap_first[None, :], 0.0),
                       axis=1).astype(jnp.int32)
    tile_id = jnp.arange(n_tiles, dtype=F32)[:, None]
    tile_e = jnp.minimum(jnp.sum((tile_end[None, :] <= tile_id).astype(jnp.int32), axis=1),
                         N_EXPERTS - 1)
    n_valid = tile_end[-1:].astype(jnp.int32)

    tb = _tile(t, TM_DISPATCH, CHUNK)
    xs = pl.pallas_call(
        functools.partial(_dispatch_kernel, tb=tb, t=t, n_pad=n_pad),
        grid_spec=pltpu.PrefetchScalarGridSpec(
            num_scalar_prefetch=2, grid=(t // tb,),
            in_specs=[pl.BlockSpec(memory_space=pl.ANY)],
            out_specs=pl.BlockSpec(memory_space=pl.ANY),
            scratch_shapes=[pltpu.SemaphoreType.DMA((1,))]),
        out_shape=jax.ShapeDtypeStruct((r_pad, d), F32),
        compiler_params=_cparams(("arbitrary",)), name="moe_dispatch",
    )(dest, pad_rows, h)

    y = pl.pallas_call(
        _expert_kernel,
        grid_spec=pltpu.PrefetchScalarGridSpec(
            num_scalar_prefetch=2, grid=(n_tiles,),
            in_specs=[pl.BlockSpec((te, d), lambda i, e, n: (i, 0)),
                      pl.BlockSpec((1, d, de), lambda i, e, n: (e[i], 0, 0)),
                      pl.BlockSpec((1, d, de), lambda i, e, n: (e[i], 0, 0)),
                      pl.BlockSpec((1, de, d), lambda i, e, n: (e[i], 0, 0))],
            out_specs=pl.BlockSpec((te, d), lambda i, e, n: (i, 0))),
        out_shape=jax.ShapeDtypeStruct((r_pad, d), F32),
        compiler_params=_cparams(("arbitrary",)), name="moe_experts",
    )(tile_e, n_valid, xs, w_gate, w_up, w_down)

    return pl.pallas_call(
        functools.partial(_combine_kernel, tm=tm, t=t),
        grid_spec=pltpu.PrefetchScalarGridSpec(
            num_scalar_prefetch=1, grid=(t // tm,),
            in_specs=[pl.BlockSpec(memory_space=pl.ANY),
                      pl.BlockSpec((tm, LANES), lambda i, dst: (i, 0)),
                      pl.BlockSpec((tm, d), lambda i, dst: (i, 0)),
                      pl.BlockSpec((tm // CHUNK, 1, d), lambda i, dst: (i, 0, 0))],
            out_specs=pl.BlockSpec((tm, d), lambda i, dst: (i, 0)),
            scratch_shapes=[pltpu.VMEM((2, 2, tm, d), F32), pltpu.SemaphoreType.DMA((2,))]),
        out_shape=jax.ShapeDtypeStruct((t, d), F32),
        compiler_params=_cparams(("arbitrary",)), name="moe_combine",
    )(dest, y, route, res, g2)


def _rel_bucket(rel):
    half = N_BUCKETS // 2
    max_exact = half // 2
    n = jnp.abs(rel)
    nf = jnp.maximum(n, 1).astype(F32)
    large = max_exact + (jnp.log(nf / max_exact) / math.log(MAX_DISTANCE / max_exact)
                         * (half - max_exact)).astype(jnp.int32)
    large = jnp.minimum(large, half - 1)
    return jnp.where(rel > 0, half, 0) + jnp.where(n < max_exact, n, large)


def _bias_tiles(table, q_offsets, k_offset, tq, tk, n_heads):
    tiles = []
    for q0 in q_offsets:
        rel = (k_offset + np.arange(tk))[None, :] - (q0 + np.arange(tq))[:, None]
        bucket = _rel_bucket(jnp.asarray(rel, jnp.int32))
        onehot = (bucket[..., None] == jnp.arange(N_BUCKETS, dtype=jnp.int32)).astype(F32)
        tiles.append(jnp.einsum('qkb,bh->hqk', onehot, table[:, :n_heads].astype(F32),
                                precision=lax.Precision.HIGHEST))
    return jnp.stack(tiles)


def _bias_far(table, n_heads):
    rel = jnp.full((1,), -(MAX_DISTANCE + 1), jnp.int32)
    return table[_rel_bucket(rel)][0, :n_heads].astype(F32)


def _diff_attend(qr, far, prev, diag, bias_far, bp_ref, bd_ref, vis):
    outs = []
    for r in range(2):
        cols = slice(r * DA_QK, (r + 1) * DA_QK)
        parts = []
        if far is not None:
            s = _dot_nt(qr[r], far[0][:, cols].astype(MXU)) * DA_QK ** -0.5 + bias_far[r]
            parts.append((s, far[1]))
        if prev is not None:
            s = _dot_nt(qr[r], prev[0][:, cols].astype(MXU)) * DA_QK ** -0.5 + bp_ref[0, r]
            parts.append((s, prev[1]))
        s = _dot_nt(qr[r], diag[0][:, cols].astype(MXU)) * DA_QK ** -0.5 + bd_ref[0, r]
        parts.append((jnp.where(vis, s, NEG_INF), diag[1]))
        m = None
        for s, _ in parts:
            mx = jnp.max(s, axis=-1, keepdims=True)
            m = mx if m is None else jnp.maximum(m, mx)
        l = None
        acc = None
        for s, v in parts:
            p = jnp.exp(s - m)
            ps = jnp.sum(p, axis=-1, keepdims=True)
            pv = _dot(p.astype(MXU), v.astype(MXU))
            l = ps if l is None else l + ps
            acc = pv if acc is None else acc + pv
        outs.append(acc / l)
    return outs


def _stick_attend(qb, blocks, tri_ref, trid_ref):
    carry = None
    acc = None
    for kb, vb, mask in blocks:
        tk = kb.shape[0]
        z = _dot_nt(qb, kb.astype(MXU)) * HEAD_DIM ** -0.5
        log_beta = jnp.minimum(z, 0.0) - jnp.log1p(jnp.exp(-jnp.abs(z)))
        log_keep = log_beta - z
        if mask is not None:
            log_keep = jnp.where(mask, log_keep, 0.0)
        tri = (trid_ref if tk != TK else tri_ref)[...]
        hi, lo = _split2(log_keep)
        r = _dot(hi, tri) + _dot(lo, tri)
        later = r[:, LANES:]
        if carry is not None:
            later = later + (carry[:, :tk] if tk <= LANES else
                             jnp.concatenate([carry] * (tk // LANES), axis=1))
        a = jnp.exp(log_beta + later)
        if mask is not None:
            a = jnp.where(mask, a, 0.0)
        pv = _dot(a.astype(MXU), vb.astype(MXU))
        acc = pv if acc is None else acc + pv
        carry = r[:, :LANES] if carry is None else carry + r[:, :LANES]
    return acc


def _even_attn_kernel(*refs, tq, tkd, nq, p_len, lam_init):
    has_past = p_len > 0
    it = iter(refs)
    far_ref, lam_ref, q_ref, kn_ref, vn_ref = (next(it) for _ in range(5))
    cache = [next(it) for _ in range(4)] if has_past else None
    bd_ref, bp_ref, g_ref, tri_ref, trid_ref, o_ref = (next(it) for _ in range(6))

    u = pl.program_id(1)
    i = pl.program_id(2)
    q = q_ref[0]
    qq = lax.broadcasted_iota(jnp.int32, (tq, tkd), 0)
    kk = lax.broadcasted_iota(jnp.int32, (tq, tkd), 1)

    def regions(c, kp_ref, vp_ref):
        if has_past:
            far = (kp_ref, vp_ref, 0, p_len - TK) if p_len > TK else None
            prev = (kp_ref, vp_ref, p_len - TK, p_len)
            diag = (kn_ref, vn_ref, 0, tkd)
        else:
            far = (kn_ref, vn_ref, 0, (c - 1) * TK) if c >= 2 else None
            prev = (kn_ref, vn_ref, (c - 1) * TK, c * TK) if c >= 1 else None
            diag = (kn_ref, vn_ref, c * TK, c * TK + tkd)
        return far, prev, diag

    def load(reg, lo=None, hi=None):
        kref, vref, a, b = reg
        lo, hi = (a, b) if lo is None else (a + lo, a + hi)
        return kref[0, lo:hi, :], vref[0, lo:hi, :]

    def diff(c):
        far, prev, diag = regions(c, *(cache[0:2] if has_past else (None, None)))
        qr = [q[:, r * DA_QK:(r + 1) * DA_QK].astype(MXU) for r in range(2)]
        vis = (kk // CHUNK) <= (qq // CHUNK)
        o0, o1 = _diff_attend(qr, None if far is None else load(far), None if prev is None else load(prev),
                              load(diag), [far_ref[2 * u], far_ref[2 * u + 1]], bp_ref, bd_ref, vis)
        o = _rms(o0 - lam_ref[0] * o1, g_ref[...]) * (1.0 - lam_init)
        o_ref[...] = o.astype(o_ref.dtype)

    def stick(c):
        far, prev, diag = regions(c, *(cache[2:4] if has_past else (None, None)))
        blocks = [load(diag) + (kk < qq,)]
        if prev is not None:
            blocks.append(load(prev) + (None,))
        if far is not None:
            for j in range((far[3] - far[2]) // TK - 1, -1, -1):
                blocks.append(load(far, j * TK, (j + 1) * TK) + (None,))
        o_ref[...] = _stick_attend(q.astype(MXU), blocks, tri_ref, trid_ref).astype(o_ref.dtype)

    for c in range(nq):
        pl.when((u < N_HEADS_A) & (i == c))(functools.partial(diff, c))
        pl.when((u >= N_HEADS_A) & (i == c))(functools.partial(stick, c))


def _tri_ones(tk):
    later = np.arange(tk)[:, None] > np.arange(tk)[None, :]
    return jnp.asarray(np.concatenate([np.ones((tk, LANES), bool), later], axis=1), MXU)


def _even_attn(qkv, past, table, lam, subln_g, lam_init, *, bsz, n):
    has_past = past is not None
    tq = min(n, TK)
    assert n % tq == 0 and (has_past or tq == TK) and (not has_past or n == tq)
    tkd = tq
    nq = n // tq
    p_len = past[0].shape[1] if has_past else 0
    far = _bias_far(table, N_MAPS_A)
    bd = _bias_tiles(table, [0], 0, tq, tkd, N_MAPS_A)
    bp = _bias_tiles(table, [0], -TK, tq, TK, N_MAPS_A)

    def seg(u, base):
        return jnp.where(u < 8, base, base + 3)

    smem = pl.BlockSpec(memory_space=pltpu.SMEM)
    in_specs = [smem, smem,
                pl.BlockSpec((1, tq, HEAD_DIM), lambda b, u, i: (seg(u, 0), b * nq + i, u % 8)),
                pl.BlockSpec((1, n, HEAD_DIM), lambda b, u, i: (seg(u, 1), b, u % 8)),
                pl.BlockSpec((1, n, HEAD_DIM), lambda b, u, i: (seg(u, 2), b, u % 8))]
    args = [far, lam.reshape(1), qkv, qkv, qkv]
    if has_past:
        assert p_len % TK == 0 and p_len >= TK
        lo = lambda b, u, i: (b, 0, jnp.minimum(u, 7))
        hi = lambda b, u, i: (b, 0, jnp.maximum(u, 8) - 8)
        in_specs += [pl.BlockSpec((1, p_len, HEAD_DIM), m) for m in (lo, lo, hi, hi)]
        args += [a.reshape(bsz, p_len, -1) for a in past]
    in_specs += [pl.BlockSpec((1, 2, tq, tkd), lambda b, u, i: (0, jnp.minimum(u, 7), 0, 0)),
                 pl.BlockSpec((1, 2, tq, TK), lambda b, u, i: (0, jnp.minimum(u, 7), 0, 0)),
                 pl.BlockSpec((1, HEAD_DIM), lambda b, u, i: (0, 0)),
                 pl.BlockSpec((TK, LANES + TK), lambda b, u, i: (0, 0)),
                 pl.BlockSpec((tkd, LANES + tkd), lambda b, u, i: (0, 0))]
    args += [bd, bp, subln_g.reshape(1, HEAD_DIM), _tri_ones(TK), _tri_ones(tkd)]
    kern = functools.partial(_even_attn_kernel, tq=tq, tkd=tkd, nq=nq, p_len=p_len, lam_init=lam_init)
    return pl.pallas_call(
        kern, grid=(bsz, 16, nq), in_specs=in_specs,
        out_specs=pl.BlockSpec((tq, HEAD_DIM), lambda b, u, i: (b * nq + i, u)),
        out_shape=jax.ShapeDtypeStruct((bsz * n, 16 * HEAD_DIM), MXU),
        compiler_params=_cparams(("arbitrary", "arbitrary", "arbitrary")),
        name="even_attn_past" if has_past else "even_attn",
    )(*args)


INT_MIN = int(np.iinfo(np.int32).min)
NEG_INF_KEY = int(np.array(-np.inf, np.float32).view(np.int32)) ^ 0x7FFFFFFF
INDEX_BITS = 15


def _sort_key(x):
    b = lax.bitcast_convert_type(x, jnp.int32)
    return b ^ ((b >> 31) & 0x7FFFFFFF)


def _indexer_kernel(*refs, tq, n, p_len, top_k):
    has_past = p_len > 0
    it = iter(refs)
    qi_ref, wi_ref, kin_ref = next(it), next(it), next(it)
    kip_ref = next(it) if has_past else None
    seln_ref = next(it)
    selp_ref = next(it) if has_past else None
    keyn_ref = next(it)
    keyp_ref = next(it) if has_past else None

    i = pl.program_id(1)
    qi = qi_ref[...]
    wi = wi_ref[...] * N_IDX_HEADS ** -0.5

    def scores(kmat):
        kb = kmat.astype(MXU)
        acc = None
        for ih in range(N_IDX_HEADS):
            d = _dot_nt(qi[:, ih * D_IDX:(ih + 1) * D_IDX], kb) * D_IDX ** -0.5
            term = wi[:, ih:ih + 1] * jnp.maximum(d, 0.0)
            acc = term if acc is None else acc + term
        return acc

    def run(w_new):
        s_new = scores(kin_ref[:w_new, :])
        qq = i * tq + lax.broadcasted_iota(jnp.int32, (tq, w_new), 0)
        kk = lax.broadcasted_iota(jnp.int32, (tq, w_new), 1)
        s_new = jnp.where((kk // CHUNK) <= (qq // CHUNK), s_new, NEG_INF)
        keyn_ref[:, :w_new] = _sort_key(s_new)
        parts = [(keyn_ref, p_len, w_new, seln_ref)]
        if has_past:
            keyp_ref[...] = _sort_key(scores(kip_ref[0]))
            parts.append((keyp_ref, 0, p_len, selp_ref))

        def count(pred):
            tot = None
            for ref, base, width, _ in parts:
                idx = base + lax.broadcasted_iota(jnp.int32, (tq, width), 1)
                c = jnp.sum(jnp.where(pred(ref[:, :width], idx), 1.0, 0.0), axis=-1, keepdims=True)
                tot = c if tot is None else tot + c
            return tot

        def body(b, t):
            cand = t + jnp.left_shift(jnp.int32(1), 31 - b)
            c = count(lambda k, idx: k >= cand)
            return jnp.where(c >= top_k, cand, t)

        t = lax.fori_loop(0, 32, body, jnp.full((tq, 1), INT_MIN, jnp.int32))
        n_gt = count(lambda k, idx: k > t)
        n_eq = count(lambda k, idx: k == t)
        need = top_k - n_gt
        tie = jnp.where((n_eq != need) & (t > NEG_INF_KEY), 1.0, 0.0)

        def write(sel_of):
            for ref, base, width, out_ref in parts:
                k = ref[:, :width]
                idx = base + lax.broadcasted_iota(jnp.int32, (tq, width), 1)
                out_ref[:, :width] = jnp.where(sel_of(k, idx) & (k > NEG_INF_KEY), 1.0, 0.0)

        write(lambda k, idx: k >= t)
        if w_new < n:
            seln_ref[:, w_new:] = jnp.zeros((tq, n - w_new), F32)

        @pl.when(jnp.max(tie) > 0.0)
        def _():
            def jbody(b, jv):
                cand = jv + jnp.left_shift(jnp.int32(1), INDEX_BITS - 1 - b)
                c = count(lambda k, idx: (k == t) & (idx < cand))
                return jnp.where(c < need, cand, jv)

            jv = lax.fori_loop(0, INDEX_BITS, jbody, jnp.zeros((tq, 1), jnp.int32))
            write(lambda k, idx: (k > t) | ((k == t) & (idx <= jv)))

    if has_past or n <= TK:
        run(n)
    else:
        for c in range(n // TK):
            pl.when((i * tq) // TK == c)(functools.partial(run, (c + 1) * TK))


def _indexer(q_i, proj, k_idx_past, *, bsz, n, top_k):
    has_past = k_idx_past is not None
    tq = min(n, 128)
    nq = n // tq
    p_len = k_idx_past.shape[1] if has_past else 0
    assert n + p_len < 2 ** (INDEX_BITS - 1) and (n <= TK or n % TK == 0)
    t = bsz * n
    in_specs = [pl.BlockSpec((tq, N_IDX_HEADS * D_IDX), lambda b, i: (b * nq + i, 0)),
                pl.BlockSpec((tq, LANES), lambda b, i: (b * nq + i, 9)),
                pl.BlockSpec((n, D_IDX), lambda b, i: (b, 8))]
    args = [q_i, proj, proj]
    out_specs = [pl.BlockSpec((tq, n), lambda b, i: (b * nq + i, 0))]
    out_shape = [jax.ShapeDtypeStruct((t, n), F32)]
    scratch = [pltpu.VMEM((tq, n), jnp.int32)]
    if has_past:
        in_specs.append(pl.BlockSpec((1, p_len, D_IDX), lambda b, i: (b, 0, 0)))
        args.append(k_idx_past)
        out_specs.append(pl.BlockSpec((tq, p_len), lambda b, i: (b * nq + i, 0)))
        out_shape.append(jax.ShapeDtypeStruct((t, p_len), F32))
        scratch.append(pltpu.VMEM((tq, p_len), jnp.int32))
    res = pl.pallas_call(
        functools.partial(_indexer_kernel, tq=tq, n=n, p_len=p_len, top_k=top_k),
        grid=(bsz, nq), in_specs=in_specs, out_specs=out_specs, out_shape=out_shape,
        scratch_shapes=scratch,
        compiler_params=_cparams(("arbitrary", "arbitrary")),
        name="indexer_past" if has_past else "indexer",
    )(*args)
    return res[0], (res[1] if has_past else None)


def _sparse_attn_kernel(*refs, tq, tkd, n, p_len):
    has_past = p_len > 0
    it = iter(refs)
    far_ref, q_ref, kvn_ref, seln_ref = (next(it) for _ in range(4))
    kvp_ref, selp_ref = (next(it), next(it)) if has_past else (None, None)
    bd_ref, bp_ref, o_ref = (next(it) for _ in range(3))
    nh = N_HEADS_C
    i = pl.program_id(1)
    q = q_ref[...].reshape(nh * tq, KV_LORA)

    def attend(regions):
        parts = []
        for kvb, sel, bias in regions:
            kvb = kvb.astype(MXU)
            w = kvb.shape[0]
            s = _dot_nt(q, kvb).reshape(nh, tq, w) * HEAD_DIM ** -0.5 + bias
            parts.append((jnp.where(sel[None] > 0.0, s, NEG_INF), kvb))
        m = None
        for s, _ in parts:
            mx = jnp.max(s, axis=-1, keepdims=True)
            m = mx if m is None else jnp.maximum(m, mx)
        l = None
        acc = None
        for s, kvb in parts:
            w = kvb.shape[0]
            p = jnp.exp(s - m)
            ps = jnp.sum(p, axis=-1, keepdims=True)
            pv = _dot(p.reshape(nh * tq, w).astype(MXU), kvb)
            l = ps if l is None else l + ps
            acc = pv if acc is None else acc + pv
        o_ref[...] = (acc.reshape(nh, tq, KV_LORA) / l).astype(o_ref.dtype)

    if has_past:
        regions = []
        if p_len > TK:
            regions.append((kvp_ref[0, :p_len - TK, :], selp_ref[:, :p_len - TK], far_ref[...]))
        regions.append((kvp_ref[0, p_len - TK:, :], selp_ref[:, p_len - TK:], bp_ref[0]))
        regions.append((kvn_ref[...], seln_ref[...], bd_ref[0]))
        attend(regions)
    else:
        def run(c):
            regions = []
            if c >= 2:
                regions.append((kvn_ref[:(c - 1) * TK, :], seln_ref[:, :(c - 1) * TK], far_ref[...]))
            if c >= 1:
                regions.append((kvn_ref[(c - 1) * TK:c * TK, :], seln_ref[:, (c - 1) * TK:c * TK], bp_ref[0]))
            regions.append((kvn_ref[c * TK:(c + 1) * TK, :], seln_ref[:, c * TK:(c + 1) * TK], bd_ref[0]))
            attend(regions)

        for c in range(n // TK):
            pl.when((i * tq) // TK == c)(functools.partial(run, c))


def _sparse_attn(q_lat, kv_lat, sel_new, kv_past, sel_past, table, *, bsz, n):
    has_past = kv_past is not None
    nh = N_HEADS_C
    t = bsz * n
    if has_past:
        tq, tkd = min(n, TQ_SPARSE_PAST), n
        p_len = kv_past.shape[1]
        assert p_len % TK == 0 and n <= TK
    else:
        tq, tkd = min(n, TQ_SPARSE), TK
        p_len = 0
        assert n % TK == 0
    q_offsets = list(range(0, tkd, tq))
    nq = n // tq
    npar = len(q_offsets)
    far = _bias_far(table, nh).reshape(nh, 1, 1)
    bd = _bias_tiles(table, q_offsets, 0, tq, tkd, nh)
    bp = _bias_tiles(table, q_offsets, -TK, tq, TK, nh)
    in_specs = [pl.BlockSpec((nh, 1, 1), lambda b, i: (0, 0, 0)),
                pl.BlockSpec((nh, tq, KV_LORA), lambda b, i: (0, b * nq + i, 0)),
                pl.BlockSpec((n, KV_LORA), lambda b, i: (b, 0)),
                pl.BlockSpec((tq, n), lambda b, i: (b * nq + i, 0))]
    args = [far, q_lat, kv_lat, sel_new]
    if has_past:
        in_specs += [pl.BlockSpec((1, p_len, KV_LORA), lambda b, i: (b, 0, 0)),
                     pl.BlockSpec((tq, p_len), lambda b, i: (b * nq + i, 0))]
        args += [kv_past, sel_past]
    in_specs += [pl.BlockSpec((1, nh, tq, tkd), lambda b, i: (i % npar, 0, 0, 0)),
                 pl.BlockSpec((1, nh, tq, TK), lambda b, i: (i % npar, 0, 0, 0))]
    args += [bd, bp]
    return pl.pallas_call(
        functools.partial(_sparse_attn_kernel, tq=tq, tkd=tkd, n=n, p_len=p_len),
        grid=(bsz, nq), in_specs=in_specs,
        out_specs=pl.BlockSpec((nh, tq, KV_LORA), lambda b, i: (0, b * nq + i, 0)),
        out_shape=jax.ShapeDtypeStruct((nh, t, KV_LORA), MXU),
        compiler_params=_cparams(("arbitrary", "arbitrary")),
        name="sparse_attn_past" if has_past else "sparse_attn",
    )(*args)


def _tile(n, cap, mult):
    best = None
    for t in range(mult, min(n, cap) + 1, mult):
        if n % t == 0:
            best = t
    assert best is not None, (n, cap, mult)
    return best


def _per_chunk(v, reps):
    return jnp.repeat(v, reps, axis=0)[:, None, :]


def _trunk(x, mod, past, p, w):
    bsz, n, d = x.shape
    t = bsz * n
    x = x.reshape(t, d)
    tm = _tile(t, 512, CHUNK)
    tmn = _tile(t, 256, CHUNK)
    tn_d = _tile(d, 1024, LANES)
    depth = p['w_ada'].shape[0]
    p_len = 0 if past is None else past[0].shape[2]
    top_k = min(TOPK_MAX, (p_len + n) // 4)
    even_rows, odd_rows = [], []
    for l in range(depth):
        m6 = mod[l].reshape(bsz, 6, d)
        sh1, sc1, g1, sh2, sc2, g2 = (_per_chunk(m6[:, k], n // CHUNK) for k in range(6))
        h = _norm(x, p['norm_mix_g'][l], tm=tmn, sc=sc1, sh=sh1, out_dtype=MXU)
        if l % 2 == 0:
            i = l // 2
            lam_init = 0.8 - 0.6 * math.exp(-0.3 * l)
            qkv = _mm(h, p['w_in_even'][i], n_out=48 * LANES, tm=tm, tn=1024, seg_out=True, name="mm_qkv")
            past_i = None if past is None else tuple(a[i] for a in past[:4])
            lam = (jnp.exp(jnp.sum(p['lam_q1'][i].astype(F32) * p['lam_k1'][i].astype(F32)))
                   - jnp.exp(jnp.sum(p['lam_q2'][i].astype(F32) * p['lam_k2'][i].astype(F32)))
                   + lam_init)
            o = _even_attn(qkv, past_i, p['rel_bias_table'], lam, p['subln_g'][i], lam_init,
                           bsz=bsz, n=n)
            even_rows.append((qkv[1].reshape(bsz, n, N_MAPS_A, DA_QK),
                              qkv[2].reshape(bsz, n, N_HEADS_A, HEAD_DIM),
                              qkv[4].reshape(bsz, n, N_HEADS_B, HEAD_DIM),
                              qkv[5].reshape(bsz, n, N_HEADS_B, HEAD_DIM)))
            x = _mm(o, p['w_out_even'][i], n_out=d, tm=tm, tn=tn_d, gate=g1, res=x)
        else:
            j = l // 2
            proj = _mm(h, w['w_in_odd'][j], n_out=10 * LANES, tm=tm, tn=10 * LANES)
            c_q = _norm(proj, p['g_q'][j], tm=tmn, width=Q_LORA, col_block=0, out_dtype=MXU)
            kv_lat = _norm(proj, p['g_kv'][j], tm=tmn, width=KV_LORA, col_block=1)
            k_i = proj[:, 1024:1152]
            q = _mm(c_q, p['w_uq'][j], n_out=N_HEADS_C * HEAD_DIM, tm=tm, tn=1024, out_dtype=MXU)
            q_lat = _mm_heads_out(q, w['w_uk_t'][j], tm=tm)
            q_i = _mm(c_q, p['w_qidx'][j], n_out=N_IDX_HEADS * D_IDX, tm=tm, tn=1024, out_dtype=MXU)
            kv_past = None if past is None else past[4][j]
            ki_past = None if past is None else past[5][j]
            sel_new, sel_past = _indexer(q_i, proj, ki_past, bsz=bsz, n=n, top_k=top_k)
            o_lat = _sparse_attn(q_lat, kv_lat, sel_new, kv_past, sel_past, p['rel_bias_table'],
                                 bsz=bsz, n=n)
            o = _mm_heads_in(o_lat, w['w_uv_h'][j], tm=tm)
            odd_rows.append((kv_lat.reshape(bsz, n, KV_LORA), k_i.reshape(bsz, n, D_IDX)))
            x = _mm(o, p['w_out_odd'][j], n_out=d, tm=tm, tn=tn_d, gate=g1, res=x)
        h2, route = _norm_router(x, p['norm_ffn_g'][l], sc2, sh2, w['w_router'], p['router_bias'], tm=tmn)
        x = _moe(h2, route, w['w_gate'][l], w['w_up'][l], w['w_down'][l], g2, x, tm=tmn)
    y = _norm(x, p['final_norm_g'], tm=tmn).reshape(bsz, n, d)
    ev = tuple(jnp.stack([r[m] for r in even_rows]) for m in range(4))
    od = tuple(jnp.stack([r[m] for r in odd_rows]) for m in range(2))
    return y, ev, od


def kernel(x_prompt, x_sample, cache_a_k, cache_a_v, cache_b_k, cache_b_v, cache_c_kv, cache_c_idx,
           c_prompt, c_sample, rel_bias_table, norm_mix_g, norm_ffn_g, final_norm_g, w_ada, b_ada,
           w_in_even, lam_q1, lam_k1, lam_q2, lam_k2, subln_g, w_out_even, w_in_odd, g_q, g_kv,
           w_uq, w_qidx, w_uk, w_uv, w_out_odd, w_router, router_bias, w_gate, w_up, w_down):
    p = dict(rel_bias_table=rel_bias_table, norm_mix_g=norm_mix_g, norm_ffn_g=norm_ffn_g,
             final_norm_g=final_norm_g, w_ada=w_ada, b_ada=b_ada, w_in_even=w_in_even,
             lam_q1=lam_q1, lam_k1=lam_k1, lam_q2=lam_q2, lam_k2=lam_k2, subln_g=subln_g,
             w_out_even=w_out_even, g_q=g_q, g_kv=g_kv, w_uq=w_uq, w_qidx=w_qidx,
             w_out_odd=w_out_odd, router_bias=router_bias)
    d = x_prompt.shape[-1]
    depth = w_ada.shape[0]
    w = dict(
        w_in_odd=jnp.pad(w_in_odd, ((0, 0), (0, 0), (0, 10 * LANES - w_in_odd.shape[-1]))),
        w_uk_t=jnp.transpose(w_uk, (0, 2, 3, 1)),
        w_uv_h=jnp.transpose(w_uv, (0, 2, 1, 3)),
        w_router=jnp.pad(w_router, ((0, 0), (0, LANES - N_EXPERTS))),
        w_gate=w_gate.astype(MXU), w_up=w_up.astype(MXU), w_down=w_down.astype(MXU))
    nb_p, nb_s = c_prompt.shape[0], c_sample.shape[0]
    c_all = jnp.concatenate([c_prompt, c_sample], axis=0)
    rows = -(-(nb_p + nb_s) // 16) * 16
    c_act = jnp.pad(c_all * (1.0 / (1.0 + jnp.exp(-c_all))), ((0, rows - nb_p - nb_s), (0, 0)))
    tn_ada = _tile(6 * d, 1024, LANES)
    mods = [_mm(c_act, w_ada[l], n_out=6 * d, tm=rows, tn=tn_ada, bias=b_ada[l]) for l in range(depth)]
    mod_p = [m[:nb_p] for m in mods]
    mod_s = [m[nb_p:nb_p + nb_s] for m in mods]

    y_prompt, ev_p, od_p = _trunk(x_prompt, mod_p, None, p, w)
    past = (cache_a_k, cache_a_v, cache_b_k, cache_b_v, cache_c_kv, cache_c_idx)
    y_sample, ev_s, od_s = _trunk(x_sample, mod_s, past, p, w)
    return (y_prompt, y_sample) + ev_p + od_p + ev_s + od_s
```

```python
import functools
import math

import jax
import jax.numpy as jnp
import numpy as np
from jax import lax
from jax.experimental import pallas as pl
from jax.experimental.pallas import tpu as pltpu

F32 = jnp.float32
MXU = jnp.bfloat16

CHUNK = 64
HEAD_DIM = 128
N_HEADS_A = 8
N_MAPS_A = 16
DA_QK = 64
N_HEADS_B = 8
N_HEADS_C = 16
Q_LORA = 512
KV_LORA = 512
N_IDX_HEADS = 16
D_IDX = 128
TOPK_MAX = 256
N_BUCKETS = 32
MAX_DISTANCE = 128
N_EXPERTS = 16
N_GROUPS = 4
EXPERTS_PER_GROUP = 4
EPS = 1e-6
LANES = 128
TK = 256
TQ_SPARSE = 64
TQ_SPARSE_PAST = 32
VMEM_LIMIT = 56 * 1024 * 1024
NEG_INF = float("-inf")


def _cparams(sem):
    return pltpu.CompilerParams(dimension_semantics=sem, vmem_limit_bytes=VMEM_LIMIT)


def _dot(a, b):
    return jnp.dot(a, b, preferred_element_type=F32)


def _dot_nt(a, b):
    return lax.dot_general(a, b, (((1,), (1,)), ((), ())), preferred_element_type=F32)


def _mm_kernel(*refs, has_bias, has_res, tm):
    it = iter(refs)
    a_ref, b_ref = next(it), next(it)
    bias_ref = next(it) if has_bias else None
    gate_ref, res_ref = (next(it), next(it)) if has_res else (None, None)
    o_ref, bsc = next(it), next(it)

    @pl.when(pl.program_id(1) == 0)
    def _():
        bsc[...] = b_ref[...].reshape(bsc.shape).astype(bsc.dtype)

    a = a_ref[...]
    a = a.reshape(a.shape[-2:]).astype(MXU)
    acc = _dot(a, bsc[...])
    if has_bias:
        acc = acc + bias_ref[...]
    if has_res:
        tn = acc.shape[-1]
        acc = (acc.reshape(tm // CHUNK, CHUNK, tn) * gate_ref[...]).reshape(tm, tn)
        acc = acc + res_ref[...]
    o_ref[...] = acc.reshape(o_ref.shape).astype(o_ref.dtype)


def _mm(a, b, *, n_out, tm, tn, bias=None, gate=None, res=None, out_dtype=F32, seg_out=False, name="mm"):
    m, k = a.shape
    assert m % tm == 0 and n_out % tn == 0 and b.shape[0] == k
    in_specs = [pl.BlockSpec((tm, k), lambda j, i: (i, 0)),
                pl.BlockSpec((k, tn), lambda j, i: (0, j))]
    args = [a, b]
    if bias is not None:
        in_specs.append(pl.BlockSpec((1, tn), lambda j, i: (0, j)))
        args.append(bias.reshape(1, n_out))
    if res is not None:
        in_specs.append(pl.BlockSpec((tm // CHUNK, 1, tn), lambda j, i: (i, 0, j)))
        in_specs.append(pl.BlockSpec((tm, tn), lambda j, i: (i, j)))
        args += [gate, res]
    return pl.pallas_call(
        functools.partial(_mm_kernel, has_bias=bias is not None, has_res=res is not None, tm=tm),
        grid=(n_out // tn, m // tm), in_specs=in_specs,
        out_specs=(pl.BlockSpec((1, tm, tn), lambda j, i: (j, i, 0)) if seg_out
                   else pl.BlockSpec((tm, tn), lambda j, i: (i, j))),
        out_shape=jax.ShapeDtypeStruct((n_out // tn, m, tn) if seg_out else (m, n_out), out_dtype),
        scratch_shapes=[pltpu.VMEM((k, tn), MXU)],
        compiler_params=_cparams(("arbitrary", "arbitrary")), name=name,
    )(*args)


def _mm_heads_out(a, b_h, *, tm):
    m = a.shape[0]
    nh, ka, n = b_h.shape
    return pl.pallas_call(
        functools.partial(_mm_kernel, has_bias=False, has_res=False, tm=tm),
        grid=(nh, m // tm),
        in_specs=[pl.BlockSpec((tm, ka), lambda h, i: (i, h)),
                  pl.BlockSpec((1, ka, n), lambda h, i: (h, 0, 0))],
        out_specs=pl.BlockSpec((1, tm, n), lambda h, i: (h, i, 0)),
        out_shape=jax.ShapeDtypeStruct((nh, m, n), MXU),
        scratch_shapes=[pltpu.VMEM((ka, n), MXU)],
        compiler_params=_cparams(("arbitrary", "arbitrary")), name="mm_heads_out",
    )(a, b_h)


def _mm_heads_in(a_h, b_h, *, tm):
    nh, m, ka = a_h.shape
    n = b_h.shape[2]
    return pl.pallas_call(
        functools.partial(_mm_kernel, has_bias=False, has_res=False, tm=tm),
        grid=(nh, m // tm),
        in_specs=[pl.BlockSpec((1, tm, ka), lambda h, i: (h, i, 0)),
                  pl.BlockSpec((1, ka, n), lambda h, i: (h, 0, 0))],
        out_specs=pl.BlockSpec((tm, n), lambda h, i: (i, h)),
        out_shape=jax.ShapeDtypeStruct((m, nh * n), MXU),
        scratch_shapes=[pltpu.VMEM((ka, n), MXU)],
        compiler_params=_cparams(("arbitrary", "arbitrary")), name="mm_heads_in",
    )(a_h, b_h)


def _rms(x, g):
    return x * lax.rsqrt(jnp.mean(x * x, axis=-1, keepdims=True) + EPS) * g


def _modulate(y, sc_ref, sh_ref, tm):
    d = y.shape[-1]
    y3 = y.reshape(tm // CHUNK, CHUNK, d)
    y3 = y3 * (1.0 + sc_ref[...]) + sh_ref[...]
    return y3.reshape(tm, d)


def _norm_kernel(*refs, has_mod, tm):
    if has_mod:
        x_ref, g_ref, sc_ref, sh_ref, o_ref = refs
    else:
        x_ref, g_ref, o_ref = refs
    y = _rms(x_ref[...], g_ref[...])
    if has_mod:
        y = _modulate(y, sc_ref, sh_ref, tm)
    o_ref[...] = y.astype(o_ref.dtype)


def _norm(x, g, *, tm, width=None, col_block=0, sc=None, sh=None, out_dtype=F32):
    width = x.shape[1] if width is None else width
    rows = x.shape[0]
    assert rows % tm == 0
    has_mod = sc is not None
    in_specs = [pl.BlockSpec((tm, width), lambda i: (i, col_block)),
                pl.BlockSpec((1, width), lambda i: (0, 0))]
    args = [x, g.reshape(1, width)]
    if has_mod:
        in_specs += [pl.BlockSpec((tm // CHUNK, 1, width), lambda i: (i, 0, 0))] * 2
        args += [sc, sh]
    return pl.pallas_call(
        functools.partial(_norm_kernel, has_mod=has_mod, tm=tm),
        grid=(rows // tm,), in_specs=in_specs,
        out_specs=pl.BlockSpec((tm, width), lambda i: (i, 0)),
        out_shape=jax.ShapeDtypeStruct((rows, width), out_dtype),
        compiler_params=_cparams(("arbitrary",)), name="norm",
    )(*args)


def _split2(x):
    hi = x.astype(MXU)
    lo = (x - hi.astype(F32)).astype(MXU)
    return hi, lo


def _route(logits, rb_ref):
    tm = logits.shape[0]
    aff = [1.0 / (1.0 + jnp.exp(-logits[:, e:e + 1])) for e in range(N_EXPERTS)]
    sc = [aff[e] + rb_ref[e] for e in range(N_EXPERTS)]
    npg = EXPERTS_PER_GROUP
    gscore = []
    for g in range(N_GROUPS):
        v = sc[g * npg:(g + 1) * npg]
        best = None
        for a in range(npg):
            for b in range(a + 1, npg):
                s = v[a] + v[b]
                best = s if best is None else jnp.maximum(best, s)
        gscore.append(best)
    gb = jnp.zeros((tm, 1), jnp.int32)
    gv = gscore[0]
    for g in range(1, N_GROUPS):
        better = gscore[g] > gv
        gb = jnp.where(better, g, gb)
        gv = jnp.where(better, gscore[g], gv)
    u = [sc[j] for j in range(npg)]
    a4 = [aff[j] for j in range(npg)]
    for g in range(1, N_GROUPS):
        pick = gb == g
        u = [jnp.where(pick, sc[g * npg + j], u[j]) for j in range(npg)]
        a4 = [jnp.where(pick, aff[g * npg + j], a4[j]) for j in range(npg)]
    i1 = jnp.zeros((tm, 1), jnp.int32)
    v1 = u[0]
    for j in range(1, npg):
        better = u[j] > v1
        i1 = jnp.where(better, j, i1)
        v1 = jnp.where(better, u[j], v1)
    i2 = jnp.full((tm, 1), -1, jnp.int32)
    v2 = jnp.full((tm, 1), NEG_INF, F32)
    for j in range(npg):
        better = (i1 != j) & ((u[j] > v2) | (i2 < 0))
        i2 = jnp.where(better, j, i2)
        v2 = jnp.where(better, u[j], v2)
    w1 = a4[0]
    w2 = a4[0]
    for j in range(1, npg):
        w1 = jnp.where(i1 == j, a4[j], w1)
        w2 = jnp.where(i2 == j, a4[j], w2)
    tot = w1 + w2
    w1 = w1 / tot
    w2 = w2 / tot
    e1 = gb * npg + i1
    e2 = gb * npg + i2
    lane = lax.broadcasted_iota(jnp.int32, (tm, LANES), 1)
    out = jnp.where(lane == 0, e1.astype(F32), 0.0)
    out = jnp.where(lane == 1, e2.astype(F32), out)
    out = jnp.where(lane == 2, w1, out)
    return jnp.where(lane == 3, w2, out)


def _norm_router_kernel(x_ref, g_ref, sc_ref, sh_ref, wr_ref, rb_ref, h_ref, route_ref, *, tm):
    h = _modulate(_rms(x_ref[...], g_ref[...]), sc_ref, sh_ref, tm)
    h_ref[...] = h
    hh, hl = _split2(h)
    wh, wl = _split2(wr_ref[...])
    logits = _dot(hh, wh) + (_dot(hl, wh) + _dot(hh, wl))
    route_ref[...] = _route(logits, rb_ref)


def _norm_router(x, g, sc, sh, w_router_pad, router_bias, *, tm):
    t, d = x.shape
    return pl.pallas_call(
        functools.partial(_norm_router_kernel, tm=tm),
        grid=(t // tm,),
        in_specs=[pl.BlockSpec((tm, d), lambda i: (i, 0)),
                  pl.BlockSpec((1, d), lambda i: (0, 0)),
                  pl.BlockSpec((tm // CHUNK, 1, d), lambda i: (i, 0, 0)),
                  pl.BlockSpec((tm // CHUNK, 1, d), lambda i: (i, 0, 0)),
                  pl.BlockSpec((d, LANES), lambda i: (0, 0)),
                  pl.BlockSpec(memory_space=pltpu.SMEM)],
        out_specs=[pl.BlockSpec((tm, d), lambda i: (i, 0)),
                   pl.BlockSpec((tm, LANES), lambda i: (i, 0))],
        out_shape=[jax.ShapeDtypeStruct((t, d), F32), jax.ShapeDtypeStruct((t, LANES), F32)],
        compiler_params=_cparams(("arbitrary",)), name="norm_router",
    )(x, g.reshape(1, d), sc, sh, w_router_pad, router_bias)


TM_EXPERT = 256
TM_DISPATCH = 256


def _row_copy(src, src_row, dst, dst_row, sem):
    return pltpu.make_async_copy(src.at[pl.ds(src_row, 1), :], dst.at[pl.ds(dst_row, 1), :], sem)


def _rows_wait(src, dst, sem, n):
    pltpu.make_async_copy(src.at[pl.ds(0, n), :], dst.at[pl.ds(0, n), :], sem).wait()


def _dispatch_kernel(dest_ref, pad_ref, h_ref, xs_hbm, sems, *, tb, t, n_pad):
    i = pl.program_id(0)
    sem = sems.at[0]

    def body(r, c):
        tok = i * tb + r
        _row_copy(h_ref, r, xs_hbm, dest_ref[tok], sem).start()
        _row_copy(h_ref, r, xs_hbm, dest_ref[t + tok], sem).start()
        return c

    lax.fori_loop(0, tb, body, 0, unroll=8)
    for _ in range(2):
        _rows_wait(h_ref, xs_hbm, sem, tb)

    @pl.when(i == 0)
    def _():
        for c0 in range(0, n_pad, tb):
            nc = min(tb, n_pad - c0)

            def pad_body(r, c, c0=c0):
                _row_copy(h_ref, 0, xs_hbm, pad_ref[c0 + r], sem).start()
                return c

            lax.fori_loop(0, nc, pad_body, 0, unroll=8)
            _rows_wait(h_ref, xs_hbm, sem, nc)


def _expert_kernel(te_ref, nv_ref, x_ref, wg_ref, wu_ref, wd_ref, y_ref):
    del te_ref
    i = pl.program_id(0)

    @pl.when(i < nv_ref[0])
    def _():
        x = x_ref[...].astype(MXU)
        a = _dot(x, wg_ref[0])
        b = _dot(x, wu_ref[0])
        he = (a / (1.0 + jnp.exp(-a))) * b
        y_ref[...] = _dot(he.astype(MXU), wd_ref[0])

    @pl.when(i >= nv_ref[0])
    def _():
        y_ref[...] = jnp.zeros_like(y_ref)


def _combine_kernel(dest_ref, y_hbm, route_ref, res_ref, g2_ref, o_ref, ybuf, sem, *, tm, t):
    i = pl.program_id(0)
    n = pl.num_programs(0)
    slot = i % 2

    def start(step, s):
        for j in range(2):
            def body(r, c):
                _row_copy(y_hbm, dest_ref[j * t + step * tm + r], ybuf.at[s, j], r, sem.at[s]).start()
                return c

            lax.fori_loop(0, tm, body, 0, unroll=8)

    @pl.when(i == 0)
    def _():
        start(0, 0)

    @pl.when(i + 1 < n)
    def _():
        start(i + 1, 1 - slot)

    for j in range(2):
        _rows_wait(y_hbm, ybuf.at[slot, j], sem.at[slot], tm)
    d = o_ref.shape[-1]
    route = route_ref[...]
    y = route[:, 2:3] * ybuf[slot, 0] + route[:, 3:4] * ybuf[slot, 1]
    o_ref[...] = res_ref[...] + (y.reshape(tm // CHUNK, CHUNK, d) * g2_ref[...]).reshape(tm, d)


def _cumsum_rows(x, blk):
    n, e = x.shape
    xb = x.reshape(n // blk, blk, e)
    tri = jnp.asarray(np.tril(np.ones((blk, blk), np.float32)))
    local = jnp.einsum('ij,bje->bie', tri, xb, precision=lax.Precision.HIGHEST)
    tot = local[:, -1, :]
    return (local + (jnp.cumsum(tot, axis=0) - tot)[:, None, :]).reshape(n, e)


def _moe(h, route, w_gate, w_up, w_down, g2, res, *, tm):
    t, d = h.shape
    de = w_gate.shape[-1]
    te = _tile(2 * t, TM_EXPERT, CHUNK)
    n_tiles = 2 * t // te + N_EXPERTS
    r_pad = n_tiles * te
    n_pad = r_pad - 2 * t
    e_flat = route[:, 0:2].T.reshape(-1)
    experts = jnp.arange(N_EXPERTS, dtype=F32)
    onehot = (e_flat[:, None] == experts[None, :]).astype(F32)
    csum = _cumsum_rows(onehot, _tile(2 * t, 256, 8))
    counts = csum[-1]
    tiles_e = jnp.ceil(counts / te)
    tile_end = jnp.cumsum(tiles_e)
    seg_start = (tile_end - tiles_e) * te
    dest = jnp.sum(onehot * (seg_start[None, :] + csum - 1.0), axis=1).astype(jnp.int32)
    gap_start = jnp.concatenate([seg_start + counts, tile_end[-1:] * te])
    gap_len = jnp.concatenate([tiles_e * te - counts, r_pad - tile_end[-1:] * te])
    gap_first = jnp.cumsum(gap_len) - gap_len
    k = jnp.arange(n_pad, dtype=F32)[:, None]
    in_gap = (k >= gap_first[None, :]) & (k < (gap_first + gap_len)[None, :])
    pad_rows = jnp.sum(jnp.where(in_gap, gap_start[None, :] + k - gap_first[None, :], 0.0),
                       axis=1).astype(jnp.int32)
    tile_id = jnp.arange(n_tiles, dtype=F32)[:, None]
    tile_e = jnp.minimum(jnp.sum((tile_end[None, :] <= tile_id).astype(jnp.int32), axis=1),
                         N_EXPERTS - 1)
    n_valid = tile_end[-1:].astype(jnp.int32)

    tb = _tile(t, TM_DISPATCH, CHUNK)
    xs = pl.pallas_call(
        functools.partial(_dispatch_kernel, tb=tb, t=t, n_pad=n_pad),
        grid_spec=pltpu.PrefetchScalarGridSpec(
            num_scalar_prefetch=2, grid=(t // tb,),
            in_specs=[pl.BlockSpec((tb, d), lambda i, dst, pad: (i, 0))],
            out_specs=pl.BlockSpec(memory_space=pl.ANY),
            scratch_shapes=[pltpu.SemaphoreType.DMA((1,))]),
        out_shape=jax.ShapeDtypeStruct((r_pad, d), F32),
        compiler_params=_cparams(("arbitrary",)), name="moe_dispatch",
    )(dest, pad_rows, h)

    y = pl.pallas_call(
        _expert_kernel,
        grid_spec=pltpu.PrefetchScalarGridSpec(
            num_scalar_prefetch=2, grid=(n_tiles,),
            in_specs=[pl.BlockSpec((te, d), lambda i, e, n: (i, 0)),
                      pl.BlockSpec((1, d, de), lambda i, e, n: (e[i], 0, 0)),
                      pl.BlockSpec((1, d, de), lambda i, e, n: (e[i], 0, 0)),
                      pl.BlockSpec((1, de, d), lambda i, e, n: (e[i], 0, 0))],
            out_specs=pl.BlockSpec((te, d), lambda i, e, n: (i, 0))),
        out_shape=jax.ShapeDtypeStruct((r_pad, d), F32),
        compiler_params=_cparams(("arbitrary",)), name="moe_experts",
    )(tile_e, n_valid, xs, w_gate, w_up, w_down)

    return pl.pallas_call(
        functools.partial(_combine_kernel, tm=tm, t=t),
        grid_spec=pltpu.PrefetchScalarGridSpec(
            num_scalar_prefetch=1, grid=(t // tm,),
            in_specs=[pl.BlockSpec(memory_space=pl.ANY),
                      pl.BlockSpec((tm, LANES), lambda i, dst: (i, 0)),
                      pl.BlockSpec((tm, d), lambda i, dst: (i, 0)),
                      pl.BlockSpec((tm // CHUNK, 1, d), lambda i, dst: (i, 0, 0))],
            out_specs=pl.BlockSpec((tm, d), lambda i, dst: (i, 0)),
            scratch_shapes=[pltpu.VMEM((2, 2, tm, d), F32), pltpu.SemaphoreType.DMA((2,))]),
        out_shape=jax.ShapeDtypeStruct((t, d), F32),
        compiler_params=_cparams(("arbitrary",)), name="moe_combine",
    )(dest, y, route, res, g2)


def _rel_bucket(rel):
    half = N_BUCKETS // 2
    max_exact = half // 2
    n = jnp.abs(rel)
    nf = jnp.maximum(n, 1).astype(F32)
    large = max_exact + (jnp.log(nf / max_exact) / math.log(MAX_DISTANCE / max_exact)
                         * (half - max_exact)).astype(jnp.int32)
    large = jnp.minimum(large, half - 1)
    return jnp.where(rel > 0, half, 0) + jnp.where(n < max_exact, n, large)


def _bias_tiles(table, q_offsets, k_offset, tq, tk, n_heads):
    tiles = []
    for q0 in q_offsets:
        rel = (k_offset + np.arange(tk))[None, :] - (q0 + np.arange(tq))[:, None]
        bucket = _rel_bucket(jnp.asarray(rel, jnp.int32))
        onehot = (bucket[..., None] == jnp.arange(N_BUCKETS, dtype=jnp.int32)).astype(F32)
        tiles.append(jnp.einsum('qkb,bh->hqk', onehot, table[:, :n_heads].astype(F32),
                                precision=lax.Precision.HIGHEST))
    return jnp.stack(tiles)


def _bias_far(table, n_heads):
    rel = jnp.full((1,), -(MAX_DISTANCE + 1), jnp.int32)
    return table[_rel_bucket(rel)][0, :n_heads].astype(F32)


def _diff_attend(qr, far, prev, diag, bias_far, bp_ref, bd_ref, vis):
    outs = []
    for r in range(2):
        cols = slice(r * DA_QK, (r + 1) * DA_QK)
        parts = []
        if far is not None:
            s = _dot_nt(qr[r], far[0][:, cols].astype(MXU)) +bias_far[r]
            parts.append((s, far[1]))
        if prev is not None:
            s = _dot_nt(qr[r], prev[0][:, cols].astype(MXU)) +bp_ref[0, r]
            parts.append((s, prev[1]))
        s = _dot_nt(qr[r], diag[0][:, cols].astype(MXU)) +bd_ref[0, r]
        parts.append((jnp.where(vis, s, NEG_INF), diag[1]))
        m = None
        for s, _ in parts:
            mx = jnp.max(s, axis=-1, keepdims=True)
            m = mx if m is None else jnp.maximum(m, mx)
        l = None
        acc = None
        for s, v in parts:
            p = jnp.exp(s - m)
            ps = jnp.sum(p, axis=-1, keepdims=True)
            pv = _dot(p.astype(MXU), v.astype(MXU))
            l = ps if l is None else l + ps
            acc = pv if acc is None else acc + pv
        outs.append(acc / l)
    return outs


def _stick_attend(qb, blocks, tri_ref, trid_ref):
    carry = None
    acc = None
    for kb, vb, mask in blocks:
        tk = kb.shape[0]
        z = _dot_nt(qb, kb.astype(MXU)) * HEAD_DIM ** -0.5
        log_beta = jnp.minimum(z, 0.0) - jnp.log1p(jnp.exp(-jnp.abs(z)))
        log_keep = log_beta - z
        if mask is not None:
            log_keep = jnp.where(mask, log_keep, 0.0)
        tri = (trid_ref if tk != TK else tri_ref)[...]
        hi, lo = _split2(log_keep)
        r = _dot(hi, tri) + _dot(lo, tri)
        later = r[:, LANES:]
        if carry is not None:
            later = later + (carry[:, :tk] if tk <= LANES else
                             jnp.concatenate([carry] * (tk // LANES), axis=1))
        a = jnp.exp(log_beta + later)
        if mask is not None:
            a = jnp.where(mask, a, 0.0)
        pv = _dot(a.astype(MXU), vb.astype(MXU))
        acc = pv if acc is None else acc + pv
        carry = r[:, :LANES] if carry is None else carry + r[:, :LANES]
    return acc


def _even_attn_kernel(*refs, tq, tkd, nq, p_len, lam_init):
    has_past = p_len > 0
    it = iter(refs)
    far_ref, lam_ref, q_ref, kn_ref, vn_ref = (next(it) for _ in range(5))
    cache = [next(it) for _ in range(4)] if has_past else None
    bd_ref, bp_ref, g_ref, tri_ref, trid_ref, o_ref = (next(it) for _ in range(6))

    u = pl.program_id(1)
    i = pl.program_id(2)
    q = q_ref[0]
    qq = lax.broadcasted_iota(jnp.int32, (tq, tkd), 0)
    kk = lax.broadcasted_iota(jnp.int32, (tq, tkd), 1)

    def regions(c, kp_ref, vp_ref):
        if has_past:
            far = (kp_ref, vp_ref, 0, p_len - TK) if p_len > TK else None
            prev = (kp_ref, vp_ref, p_len - TK, p_len)
            diag = (kn_ref, vn_ref, 0, tkd)
        else:
            far = (kn_ref, vn_ref, 0, (c - 1) * TK) if c >= 2 else None
            prev = (kn_ref, vn_ref, (c - 1) * TK, c * TK) if c >= 1 else None
            diag = (kn_ref, vn_ref, c * TK, c * TK + tkd)
        return far, prev, diag

    def load(reg, lo=None, hi=None):
        kref, vref, a, b = reg
        lo, hi = (a, b) if lo is None else (a + lo, a + hi)
        return kref[0, lo:hi, :], vref[0, lo:hi, :]

    def diff(c):
        far, prev, diag = regions(c, *(cache[0:2] if has_past else (None, None)))
        qr = [(q[:, r * DA_QK:(r + 1) * DA_QK] * DA_QK ** -0.5).astype(MXU) for r in range(2)]
        vis = (kk // CHUNK) <= (qq // CHUNK)
        o0, o1 = _diff_attend(qr, None if far is None else load(far), None if prev is None else load(prev),
                              load(diag), [far_ref[2 * u], far_ref[2 * u + 1]], bp_ref, bd_ref, vis)
        o = _rms(o0 - lam_ref[0] * o1, g_ref[...]) * (1.0 - lam_init)
        o_ref[...] = o.astype(o_ref.dtype)

    def stick(c):
        far, prev, diag = regions(c, *(cache[2:4] if has_past else (None, None)))
        blocks = [load(diag) + (kk < qq,)]
        if prev is not None:
            blocks.append(load(prev) + (None,))
        if far is not None:
            for j in range((far[3] - far[2]) // TK - 1, -1, -1):
                blocks.append(load(far, j * TK, (j + 1) * TK) + (None,))
        o_ref[...] = _stick_attend(q.astype(MXU), blocks, tri_ref, trid_ref).astype(o_ref.dtype)

    for c in range(nq):
        pl.when((u < N_HEADS_A) & (i == c))(functools.partial(diff, c))
        pl.when((u >= N_HEADS_A) & (i == c))(functools.partial(stick, c))


def _tri_ones(tk):
    later = np.arange(tk)[:, None] > np.arange(tk)[None, :]
    return jnp.asarray(np.concatenate([np.ones((tk, LANES), bool), later], axis=1), MXU)


def _even_attn(qkv, past, table, lam, subln_g, lam_init, *, bsz, n):
    has_past = past is not None
    tq = min(n, TK)
    assert n % tq == 0 and (has_past or tq == TK) and (not has_past or n == tq)
    tkd = tq
    nq = n // tq
    p_len = past[0].shape[1] if has_past else 0
    far = _bias_far(table, N_MAPS_A)
    bd = _bias_tiles(table, [0], 0, tq, tkd, N_MAPS_A)
    bp = _bias_tiles(table, [0], -TK, tq, TK, N_MAPS_A)

    def seg(u, base):
        return jnp.where(u < 8, base, base + 3)

    smem = pl.BlockSpec(memory_space=pltpu.SMEM)
    in_specs = [smem, smem,
                pl.BlockSpec((1, tq, HEAD_DIM), lambda b, u, i: (seg(u, 0), b * nq + i, u % 8)),
                pl.BlockSpec((1, n, HEAD_DIM), lambda b, u, i: (seg(u, 1), b, u % 8)),
                pl.BlockSpec((1, n, HEAD_DIM), lambda b, u, i: (seg(u, 2), b, u % 8))]
    args = [far, lam.reshape(1), qkv, qkv, qkv]
    if has_past:
        assert p_len % TK == 0 and p_len >= TK
        lo = lambda b, u, i: (b, 0, jnp.minimum(u, 7))
        hi = lambda b, u, i: (b, 0, jnp.maximum(u, 8) - 8)
        in_specs += [pl.BlockSpec((1, p_len, HEAD_DIM), m) for m in (lo, lo, hi, hi)]
        args += [a.reshape(bsz, p_len, -1) for a in past]
    in_specs += [pl.BlockSpec((1, 2, tq, tkd), lambda b, u, i: (0, jnp.minimum(u, 7), 0, 0)),
                 pl.BlockSpec((1, 2, tq, TK), lambda b, u, i: (0, jnp.minimum(u, 7), 0, 0)),
                 pl.BlockSpec((1, HEAD_DIM), lambda b, u, i: (0, 0)),
                 pl.BlockSpec((TK, LANES + TK), lambda b, u, i: (0, 0)),
                 pl.BlockSpec((tkd, LANES + tkd), lambda b, u, i: (0, 0))]
    args += [bd, bp, subln_g.reshape(1, HEAD_DIM), _tri_ones(TK), _tri_ones(tkd)]
    kern = functools.partial(_even_attn_kernel, tq=tq, tkd=tkd, nq=nq, p_len=p_len, lam_init=lam_init)
    return pl.pallas_call(
        kern, grid=(bsz, 16, nq), in_specs=in_specs,
        out_specs=pl.BlockSpec((tq, HEAD_DIM), lambda b, u, i: (b * nq + i, u)),
        out_shape=jax.ShapeDtypeStruct((bsz * n, 16 * HEAD_DIM), MXU),
        compiler_params=_cparams(("arbitrary", "arbitrary", "arbitrary")),
        name="even_attn_past" if has_past else "even_attn",
    )(*args)


INT_MIN = int(np.iinfo(np.int32).min)
NEG_INF_KEY = int(np.array(-np.inf, np.float32).view(np.int32)) ^ 0x7FFFFFFF
INDEX_BITS = 15


def _sort_key(x):
    b = lax.bitcast_convert_type(x, jnp.int32)
    return b ^ ((b >> 31) & 0x7FFFFFFF)


def _indexer_kernel(*refs, tq, n, p_len, top_k):
    has_past = p_len > 0
    it = iter(refs)
    qi_ref, wi_ref, kin_ref = next(it), next(it), next(it)
    kip_ref = next(it) if has_past else None
    seln_ref = next(it)
    selp_ref = next(it) if has_past else None
    keyn_ref = next(it)
    keyp_ref = next(it) if has_past else None

    i = pl.program_id(1)
    qi = qi_ref[...]
    wi = wi_ref[...] * (N_IDX_HEADS ** -0.5 * D_IDX ** -0.5)

    def scores(kmat):
        kb = kmat.astype(MXU)
        acc = None
        for ih in range(N_IDX_HEADS):
            d = _dot_nt(qi[:, ih * D_IDX:(ih + 1) * D_IDX], kb)
            term = wi[:, ih:ih + 1] * jnp.maximum(d, 0.0)
            acc = term if acc is None else acc + term
        return acc

    def run(w_new):
        s_new = scores(kin_ref[:w_new, :])
        qq = i * tq + lax.broadcasted_iota(jnp.int32, (tq, w_new), 0)
        kk = lax.broadcasted_iota(jnp.int32, (tq, w_new), 1)
        s_new = jnp.where((kk // CHUNK) <= (qq // CHUNK), s_new, NEG_INF)
        keyn_ref[:, :w_new] = _sort_key(s_new)
        parts = [(keyn_ref, p_len, w_new, seln_ref)]
        if has_past:
            keyp_ref[...] = _sort_key(scores(kip_ref[0]))
            parts.append((keyp_ref, 0, p_len, selp_ref))

        def count(pred):
            tot = None
            for ref, base, width, _ in parts:
                idx = base + lax.broadcasted_iota(jnp.int32, (tq, width), 1)
                c = jnp.sum(jnp.where(pred(ref[:, :width], idx), 1.0, 0.0), axis=-1, keepdims=True)
                tot = c if tot is None else tot + c
            return tot

        def body(b, t):
            cand = t + jnp.left_shift(jnp.int32(1), 31 - b)
            c = count(lambda k, idx: k >= cand)
            return jnp.where(c >= top_k, cand, t)

        t = lax.fori_loop(0, 32, body, jnp.full((tq, 1), INT_MIN, jnp.int32))
        n_gt = count(lambda k, idx: k > t)
        n_eq = count(lambda k, idx: k == t)
        need = top_k - n_gt
        tie = jnp.where((n_eq != need) & (t > NEG_INF_KEY), 1.0, 0.0)

        def write(sel_of):
            for ref, base, width, out_ref in parts:
                k = ref[:, :width]
                idx = base + lax.broadcasted_iota(jnp.int32, (tq, width), 1)
                out_ref[:, :width] = jnp.where(sel_of(k, idx) & (k > NEG_INF_KEY), 1.0, 0.0)

        write(lambda k, idx: k >= t)
        if w_new < n:
            seln_ref[:, w_new:] = jnp.zeros((tq, n - w_new), F32)

        @pl.when(jnp.max(tie) > 0.0)
        def _():
            def jbody(b, jv):
                cand = jv + jnp.left_shift(jnp.int32(1), INDEX_BITS - 1 - b)
                c = count(lambda k, idx: (k == t) & (idx < cand))
                return jnp.where(c < need, cand, jv)

            jv = lax.fori_loop(0, INDEX_BITS, jbody, jnp.zeros((tq, 1), jnp.int32))
            write(lambda k, idx: (k > t) | ((k == t) & (idx <= jv)))

    if has_past or n <= TK:
        run(n)
    else:
        for c in range(n // TK):
            pl.when((i * tq) // TK == c)(functools.partial(run, (c + 1) * TK))


def _indexer(q_i, proj, k_idx_past, *, bsz, n, top_k):
    has_past = k_idx_past is not None
    tq = min(n, 128)
    nq = n // tq
    p_len = k_idx_past.shape[1] if has_past else 0
    assert n + p_len < 2 ** (INDEX_BITS - 1) and (n <= TK or n % TK == 0)
    t = bsz * n
    in_specs = [pl.BlockSpec((tq, N_IDX_HEADS * D_IDX), lambda b, i: (b * nq + i, 0)),
                pl.BlockSpec((tq, LANES), lambda b, i: (b * nq + i, 9)),
                pl.BlockSpec((n, D_IDX), lambda b, i: (b, 8))]
    args = [q_i, proj, proj]
    out_specs = [pl.BlockSpec((tq, n), lambda b, i: (b * nq + i, 0))]
    out_shape = [jax.ShapeDtypeStruct((t, n), F32)]
    scratch = [pltpu.VMEM((tq, n), jnp.int32)]
    if has_past:
        in_specs.append(pl.BlockSpec((1, p_len, D_IDX), lambda b, i: (b, 0, 0)))
        args.append(k_idx_past)
        out_specs.append(pl.BlockSpec((tq, p_len), lambda b, i: (b * nq + i, 0)))
        out_shape.append(jax.ShapeDtypeStruct((t, p_len), F32))
        scratch.append(pltpu.VMEM((tq, p_len), jnp.int32))
    res = pl.pallas_call(
        functools.partial(_indexer_kernel, tq=tq, n=n, p_len=p_len, top_k=top_k),
        grid=(bsz, nq), in_specs=in_specs, out_specs=out_specs, out_shape=out_shape,
        scratch_shapes=scratch,
        compiler_params=_cparams(("arbitrary", "arbitrary")),
        name="indexer_past" if has_past else "indexer",
    )(*args)
    return res[0], (res[1] if has_past else None)


def _sparse_attn_kernel(*refs, tq, tkd, n, p_len):
    has_past = p_len > 0
    it = iter(refs)
    far_ref, q_ref, kvn_ref, seln_ref = (next(it) for _ in range(4))
    kvp_ref, selp_ref = (next(it), next(it)) if has_past else (None, None)
    bd_ref, bp_ref, o_ref = (next(it) for _ in range(3))
    nh = N_HEADS_C
    i = pl.program_id(1)
    q = q_ref[...].reshape(nh * tq, KV_LORA)

    def attend(regions):
        parts = []
        for kvb, sel, bias in regions:
            kvb = kvb.astype(MXU)
            w = kvb.shape[0]
            s = _dot_nt(q, kvb).reshape(nh, tq, w) * HEAD_DIM ** -0.5 + bias
            parts.append((jnp.where(sel[None] > 0.0, s, NEG_INF), kvb))
        m = None
        for s, _ in parts:
            mx = jnp.max(s, axis=-1, keepdims=True)
            m = mx if m is None else jnp.maximum(m, mx)
        l = None
        acc = None
        for s, kvb in parts:
            w = kvb.shape[0]
            p = jnp.exp(s - m)
            ps = jnp.sum(p, axis=-1, keepdims=True)
            pv = _dot(p.reshape(nh * tq, w).astype(MXU), kvb)
            l = ps if l is None else l + ps
            acc = pv if acc is None else acc + pv
        o_ref[...] = (acc.reshape(nh, tq, KV_LORA) / l).astype(o_ref.dtype)

    if has_past:
        regions = []
        if p_len > TK:
            regions.append((kvp_ref[0, :p_len - TK, :], selp_ref[:, :p_len - TK], far_ref[...]))
        regions.append((kvp_ref[0, p_len - TK:, :], selp_ref[:, p_len - TK:], bp_ref[0]))
        regions.append((kvn_ref[...], seln_ref[...], bd_ref[0]))
        attend(regions)
    else:
        def run(c):
            regions = []
            if c >= 2:
                regions.append((kvn_ref[:(c - 1) * TK, :], seln_ref[:, :(c - 1) * TK], far_ref[...]))
            if c >= 1:
                regions.append((kvn_ref[(c - 1) * TK:c * TK, :], seln_ref[:, (c - 1) * TK:c * TK], bp_ref[0]))
            regions.append((kvn_ref[c * TK:(c + 1) * TK, :], seln_ref[:, c * TK:(c + 1) * TK], bd_ref[0]))
            attend(regions)

        for c in range(n // TK):
            pl.when((i * tq) // TK == c)(functools.partial(run, c))


def _sparse_attn(q_lat, kv_lat, sel_new, kv_past, sel_past, table, *, bsz, n):
    has_past = kv_past is not None
    nh = N_HEADS_C
    t = bsz * n
    if has_past:
        tq, tkd = min(n, TQ_SPARSE_PAST), n
        p_len = kv_past.shape[1]
        assert p_len % TK == 0 and n <= TK
    else:
        tq, tkd = min(n, TQ_SPARSE), TK
        p_len = 0
        assert n % TK == 0
    q_offsets = list(range(0, tkd, tq))
    nq = n // tq
    npar = len(q_offsets)
    far = _bias_far(table, nh).reshape(nh, 1, 1)
    bd = _bias_tiles(table, q_offsets, 0, tq, tkd, nh)
    bp = _bias_tiles(table, q_offsets, -TK, tq, TK, nh)
    in_specs = [pl.BlockSpec((nh, 1, 1), lambda b, i: (0, 0, 0)),
                pl.BlockSpec((nh, tq, KV_LORA), lambda b, i: (0, b * nq + i, 0)),
                pl.BlockSpec((n, KV_LORA), lambda b, i: (b, 0)),
                pl.BlockSpec((tq, n), lambda b, i: (b * nq + i, 0))]
    args = [far, q_lat, kv_lat, sel_new]
    if has_past:
        in_specs += [pl.BlockSpec((1, p_len, KV_LORA), lambda b, i: (b, 0, 0)),
                     pl.BlockSpec((tq, p_len), lambda b, i: (b * nq + i, 0))]
        args += [kv_past, sel_past]
    in_specs += [pl.BlockSpec((1, nh, tq, tkd), lambda b, i: (i % npar, 0, 0, 0)),
                 pl.BlockSpec((1, nh, tq, TK), lambda b, i: (i % npar, 0, 0, 0))]
    args += [bd, bp]
    return pl.pallas_call(
        functools.partial(_sparse_attn_kernel, tq=tq, tkd=tkd, n=n, p_len=p_len),
        grid=(bsz, nq), in_specs=in_specs,
        out_specs=pl.BlockSpec((nh, tq, KV_LORA), lambda b, i: (0, b * nq + i, 0)),
        out_shape=jax.ShapeDtypeStruct((nh, t, KV_LORA), MXU),
        compiler_params=_cparams(("arbitrary", "arbitrary")),
        name="sparse_attn_past" if has_past else "sparse_attn",
    )(*args)


def _tile(n, cap, mult):
    best = None
    for t in range(mult, min(n, cap) + 1, mult):
        if n % t == 0:
            best = t
    assert best is not None, (n, cap, mult)
    return best


def _per_chunk(v, reps):
    return jnp.repeat(v, reps, axis=0)[:, None, :]


def _trunk(x, mod, past, p, w):
    bsz, n, d = x.shape
    t = bsz * n
    x = x.reshape(t, d)
    tm = _tile(t, 512, CHUNK)
    tmn = _tile(t, 256, CHUNK)
    tn_d = _tile(d, 1024, LANES)
    depth = p['w_ada'].shape[0]
    p_len = 0 if past is None else past[0].shape[2]
    top_k = min(TOPK_MAX, (p_len + n) // 4)
    even_rows, odd_rows = [], []
    for l in range(depth):
        m6 = mod[l].reshape(bsz, 6, d)
        sh1, sc1, g1, sh2, sc2, g2 = (_per_chunk(m6[:, k], n // CHUNK) for k in range(6))
        h = _norm(x, p['norm_mix_g'][l], tm=tmn, sc=sc1, sh=sh1, out_dtype=MXU)
        if l % 2 == 0:
            i = l // 2
            lam_init = 0.8 - 0.6 * math.exp(-0.3 * l)
            qkv = _mm(h, p['w_in_even'][i], n_out=48 * LANES, tm=tm, tn=1024, seg_out=True, name="mm_qkv")
            past_i = None if past is None else tuple(a[i] for a in past[:4])
            lam = (jnp.exp(jnp.sum(p['lam_q1'][i].astype(F32) * p['lam_k1'][i].astype(F32)))
                   - jnp.exp(jnp.sum(p['lam_q2'][i].astype(F32) * p['lam_k2'][i].astype(F32)))
                   + lam_init)
            o = _even_attn(qkv, past_i, p['rel_bias_table'], lam, p['subln_g'][i], lam_init,
                           bsz=bsz, n=n)
            even_rows.append((qkv[1].reshape(bsz, n, N_MAPS_A, DA_QK),
                              qkv[2].reshape(bsz, n, N_HEADS_A, HEAD_DIM),
                              qkv[4].reshape(bsz, n, N_HEADS_B, HEAD_DIM),
                              qkv[5].reshape(bsz, n, N_HEADS_B, HEAD_DIM)))
            x = _mm(o, p['w_out_even'][i], n_out=d, tm=tm, tn=tn_d, gate=g1, res=x)
        else:
            j = l // 2
            proj = _mm(h, w['w_in_odd'][j], n_out=10 * LANES, tm=tm, tn=10 * LANES)
            c_q = _norm(proj, p['g_q'][j], tm=tmn, width=Q_LORA, col_block=0, out_dtype=MXU)
            kv_lat = _norm(proj, p['g_kv'][j], tm=tmn, width=KV_LORA, col_block=1)
            k_i = proj[:, 1024:1152]
            q = _mm(c_q, p['w_uq'][j], n_out=N_HEADS_C * HEAD_DIM, tm=tm, tn=1024, out_dtype=MXU)
            q_lat = _mm_heads_out(q, w['w_uk_t'][j], tm=tm)
            q_i = _mm(c_q, p['w_qidx'][j], n_out=N_IDX_HEADS * D_IDX, tm=tm, tn=1024, out_dtype=MXU)
            kv_past = None if past is None else past[4][j]
            ki_past = None if past is None else past[5][j]
            sel_new, sel_past = _indexer(q_i, proj, ki_past, bsz=bsz, n=n, top_k=top_k)
            o_lat = _sparse_attn(q_lat, kv_lat, sel_new, kv_past, sel_past, p['rel_bias_table'],
                                 bsz=bsz, n=n)
            o = _mm_heads_in(o_lat, w['w_uv_h'][j], tm=tm)
            odd_rows.append((kv_lat.reshape(bsz, n, KV_LORA), k_i.reshape(bsz, n, D_IDX)))
            x = _mm(o, p['w_out_odd'][j], n_out=d, tm=tm, tn=tn_d, gate=g1, res=x)
        h2, route = _norm_router(x, p['norm_ffn_g'][l], sc2, sh2, w['w_router'], p['router_bias'], tm=tmn)
        x = _moe(h2, route, w['w_gate'][l], w['w_up'][l], w['w_down'][l], g2, x, tm=tmn)
    y = _norm(x, p['final_norm_g'], tm=tmn).reshape(bsz, n, d)
    ev = tuple(jnp.stack([r[m] for r in even_rows]) for m in range(4))
    od = tuple(jnp.stack([r[m] for r in odd_rows]) for m in range(2))
    return y, ev, od


def kernel(x_prompt, x_sample, cache_a_k, cache_a_v, cache_b_k, cache_b_v, cache_c_kv, cache_c_idx,
           c_prompt, c_sample, rel_bias_table, norm_mix_g, norm_ffn_g, final_norm_g, w_ada, b_ada,
           w_in_even, lam_q1, lam_k1, lam_q2, lam_k2, subln_g, w_out_even, w_in_odd, g_q, g_kv,
           w_uq, w_qidx, w_uk, w_uv, w_out_odd, w_router, router_bias, w_gate, w_up, w_down):
    p = dict(rel_bias_table=rel_bias_table, norm_mix_g=norm_mix_g, norm_ffn_g=norm_ffn_g,
             final_norm_g=final_norm_g, w_ada=w_ada, b_ada=b_ada, w_in_even=w_in_even,
             lam_q1=lam_q1, lam_k1=lam_k1, lam_q2=lam_q2, lam_k2=lam_k2, subln_g=subln_g,
             w_out_even=w_out_even, g_q=g_q, g_kv=g_kv, w_uq=w_uq, w_qidx=w_qidx,
             w_out_odd=w_out_odd, router_bias=router_bias)
    d = x_prompt.shape[-1]
    depth = w_ada.shape[0]
    w = dict(
        w_in_odd=jnp.pad(w_in_odd, ((0, 0), (0, 0), (0, 10 * LANES - w_in_odd.shape[-1]))),
        w_uk_t=jnp.transpose(w_uk, (0, 2, 3, 1)),
        w_uv_h=jnp.transpose(w_uv, (0, 2, 1, 3)),
        w_router=jnp.pad(w_router, ((0, 0), (0, LANES - N_EXPERTS))),
        w_gate=w_gate.astype(MXU), w_up=w_up.astype(MXU), w_down=w_down.astype(MXU))
    nb_p, nb_s = c_prompt.shape[0], c_sample.shape[0]
    c_all = jnp.concatenate([c_prompt, c_sample], axis=0)
    rows = -(-(nb_p + nb_s) // 16) * 16
    c_act = jnp.pad(c_all * (1.0 / (1.0 + jnp.exp(-c_all))), ((0, rows - nb_p - nb_s), (0, 0)))
    tn_ada = _tile(6 * d, 1024, LANES)
    mods = [_mm(c_act, w_ada[l], n_out=6 * d, tm=rows, tn=tn_ada, bias=b_ada[l]) for l in range(depth)]
    mod_p = [m[:nb_p] for m in mods]
    mod_s = [m[nb_p:nb_p + nb_s] for m in mods]

    y_prompt, ev_p, od_p = _trunk(x_prompt, mod_p, None, p, w)
    past = (cache_a_k, cache_a_v, cache_b_k, cache_b_v, cache_c_kv, cache_c_idx)
    y_sample, ev_s, od_s = _trunk(x_sample, mod_s, past, p, w)
    return (y_prompt, y_sample) + ev_p + od_p + ev_s + od_s
```

```python
import functools
import math

import jax
import jax.numpy as jnp
import numpy as np
from jax import lax
from jax.experimental import pallas as pl
from jax.experimental.pallas import tpu as pltpu

F32 = jnp.float32
MXU = jnp.bfloat16

CHUNK = 64
HEAD_DIM = 128
N_HEADS_A = 8
N_MAPS_A = 16
DA_QK = 64
N_HEADS_B = 8
N_HEADS_C = 16
Q_LORA = 512
KV_LORA = 512
N_IDX_HEADS = 16
D_IDX = 128
TOPK_MAX = 256
N_BUCKETS = 32
MAX_DISTANCE = 128
N_EXPERTS = 16
N_GROUPS = 4
EXPERTS_PER_GROUP = 4
EPS = 1e-6
LANES = 128
TK = 256
TQ_SPARSE = 64
TQ_SPARSE_PAST = 32
TKC = 512
TM_HEADS = 2048
VMEM_LIMIT = 56 * 1024 * 1024
NEG_INF = float("-inf")


def _cparams(sem):
    return pltpu.CompilerParams(dimension_semantics=sem, vmem_limit_bytes=VMEM_LIMIT)


def _dot(a, b):
    return jnp.dot(a, b, preferred_element_type=F32)


def _dot_nt(a, b):
    return lax.dot_general(a, b, (((1,), (1,)), ((), ())), preferred_element_type=F32)


def _mm_kernel(*refs, has_bias, has_res, tm):
    it = iter(refs)
    a_ref, b_ref = next(it), next(it)
    bias_ref = next(it) if has_bias else None
    gate_ref, res_ref = (next(it), next(it)) if has_res else (None, None)
    o_ref, bsc = next(it), next(it)

    @pl.when(pl.program_id(1) == 0)
    def _():
        bsc[...] = b_ref[...].reshape(bsc.shape).astype(bsc.dtype)

    a = a_ref[...]
    a = a.reshape(a.shape[-2:]).astype(MXU)
    acc = _dot(a, bsc[...])
    if has_bias:
        acc = acc + bias_ref[...]
    if has_res:
        tn = acc.shape[-1]
        acc = (acc.reshape(tm // CHUNK, CHUNK, tn) * gate_ref[...]).reshape(tm, tn)
        acc = acc + res_ref[...]
    o_ref[...] = acc.reshape(o_ref.shape).astype(o_ref.dtype)


def _mm(a, b, *, n_out, tm, tn, bias=None, gate=None, res=None, out_dtype=F32, seg_out=False, layer=0,
        name="mm"):
    m, k = a.shape
    assert m % tm == 0 and n_out % tn == 0 and b.shape[-2] == k
    b_spec = (pl.BlockSpec((k, tn), lambda j, i: (0, j)) if b.ndim == 2
              else pl.BlockSpec((1, k, tn), lambda j, i: (layer, 0, j)))
    in_specs = [pl.BlockSpec((tm, k), lambda j, i: (i, 0)), b_spec]
    args = [a, b]
    if bias is not None:
        in_specs.append(pl.BlockSpec((1, tn), lambda j, i: (0, j)))
        args.append(bias.reshape(1, n_out))
    if res is not None:
        in_specs.append(pl.BlockSpec((tm // CHUNK, 1, tn), lambda j, i: (i, 0, j)))
        in_specs.append(pl.BlockSpec((tm, tn), lambda j, i: (i, j)))
        args += [gate, res]
    return pl.pallas_call(
        functools.partial(_mm_kernel, has_bias=bias is not None, has_res=res is not None, tm=tm),
        grid=(n_out // tn, m // tm), in_specs=in_specs,
        out_specs=(pl.BlockSpec((1, tm, tn), lambda j, i: (j, i, 0)) if seg_out
                   else pl.BlockSpec((tm, tn), lambda j, i: (i, j))),
        out_shape=jax.ShapeDtypeStruct((n_out // tn, m, tn) if seg_out else (m, n_out), out_dtype),
        scratch_shapes=[pltpu.VMEM((k, tn), MXU)],
        compiler_params=_cparams(("arbitrary", "arbitrary")), name=name,
    )(*args)


def _mm_heads_out(a, b_h, *, tm):
    m = a.shape[0]
    nh, ka, n = b_h.shape
    return pl.pallas_call(
        functools.partial(_mm_kernel, has_bias=False, has_res=False, tm=tm),
        grid=(nh, m // tm),
        in_specs=[pl.BlockSpec((tm, ka), lambda h, i: (i, h)),
                  pl.BlockSpec((1, ka, n), lambda h, i: (h, 0, 0))],
        out_specs=pl.BlockSpec((1, tm, n), lambda h, i: (h, i, 0)),
        out_shape=jax.ShapeDtypeStruct((nh, m, n), MXU),
        scratch_shapes=[pltpu.VMEM((ka, n), MXU)],
        compiler_params=_cparams(("arbitrary", "arbitrary")), name="mm_heads_out",
    )(a, b_h)


def _mm_heads_in(a_h, b_h, *, tm):
    nh, m, ka = a_h.shape
    n = b_h.shape[2]
    return pl.pallas_call(
        functools.partial(_mm_kernel, has_bias=False, has_res=False, tm=tm),
        grid=(nh, m // tm),
        in_specs=[pl.BlockSpec((1, tm, ka), lambda h, i: (h, i, 0)),
                  pl.BlockSpec((1, ka, n), lambda h, i: (h, 0, 0))],
        out_specs=pl.BlockSpec((tm, n), lambda h, i: (i, h)),
        out_shape=jax.ShapeDtypeStruct((m, nh * n), MXU),
        scratch_shapes=[pltpu.VMEM((ka, n), MXU)],
        compiler_params=_cparams(("arbitrary", "arbitrary")), name="mm_heads_in",
    )(a_h, b_h)


def _rms(x, g):
    return x * lax.rsqrt(jnp.mean(x * x, axis=-1, keepdims=True) + EPS) * g


def _modulate(y, sc_ref, sh_ref, tm):
    d = y.shape[-1]
    y3 = y.reshape(tm // CHUNK, CHUNK, d)
    y3 = y3 * (1.0 + sc_ref[...]) + sh_ref[...]
    return y3.reshape(tm, d)


def _norm_kernel(*refs, has_mod, tm):
    if has_mod:
        x_ref, g_ref, sc_ref, sh_ref, o_ref = refs
    else:
        x_ref, g_ref, o_ref = refs
    y = _rms(x_ref[...], g_ref[...])
    if has_mod:
        y = _modulate(y, sc_ref, sh_ref, tm)
    o_ref[...] = y.astype(o_ref.dtype)


def _norm(x, g, *, tm, width=None, col_block=0, sc=None, sh=None, out_dtype=F32):
    width = x.shape[1] if width is None else width
    rows = x.shape[0]
    assert rows % tm == 0
    has_mod = sc is not None
    in_specs = [pl.BlockSpec((tm, width), lambda i: (i, col_block)),
                pl.BlockSpec((1, width), lambda i: (0, 0))]
    args = [x, g.reshape(1, width)]
    if has_mod:
        in_specs += [pl.BlockSpec((tm // CHUNK, 1, width), lambda i: (i, 0, 0))] * 2
        args += [sc, sh]
    return pl.pallas_call(
        functools.partial(_norm_kernel, has_mod=has_mod, tm=tm),
        grid=(rows // tm,), in_specs=in_specs,
        out_specs=pl.BlockSpec((tm, width), lambda i: (i, 0)),
        out_shape=jax.ShapeDtypeStruct((rows, width), out_dtype),
        compiler_params=_cparams(("arbitrary",)), name="norm",
    )(*args)


def _split2(x):
    hi = x.astype(MXU)
    lo = (x - hi.astype(F32)).astype(MXU)
    return hi, lo


def _route(logits, rb_ref):
    tm = logits.shape[0]
    aff = [1.0 / (1.0 + jnp.exp(-logits[:, e:e + 1])) for e in range(N_EXPERTS)]
    sc = [aff[e] + rb_ref[e] for e in range(N_EXPERTS)]
    npg = EXPERTS_PER_GROUP
    gscore = []
    for g in range(N_GROUPS):
        v = sc[g * npg:(g + 1) * npg]
        best = None
        for a in range(npg):
            for b in range(a + 1, npg):
                s = v[a] + v[b]
                best = s if best is None else jnp.maximum(best, s)
        gscore.append(best)
    gb = jnp.zeros((tm, 1), jnp.int32)
    gv = gscore[0]
    for g in range(1, N_GROUPS):
        better = gscore[g] > gv
        gb = jnp.where(better, g, gb)
        gv = jnp.where(better, gscore[g], gv)
    u = [sc[j] for j in range(npg)]
    a4 = [aff[j] for j in range(npg)]
    for g in range(1, N_GROUPS):
        pick = gb == g
        u = [jnp.where(pick, sc[g * npg + j], u[j]) for j in range(npg)]
        a4 = [jnp.where(pick, aff[g * npg + j], a4[j]) for j in range(npg)]
    i1 = jnp.zeros((tm, 1), jnp.int32)
    v1 = u[0]
    for j in range(1, npg):
        better = u[j] > v1
        i1 = jnp.where(better, j, i1)
        v1 = jnp.where(better, u[j], v1)
    i2 = jnp.full((tm, 1), -1, jnp.int32)
    v2 = jnp.full((tm, 1), NEG_INF, F32)
    for j in range(npg):
        better = (i1 != j) & ((u[j] > v2) | (i2 < 0))
        i2 = jnp.where(better, j, i2)
        v2 = jnp.where(better, u[j], v2)
    w1 = a4[0]
    w2 = a4[0]
    for j in range(1, npg):
        w1 = jnp.where(i1 == j, a4[j], w1)
        w2 = jnp.where(i2 == j, a4[j], w2)
    tot = w1 + w2
    w1 = w1 / tot
    w2 = w2 / tot
    e1 = gb * npg + i1
    e2 = gb * npg + i2
    lane = lax.broadcasted_iota(jnp.int32, (tm, LANES), 1)
    out = jnp.where(lane == 0, e1.astype(F32), 0.0)
    out = jnp.where(lane == 1, e2.astype(F32), out)
    out = jnp.where(lane == 2, w1, out)
    return jnp.where(lane == 3, w2, out)


def _norm_router_kernel(x_ref, g_ref, sc_ref, sh_ref, wr_ref, rb_ref, h_ref, route_ref, *, tm):
    h = _modulate(_rms(x_ref[...], g_ref[...]), sc_ref, sh_ref, tm)
    h_ref[...] = h
    hh, hl = _split2(h)
    wh, wl = _split2(wr_ref[...])
    logits = _dot(hh, wh) + (_dot(hl, wh) + _dot(hh, wl))
    route_ref[...] = _route(logits, rb_ref)


def _norm_router(x, g, sc, sh, w_router_pad, router_bias, *, tm):
    t, d = x.shape
    return pl.pallas_call(
        functools.partial(_norm_router_kernel, tm=tm),
        grid=(t // tm,),
        in_specs=[pl.BlockSpec((tm, d), lambda i: (i, 0)),
                  pl.BlockSpec((1, d), lambda i: (0, 0)),
                  pl.BlockSpec((tm // CHUNK, 1, d), lambda i: (i, 0, 0)),
                  pl.BlockSpec((tm // CHUNK, 1, d), lambda i: (i, 0, 0)),
                  pl.BlockSpec((d, LANES), lambda i: (0, 0)),
                  pl.BlockSpec(memory_space=pltpu.SMEM)],
        out_specs=[pl.BlockSpec((tm, d), lambda i: (i, 0)),
                   pl.BlockSpec((tm, LANES), lambda i: (i, 0))],
        out_shape=[jax.ShapeDtypeStruct((t, d), F32), jax.ShapeDtypeStruct((t, LANES), F32)],
        compiler_params=_cparams(("arbitrary",)), name="norm_router",
    )(x, g.reshape(1, d), sc, sh, w_router_pad, router_bias)


TM_EXPERT = 256
TM_DISPATCH = 256


def _row_copy(src, src_row, dst, dst_row, sem):
    return pltpu.make_async_copy(src.at[pl.ds(src_row, 1), :], dst.at[pl.ds(dst_row, 1), :], sem)


def _rows_wait(src, dst, sem, n):
    pltpu.make_async_copy(src.at[pl.ds(0, n), :], dst.at[pl.ds(0, n), :], sem).wait()


def _dispatch_kernel(dest_ref, pad_ref, h_ref, xs_hbm, sems, *, tb, t, n_pad):
    i = pl.program_id(0)
    sem = sems.at[0]

    def body(r, c):
        tok = i * tb + r
        _row_copy(h_ref, r, xs_hbm, dest_ref[tok], sem).start()
        _row_copy(h_ref, r, xs_hbm, dest_ref[t + tok], sem).start()
        return c

    lax.fori_loop(0, tb, body, 0, unroll=8)
    for _ in range(2):
        _rows_wait(h_ref, xs_hbm, sem, tb)

    @pl.when(i == 0)
    def _():
        for c0 in range(0, n_pad, tb):
            nc = min(tb, n_pad - c0)

            def pad_body(r, c, c0=c0):
                _row_copy(h_ref, 0, xs_hbm, pad_ref[c0 + r], sem).start()
                return c

            lax.fori_loop(0, nc, pad_body, 0, unroll=8)
            _rows_wait(h_ref, xs_hbm, sem, nc)


def _expert_kernel(te_ref, nv_ref, x_ref, wg_ref, wu_ref, wd_ref, y_ref):
    del te_ref
    i = pl.program_id(0)

    @pl.when(i < nv_ref[0])
    def _():
        x = x_ref[...].astype(MXU)
        a = _dot(x, wg_ref[0, 0])
        b = _dot(x, wu_ref[0, 0])
        he = (a / (1.0 + jnp.exp(-a))) * b
        y_ref[...] = _dot(he.astype(MXU), wd_ref[0, 0])

    @pl.when(i >= nv_ref[0])
    def _():
        y_ref[...] = jnp.zeros_like(y_ref)


def _combine_kernel(dest_ref, y_hbm, route_ref, res_ref, g2_ref, o_ref, ybuf, sem, *, tm, t):
    i = pl.program_id(0)
    n = pl.num_programs(0)
    slot = i % 2

    def start(step, s):
        for j in range(2):
            def body(r, c):
                _row_copy(y_hbm, dest_ref[j * t + step * tm + r], ybuf.at[s, j], r, sem.at[s]).start()
                return c

            lax.fori_loop(0, tm, body, 0, unroll=8)

    @pl.when(i == 0)
    def _():
        start(0, 0)

    @pl.when(i + 1 < n)
    def _():
        start(i + 1, 1 - slot)

    for j in range(2):
        _rows_wait(y_hbm, ybuf.at[slot, j], sem.at[slot], tm)
    d = o_ref.shape[-1]
    route = route_ref[...]
    y = route[:, 2:3] * ybuf[slot, 0] + route[:, 3:4] * ybuf[slot, 1]
    o_ref[...] = res_ref[...] + (y.reshape(tm // CHUNK, CHUNK, d) * g2_ref[...]).reshape(tm, d)


def _cumsum_rows(x, blk):
    n, e = x.shape
    xb = x.reshape(n // blk, blk, e)
    tri = jnp.asarray(np.tril(np.ones((blk, blk), np.float32)))
    local = jnp.einsum('ij,bje->bie', tri, xb, precision=lax.Precision.HIGHEST)
    tot = local[:, -1, :]
    return (local + (jnp.cumsum(tot, axis=0) - tot)[:, None, :]).reshape(n, e)


def _moe(h, route, w_gate, w_up, w_down, layer, g2, res, *, tm):
    t, d = h.shape
    de = w_gate.shape[-1]
    te = _tile(2 * t, TM_EXPERT, CHUNK)
    n_tiles = 2 * t // te + N_EXPERTS
    r_pad = n_tiles * te
    n_pad = r_pad - 2 * t
    e_flat = route[:, 0:2].T.reshape(-1)
    experts = jnp.arange(N_EXPERTS, dtype=F32)
    onehot = (e_flat[:, None] == experts[None, :]).astype(F32)
    csum = _cumsum_rows(onehot, _tile(2 * t, 256, 8))
    counts = csum[-1]
    tiles_e = jnp.ceil(counts / te)
    tile_end = jnp.cumsum(tiles_e)
    seg_start = (tile_end - tiles_e) * te
    dest = jnp.sum(onehot * (seg_start[None, :] + csum - 1.0), axis=1).astype(jnp.int32)
    gap_start = jnp.concatenate([seg_start + counts, tile_end[-1:] * te])
    gap_len = jnp.concatenate([tiles_e * te - counts, r_pad - tile_end[-1:] * te])
    gap_first = jnp.cumsum(gap_len) - gap_len
    k = jnp.arange(n_pad, dtype=F32)[:, None]
    in_gap = (k >= gap_first[None, :]) & (k < (gap_first + gap_len)[None, :])
    pad_rows = jnp.sum(jnp.where(in_gap, gap_start[None, :] + k - gap_first[None, :], 0.0),
                       axis=1).astype(jnp.int32)
    tile_id = jnp.arange(n_tiles, dtype=F32)[:, None]
    tile_e = jnp.minimum(jnp.sum((tile_end[None, :] <= tile_id).astype(jnp.int32), axis=1),
                         N_EXPERTS - 1)
    n_valid = tile_end[-1:].astype(jnp.int32)

    tb = _tile(t, TM_DISPATCH, CHUNK)
    xs = pl.pallas_call(
        functools.partial(_dispatch_kernel, tb=tb, t=t, n_pad=n_pad),
        grid_spec=pltpu.PrefetchScalarGridSpec(
            num_scalar_prefetch=2, grid=(t // tb,),
            in_specs=[pl.BlockSpec((tb, d), lambda i, dst, pad: (i, 0))],
            out_specs=pl.BlockSpec(memory_space=pl.ANY),
            scratch_shapes=[pltpu.SemaphoreType.DMA((1,))]),
        out_shape=jax.ShapeDtypeStruct((r_pad, d), F32),
        compiler_params=_cparams(("arbitrary",)), name="moe_dispatch",
    )(dest, pad_rows, h)

    y = pl.pallas_call(
        _expert_kernel,
        grid_spec=pltpu.PrefetchScalarGridSpec(
            num_scalar_prefetch=2, grid=(n_tiles,),
            in_specs=[pl.BlockSpec((te, d), lambda i, e, n: (i, 0)),
                      pl.BlockSpec((1, 1, d, de), lambda i, e, n: (layer, e[i], 0, 0)),
                      pl.BlockSpec((1, 1, d, de), lambda i, e, n: (layer, e[i], 0, 0)),
                      pl.BlockSpec((1, 1, de, d), lambda i, e, n: (layer, e[i], 0, 0))],
            out_specs=pl.BlockSpec((te, d), lambda i, e, n: (i, 0))),
        out_shape=jax.ShapeDtypeStruct((r_pad, d), F32),
        compiler_params=_cparams(("arbitrary",)), name="moe_experts",
    )(tile_e, n_valid, xs, w_gate, w_up, w_down)

    return pl.pallas_call(
        functools.partial(_combine_kernel, tm=tm, t=t),
        grid_spec=pltpu.PrefetchScalarGridSpec(
            num_scalar_prefetch=1, grid=(t // tm,),
            in_specs=[pl.BlockSpec(memory_space=pl.ANY),
                      pl.BlockSpec((tm, LANES), lambda i, dst: (i, 0)),
                      pl.BlockSpec((tm, d), lambda i, dst: (i, 0)),
                      pl.BlockSpec((tm // CHUNK, 1, d), lambda i, dst: (i, 0, 0))],
            out_specs=pl.BlockSpec((tm, d), lambda i, dst: (i, 0)),
            scratch_shapes=[pltpu.VMEM((2, 2, tm, d), F32), pltpu.SemaphoreType.DMA((2,))]),
        out_shape=jax.ShapeDtypeStruct((t, d), F32),
        compiler_params=_cparams(("arbitrary",)), name="moe_combine",
    )(dest, y, route, res, g2)


def _rel_bucket(rel):
    half = N_BUCKETS // 2
    max_exact = half // 2
    n = jnp.abs(rel)
    nf = jnp.maximum(n, 1).astype(F32)
    large = max_exact + (jnp.log(nf / max_exact) / math.log(MAX_DISTANCE / max_exact)
                         * (half - max_exact)).astype(jnp.int32)
    large = jnp.minimum(large, half - 1)
    return jnp.where(rel > 0, half, 0) + jnp.where(n < max_exact, n, large)


def _bias_tiles(table, q_offsets, k_offset, tq, tk, n_heads):
    tiles = []
    for q0 in q_offsets:
        rel = (k_offset + np.arange(tk))[None, :] - (q0 + np.arange(tq))[:, None]
        bucket = _rel_bucket(jnp.asarray(rel, jnp.int32))
        onehot = (bucket[..., None] == jnp.arange(N_BUCKETS, dtype=jnp.int32)).astype(F32)
        tiles.append(jnp.einsum('qkb,bh->hqk', onehot, table[:, :n_heads].astype(F32),
                                precision=lax.Precision.HIGHEST))
    return jnp.stack(tiles)


def _bias_far(table, n_heads):
    rel = jnp.full((1,), -(MAX_DISTANCE + 1), jnp.int32)
    return table[_rel_bucket(rel)][0, :n_heads].astype(F32)


def _diff_attend(qr, far, prev, diag, bias_far, bp_ref, bd_ref, vis):
    outs = []
    for r in range(2):
        cols = slice(r * DA_QK, (r + 1) * DA_QK)
        parts = []
        if far is not None:
            s = _dot_nt(qr[r], far[0][:, cols].astype(MXU)) + bias_far[r]
            parts.append((s, far[1]))
        if prev is not None:
            s = _dot_nt(qr[r], prev[0][:, cols].astype(MXU)) + bp_ref[0, r]
            parts.append((s, prev[1]))
        s = _dot_nt(qr[r], diag[0][:, cols].astype(MXU)) + bd_ref[0, r]
        parts.append((jnp.where(vis, s, NEG_INF), diag[1]))
        m = None
        for s, _ in parts:
            mx = jnp.max(s, axis=-1, keepdims=True)
            m = mx if m is None else jnp.maximum(m, mx)
        l = None
        acc = None
        for s, v in parts:
            p = jnp.exp(s - m)
            ps = jnp.sum(p, axis=-1, keepdims=True)
            pv = _dot(p.astype(MXU), v.astype(MXU))
            l = ps if l is None else l + ps
            acc = pv if acc is None else acc + pv
        outs.append(acc / l)
    return outs


def _stick_attend(qb, blocks, tri_ref, trid_ref, carry=None):
    acc = None
    for kb, vb, mask in blocks:
        tk = kb.shape[0]
        z = _dot_nt(qb, kb.astype(MXU)) * HEAD_DIM ** -0.5
        log_beta = jnp.minimum(z, 0.0) - jnp.log1p(jnp.exp(-jnp.abs(z)))
        log_keep = log_beta - z
        if mask is not None:
            log_keep = jnp.where(mask, log_keep, 0.0)
        tri = (trid_ref if tk != TK else tri_ref)[...]
        hi, lo = _split2(log_keep)
        later = _dot(hi, tri) + _dot(lo, tri)
        total = jnp.broadcast_to(later[:, 0:1] + log_keep[:, 0:1], (later.shape[0], LANES))
        if carry is not None:
            later = later + (carry[:, :tk] if tk <= LANES else
                             jnp.concatenate([carry] * (tk // LANES), axis=1))
        a = jnp.exp(log_beta + later)
        if mask is not None:
            a = jnp.where(mask, a, 0.0)
        pv = _dot(a.astype(MXU), vb.astype(MXU))
        acc = pv if acc is None else acc + pv
        carry = total if carry is None else carry + total
    return acc, carry


def _even_attn_kernel(far_ref, lam_ref, q_ref, kn_ref, vn_ref, bd_ref, bp_ref, g_ref, tri_ref, o_ref,
                      *, tq, nq, lam_init):
    u = pl.program_id(1)
    i = pl.program_id(2)
    q = q_ref[0]
    qq = lax.broadcasted_iota(jnp.int32, (tq, TK), 0)
    kk = lax.broadcasted_iota(jnp.int32, (tq, TK), 1)

    def regions(c):
        far = (0, (c - 1) * TK) if c >= 2 else None
        prev = ((c - 1) * TK, c * TK) if c >= 1 else None
        return far, prev, (c * TK, (c + 1) * TK)

    def load(reg, lo=None, hi=None):
        a, b = reg
        lo, hi = (a, b) if lo is None else (a + lo, a + hi)
        return kn_ref[0, lo:hi, :], vn_ref[0, lo:hi, :]

    def diff(c):
        far, prev, diag = regions(c)
        qr = [(q[:, r * DA_QK:(r + 1) * DA_QK] * DA_QK ** -0.5).astype(MXU) for r in range(2)]
        vis = (kk // CHUNK) <= (qq // CHUNK)
        o0, o1 = _diff_attend(qr, None if far is None else load(far), None if prev is None else load(prev),
                              load(diag), [far_ref[2 * u], far_ref[2 * u + 1]], bp_ref, bd_ref, vis)
        o = _rms(o0 - lam_ref[0] * o1, g_ref[...]) * (1.0 - lam_init)
        o_ref[...] = o.astype(o_ref.dtype)

    def stick(c):
        far, prev, diag = regions(c)
        blocks = [load(diag) + (kk < qq,)]
        if prev is not None:
            blocks.append(load(prev) + (None,))
        if far is not None:
            for j in range((far[1] - far[0]) // TK - 1, -1, -1):
                blocks.append(load(far, j * TK, (j + 1) * TK) + (None,))
        o_ref[...] = _stick_attend(q.astype(MXU), blocks, tri_ref, tri_ref)[0].astype(o_ref.dtype)

    for c in range(nq):
        pl.when((u < N_HEADS_A) & (i == c))(functools.partial(diff, c))
        pl.when((u >= N_HEADS_A) & (i == c))(functools.partial(stick, c))


def _tri_later(tk):
    return jnp.asarray(np.arange(tk)[:, None] > np.arange(tk)[None, :], MXU)


def _even_attn(qkv, table, lam, subln_g, lam_init, *, bsz, n):
    tq = TK
    assert n % tq == 0
    nq = n // tq
    far = _bias_far(table, N_MAPS_A)
    bd = _bias_tiles(table, [0], 0, tq, TK, N_MAPS_A)
    bp = _bias_tiles(table, [0], -TK, tq, TK, N_MAPS_A)

    def seg(u, base):
        return jnp.where(u < 8, base, base + 3)

    smem = pl.BlockSpec(memory_space=pltpu.SMEM)
    in_specs = [smem, smem,
                pl.BlockSpec((1, tq, HEAD_DIM), lambda b, u, i: (seg(u, 0), b * nq + i, u % 8)),
                pl.BlockSpec((1, n, HEAD_DIM), lambda b, u, i: (seg(u, 1), b, u % 8)),
                pl.BlockSpec((1, n, HEAD_DIM), lambda b, u, i: (seg(u, 2), b, u % 8)),
                pl.BlockSpec((1, 2, tq, TK), lambda b, u, i: (0, jnp.minimum(u, 7), 0, 0)),
                pl.BlockSpec((1, 2, tq, TK), lambda b, u, i: (0, jnp.minimum(u, 7), 0, 0)),
                pl.BlockSpec((1, HEAD_DIM), lambda b, u, i: (0, 0)),
                pl.BlockSpec((TK, TK), lambda b, u, i: (0, 0))]
    return pl.pallas_call(
        functools.partial(_even_attn_kernel, tq=tq, nq=nq, lam_init=lam_init),
        grid=(bsz, 16, nq), in_specs=in_specs,
        out_specs=pl.BlockSpec((tq, HEAD_DIM), lambda b, u, i: (b * nq + i, u)),
        out_shape=jax.ShapeDtypeStruct((bsz * n, 16 * HEAD_DIM), MXU),
        compiler_params=_cparams(("arbitrary", "arbitrary", "arbitrary")), name="even_attn",
    )(far, lam.reshape(1), qkv, qkv, qkv, bd, bp, subln_g.reshape(1, HEAD_DIM), _tri_later(TK))


def _even_attn_past_kernel(lam_ref, q_ref, kn_ref, vn_ref, ak_hbm, av_hbm, bk_hbm, bv_hbm, bd_ref, bc_ref,
                           g_ref, tri_ref, trid_ref, o_ref, m_ref, l_ref, acc_ref, carry_ref,
                           ka_buf, kb_buf, v_buf, sem, *, n, tkc, nkc, lam_init):
    b = pl.program_id(0)
    mix = pl.program_id(1)
    kc = pl.program_id(2)
    step = (b * 2 + mix) * nkc + kc
    n_steps = pl.num_programs(0) * 2 * nkc
    slot = step % 2
    qq = lax.broadcasted_iota(jnp.int32, (n, n), 0)
    kk = lax.broadcasted_iota(jnp.int32, (n, n), 1)

    def chunk_copies(bb, mm_is_stick, cc, s):
        if mm_is_stick:
            rows = pl.ds(pl.multiple_of((nkc - 1 - cc) * tkc, tkc), tkc)
            return ([pltpu.make_async_copy(bk_hbm.at[bb, rows, h, :], kb_buf.at[s, h], sem.at[s])
                     for h in range(N_HEADS_B)] +
                    [pltpu.make_async_copy(bv_hbm.at[bb, rows, h, :], v_buf.at[s, h], sem.at[s])
                     for h in range(N_HEADS_B)])
        rows = pl.ds(pl.multiple_of(cc * tkc, tkc), tkc)
        return ([pltpu.make_async_copy(ak_hbm.at[bb, rows, mp, :], ka_buf.at[s, mp], sem.at[s])
                 for mp in range(N_MAPS_A)] +
                [pltpu.make_async_copy(av_hbm.at[bb, rows, h, :], v_buf.at[s, h], sem.at[s])
                 for h in range(N_HEADS_A)])

    def start(bb, mm, cc, s):
        for stick in (False, True):
            @pl.when(mm == int(stick))
            def _(stick=stick):
                for cp in chunk_copies(bb, stick, cc, s):
                    cp.start()

    @pl.when(step == 0)
    def _():
        start(b, mix, kc, slot)

    @pl.when(step + 1 < n_steps)
    def _():
        nxt = step + 1
        start(nxt // (2 * nkc), (nxt // nkc) % 2, nxt % nkc, 1 - slot)

    for stick in (False, True):
        @pl.when(mix == int(stick))
        def _(stick=stick):
            for cp in chunk_copies(b, stick, kc, slot):
                cp.wait()

    def head(ref, h, width=HEAD_DIM, off=0):
        return ref[0, :, h * HEAD_DIM + off:h * HEAD_DIM + off + width]

    def diff_update(s, vs, first):
        mx = jnp.max(s, axis=-1, keepdims=True)
        m_new = mx if first else jnp.maximum(m_ref[...], mx)
        p = jnp.exp(s - m_new)
        ps = jnp.sum(p, axis=-1, keepdims=True)
        pv = jnp.stack([_dot(p[mp].astype(MXU), vs[mp // 2].astype(MXU)) for mp in range(N_MAPS_A)])
        if first:
            l_ref[...] = ps
            acc_ref[...] = pv
        else:
            alpha = jnp.exp(m_ref[...] - m_new)
            l_ref[...] = alpha * l_ref[...] + ps
            acc_ref[...] = alpha * acc_ref[...] + pv
        m_ref[...] = m_new

    def diff_q(mp):
        h, r = divmod(mp, 2)
        return (head(q_ref, h, DA_QK, r * DA_QK) * DA_QK ** -0.5).astype(MXU)

    @pl.when((mix == 0) & (kc == 0))
    def _():
        vis = (kk // CHUNK) <= (qq // CHUNK)
        s = jnp.stack([_dot_nt(diff_q(mp), head(kn_ref, mp // 2, DA_QK, (mp % 2) * DA_QK).astype(MXU))
                       for mp in range(N_MAPS_A)]) + bd_ref[0]
        diff_update(jnp.where(vis[None], s, NEG_INF), [head(vn_ref, h) for h in range(N_HEADS_A)], True)

    @pl.when(mix == 0)
    def _():
        s = jnp.stack([_dot_nt(diff_q(mp), ka_buf[slot, mp].astype(MXU))
                       for mp in range(N_MAPS_A)]) + bc_ref[0]
        diff_update(s, [v_buf[slot, h] for h in range(N_HEADS_A)], False)

    @pl.when((mix == 0) & (kc == nkc - 1))
    def _():
        for h in range(N_HEADS_A):
            o = acc_ref[2 * h] / l_ref[2 * h] - lam_ref[0] * (acc_ref[2 * h + 1] / l_ref[2 * h + 1])
            o = _rms(o, g_ref[...]) * (1.0 - lam_init)
            o_ref[:, h * HEAD_DIM:(h + 1) * HEAD_DIM] = o.astype(o_ref.dtype)

    def stick_q(h):
        return head(q_ref, h).astype(MXU)

    @pl.when((mix == 1) & (kc == 0))
    def _():
        res = [_stick_attend(stick_q(h), [(head(kn_ref, h), head(vn_ref, h), kk < qq)], tri_ref, trid_ref)
               for h in range(N_HEADS_B)]
        acc_ref[:N_HEADS_B] = jnp.stack([r[0] for r in res])
        carry_ref[...] = jnp.stack([r[1] for r in res])

    @pl.when(mix == 1)
    def _():
        carry = carry_ref[...]
        res = []
        for h in range(N_HEADS_B):
            blocks = [(kb_buf[slot, h, j * TK:(j + 1) * TK, :], v_buf[slot, h, j * TK:(j + 1) * TK, :], None)
                      for j in range(tkc // TK - 1, -1, -1)]
            res.append(_stick_attend(stick_q(h), blocks, tri_ref, trid_ref, carry[h]))
        acc_ref[:N_HEADS_B] = acc_ref[:N_HEADS_B] + jnp.stack([r[0] for r in res])
        carry_ref[...] = jnp.stack([r[1] for r in res])

    @pl.when((mix == 1) & (kc == nkc - 1))
    def _():
        for h in range(N_HEADS_B):
            o_ref[:, h * HEAD_DIM:(h + 1) * HEAD_DIM] = acc_ref[h].astype(o_ref.dtype)


def _even_attn_past(qkv, past, table, lam, subln_g, lam_init, *, bsz, n):
    ak, av, bk, bv = past
    p_len = ak.shape[1]
    tkc = _tile(p_len, TKC, TK)
    nkc = p_len // tkc
    assert n <= CHUNK and n % 8 == 0
    bd = _bias_tiles(table, [0], 0, n, n, N_MAPS_A)
    far = jnp.broadcast_to(_bias_far(table, N_MAPS_A)[:, None, None], (N_MAPS_A, n, tkc))
    bc = jnp.stack([far, _bias_tiles(table, [0], -tkc, n, tkc, N_MAPS_A)[0]])

    def row(s):
        return lambda b, mix, kc: (3 * mix + s, b, 0)

    width = N_HEADS_A * HEAD_DIM
    in_specs = [pl.BlockSpec(memory_space=pltpu.SMEM),
                pl.BlockSpec((1, n, width), row(0)),
                pl.BlockSpec((1, n, width), row(1)),
                pl.BlockSpec((1, n, width), row(2)),
                pl.BlockSpec(memory_space=pl.ANY), pl.BlockSpec(memory_space=pl.ANY),
                pl.BlockSpec(memory_space=pl.ANY), pl.BlockSpec(memory_space=pl.ANY),
                pl.BlockSpec((1, N_MAPS_A, n, n), lambda b, mix, kc: (0, 0, 0, 0)),
                pl.BlockSpec((1, N_MAPS_A, n, tkc), lambda b, mix, kc: ((kc + 1) // nkc, 0, 0, 0)),
                pl.BlockSpec((1, HEAD_DIM), lambda b, mix, kc: (0, 0)),
                pl.BlockSpec((TK, TK), lambda b, mix, kc: (0, 0)),
                pl.BlockSpec((n, n), lambda b, mix, kc: (0, 0))]
    return pl.pallas_call(
        functools.partial(_even_attn_past_kernel, n=n, tkc=tkc, nkc=nkc, lam_init=lam_init),
        grid=(bsz, 2, nkc), in_specs=in_specs,
        out_specs=pl.BlockSpec((n, width), lambda b, mix, kc: (b, mix)),
        out_shape=jax.ShapeDtypeStruct((bsz * n, 2 * width), MXU),
        scratch_shapes=[pltpu.VMEM((N_MAPS_A, n, 1), F32), pltpu.VMEM((N_MAPS_A, n, 1), F32),
                        pltpu.VMEM((N_MAPS_A, n, HEAD_DIM), F32), pltpu.VMEM((N_HEADS_B, n, LANES), F32),
                        pltpu.VMEM((2, N_MAPS_A, tkc, DA_QK), F32), pltpu.VMEM((2, N_HEADS_B, tkc, HEAD_DIM), F32),
                        pltpu.VMEM((2, N_HEADS_A, tkc, HEAD_DIM), F32), pltpu.SemaphoreType.DMA((2,))],
        compiler_params=_cparams(("arbitrary", "arbitrary", "arbitrary")), name="even_attn_past",
    )(lam.reshape(1), qkv, qkv, qkv, ak, av, bk, bv, bd, bc, subln_g.reshape(1, HEAD_DIM),
      _tri_later(TK), _tri_later(n))


INT_MIN = int(np.iinfo(np.int32).min)
NEG_INF_KEY = int(np.array(-np.inf, np.float32).view(np.int32)) ^ 0x7FFFFFFF
INDEX_BITS = 15


def _sort_key(x):
    b = lax.bitcast_convert_type(x, jnp.int32)
    return b ^ ((b >> 31) & 0x7FFFFFFF)


def _indexer_kernel(*refs, tq, n, p_len, top_k):
    has_past = p_len > 0
    it = iter(refs)
    qi_ref, wi_ref, kin_ref = next(it), next(it), next(it)
    kip_ref = next(it) if has_past else None
    seln_ref = next(it)
    selp_ref = next(it) if has_past else None
    keyn_ref = next(it)
    keyp_ref = next(it) if has_past else None

    i = pl.program_id(1)
    qi = qi_ref[...]
    wi = wi_ref[...] * (N_IDX_HEADS ** -0.5 * D_IDX ** -0.5)

    def scores(kmat):
        kb = kmat.astype(MXU)
        acc = None
        for ih in range(N_IDX_HEADS):
            d = _dot_nt(qi[:, ih * D_IDX:(ih + 1) * D_IDX], kb)
            term = wi[:, ih:ih + 1] * jnp.maximum(d, 0.0)
            acc = term if acc is None else acc + term
        return acc

    def run(w_new):
        s_new = scores(kin_ref[:w_new, :])
        qq = i * tq + lax.broadcasted_iota(jnp.int32, (tq, w_new), 0)
        kk = lax.broadcasted_iota(jnp.int32, (tq, w_new), 1)
        s_new = jnp.where((kk // CHUNK) <= (qq // CHUNK), s_new, NEG_INF)
        keyn_ref[:, :w_new] = _sort_key(s_new)
        parts = [(keyn_ref, p_len, w_new, seln_ref)]
        if has_past:
            keyp_ref[...] = _sort_key(scores(kip_ref[0]))
            parts.append((keyp_ref, 0, p_len, selp_ref))

        def count(pred):
            tot = None
            for ref, base, width, _ in parts:
                idx = base + lax.broadcasted_iota(jnp.int32, (tq, width), 1)
                c = jnp.sum(jnp.where(pred(ref[:, :width], idx), 1.0, 0.0), axis=-1, keepdims=True)
                tot = c if tot is None else tot + c
            return tot

        def body(b, t):
            cand = t + jnp.left_shift(jnp.int32(1), 31 - b)
            c = count(lambda k, idx: k >= cand)
            return jnp.where(c >= top_k, cand, t)

        t = lax.fori_loop(0, 32, body, jnp.full((tq, 1), INT_MIN, jnp.int32))
        n_gt = count(lambda k, idx: k > t)
        n_eq = count(lambda k, idx: k == t)
        need = top_k - n_gt
        tie = jnp.where((n_eq != need) & (t > NEG_INF_KEY), 1.0, 0.0)

        def write(sel_of):
            for ref, base, width, out_ref in parts:
                k = ref[:, :width]
                idx = base + lax.broadcasted_iota(jnp.int32, (tq, width), 1)
                out_ref[:, :width] = jnp.where(sel_of(k, idx) & (k > NEG_INF_KEY), 1.0, 0.0)

        write(lambda k, idx: k >= t)
        if w_new < n:
            seln_ref[:, w_new:] = jnp.zeros((tq, n - w_new), F32)

        @pl.when(jnp.max(tie) > 0.0)
        def _():
            def jbody(b, jv):
                cand = jv + jnp.left_shift(jnp.int32(1), INDEX_BITS - 1 - b)
                c = count(lambda k, idx: (k == t) & (idx < cand))
                return jnp.where(c < need, cand, jv)

            jv = lax.fori_loop(0, INDEX_BITS, jbody, jnp.zeros((tq, 1), jnp.int32))
            write(lambda k, idx: (k > t) | ((k == t) & (idx <= jv)))

    if has_past or n <= TK:
        run(n)
    else:
        for c in range(n // TK):
            pl.when((i * tq) // TK == c)(functools.partial(run, (c + 1) * TK))


def _indexer(q_i, proj, k_idx_past, *, bsz, n, top_k):
    has_past = k_idx_past is not None
    tq = min(n, 128)
    nq = n // tq
    p_len = k_idx_past.shape[1] if has_past else 0
    assert n + p_len < 2 ** (INDEX_BITS - 1) and (n <= TK or n % TK == 0)
    t = bsz * n
    in_specs = [pl.BlockSpec((tq, N_IDX_HEADS * D_IDX), lambda b, i: (b * nq + i, 0)),
                pl.BlockSpec((tq, LANES), lambda b, i: (b * nq + i, 9)),
                pl.BlockSpec((n, D_IDX), lambda b, i: (b, 8))]
    args = [q_i, proj, proj]
    out_specs = [pl.BlockSpec((tq, n), lambda b, i: (b * nq + i, 0))]
    out_shape = [jax.ShapeDtypeStruct((t, n), F32)]
    scratch = [pltpu.VMEM((tq, n), jnp.int32)]
    if has_past:
        in_specs.append(pl.BlockSpec((1, p_len, D_IDX), lambda b, i: (b, 0, 0)))
        args.append(k_idx_past)
        out_specs.append(pl.BlockSpec((tq, p_len), lambda b, i: (b * nq + i, 0)))
        out_shape.append(jax.ShapeDtypeStruct((t, p_len), F32))
        scratch.append(pltpu.VMEM((tq, p_len), jnp.int32))
    res = pl.pallas_call(
        functools.partial(_indexer_kernel, tq=tq, n=n, p_len=p_len, top_k=top_k),
        grid=(bsz, nq), in_specs=in_specs, out_specs=out_specs, out_shape=out_shape,
        scratch_shapes=scratch,
        compiler_params=_cparams(("arbitrary", "arbitrary")),
        name="indexer_past" if has_past else "indexer",
    )(*args)
    return res[0], (res[1] if has_past else None)


def _sparse_attn_kernel(*refs, tq, tkd, n, p_len):
    has_past = p_len > 0
    it = iter(refs)
    far_ref, q_ref, kvn_ref, seln_ref = (next(it) for _ in range(4))
    kvp_ref, selp_ref = (next(it), next(it)) if has_past else (None, None)
    bd_ref, bp_ref, o_ref = (next(it) for _ in range(3))
    nh = N_HEADS_C
    i = pl.program_id(1)
    q = q_ref[...].reshape(nh * tq, KV_LORA)

    def attend(regions):
        parts = []
        for kvb, sel, bias in regions:
            kvb = kvb.astype(MXU)
            w = kvb.shape[0]
            s = _dot_nt(q, kvb).reshape(nh, tq, w) * HEAD_DIM ** -0.5 + bias
            parts.append((jnp.where(sel[None] > 0.0, s, NEG_INF), kvb))
        m = None
        for s, _ in parts:
            mx = jnp.max(s, axis=-1, keepdims=True)
            m = mx if m is None else jnp.maximum(m, mx)
        l = None
        acc = None
        for s, kvb in parts:
            w = kvb.shape[0]
            p = jnp.exp(s - m)
            ps = jnp.sum(p, axis=-1, keepdims=True)
            pv = _dot(p.reshape(nh * tq, w).astype(MXU), kvb)
            l = ps if l is None else l + ps
            acc = pv if acc is None else acc + pv
        o_ref[...] = (acc.reshape(nh, tq, KV_LORA) / l).astype(o_ref.dtype)

    if has_past:
        regions = []
        if p_len > TK:
            regions.append((kvp_ref[0, :p_len - TK, :], selp_ref[:, :p_len - TK], far_ref[...]))
        regions.append((kvp_ref[0, p_len - TK:, :], selp_ref[:, p_len - TK:], bp_ref[0]))
        regions.append((kvn_ref[...], seln_ref[...], bd_ref[0]))
        attend(regions)
    else:
        def run(c):
            regions = []
            if c >= 2:
                regions.append((kvn_ref[:(c - 1) * TK, :], seln_ref[:, :(c - 1) * TK], far_ref[...]))
            if c >= 1:
                regions.append((kvn_ref[(c - 1) * TK:c * TK, :], seln_ref[:, (c - 1) * TK:c * TK], bp_ref[0]))
            regions.append((kvn_ref[c * TK:(c + 1) * TK, :], seln_ref[:, c * TK:(c + 1) * TK], bd_ref[0]))
            attend(regions)

        for c in range(n // TK):
            pl.when((i * tq) // TK == c)(functools.partial(run, c))


def _sparse_attn(q_lat, kv_lat, sel_new, kv_past, sel_past, table, *, bsz, n):
    has_past = kv_past is not None
    nh = N_HEADS_C
    t = bsz * n
    if has_past:
        tq, tkd = min(n, TQ_SPARSE_PAST), n
        p_len = kv_past.shape[1]
        assert p_len % TK == 0 and n <= TK
    else:
        tq, tkd = min(n, TQ_SPARSE), TK
        p_len = 0
        assert n % TK == 0
    q_offsets = list(range(0, tkd, tq))
    nq = n // tq
    npar = len(q_offsets)
    far = _bias_far(table, nh).reshape(nh, 1, 1)
    bd = _bias_tiles(table, q_offsets, 0, tq, tkd, nh)
    bp = _bias_tiles(table, q_offsets, -TK, tq, TK, nh)
    in_specs = [pl.BlockSpec((nh, 1, 1), lambda b, i: (0, 0, 0)),
                pl.BlockSpec((nh, tq, KV_LORA), lambda b, i: (0, b * nq + i, 0)),
                pl.BlockSpec((n, KV_LORA), lambda b, i: (b, 0)),
                pl.BlockSpec((tq, n), lambda b, i: (b * nq + i, 0))]
    args = [far, q_lat, kv_lat, sel_new]
    if has_past:
        in_specs += [pl.BlockSpec((1, p_len, KV_LORA), lambda b, i: (b, 0, 0)),
                     pl.BlockSpec((tq, p_len), lambda b, i: (b * nq + i, 0))]
        args += [kv_past, sel_past]
    in_specs += [pl.BlockSpec((1, nh, tq, tkd), lambda b, i: (i % npar, 0, 0, 0)),
                 pl.BlockSpec((1, nh, tq, TK), lambda b, i: (i % npar, 0, 0, 0))]
    args += [bd, bp]
    return pl.pallas_call(
        functools.partial(_sparse_attn_kernel, tq=tq, tkd=tkd, n=n, p_len=p_len),
        grid=(bsz, nq), in_specs=in_specs,
        out_specs=pl.BlockSpec((nh, tq, KV_LORA), lambda b, i: (0, b * nq + i, 0)),
        out_shape=jax.ShapeDtypeStruct((nh, t, KV_LORA), MXU),
        compiler_params=_cparams(("arbitrary", "arbitrary")),
        name="sparse_attn_past" if has_past else "sparse_attn",
    )(*args)


def _tile(n, cap, mult):
    best = None
    for t in range(mult, min(n, cap) + 1, mult):
        if n % t == 0:
            best = t
    assert best is not None, (n, cap, mult)
    return best


def _per_chunk(v, reps):
    return jnp.repeat(v, reps, axis=0)[:, None, :]


def _trunk(x, mod, past, p, w):
    bsz, n, d = x.shape
    t = bsz * n
    x = x.reshape(t, d)
    tm = _tile(t, 512, CHUNK)
    tmn = _tile(t, 256, CHUNK)
    tn_d = _tile(d, 1024, LANES)
    depth = p['w_ada'].shape[0]
    p_len = 0 if past is None else past[0].shape[2]
    top_k = min(TOPK_MAX, (p_len + n) // 4)
    even_rows, odd_rows = [], []
    for l in range(depth):
        m6 = mod[l].reshape(bsz, 6, d)
        sh1, sc1, g1, sh2, sc2, g2 = (_per_chunk(m6[:, k], n // CHUNK) for k in range(6))
        h = _norm(x, p['norm_mix_g'][l], tm=tmn, sc=sc1, sh=sh1, out_dtype=MXU)
        if l % 2 == 0:
            i = l // 2
            lam_init = 0.8 - 0.6 * math.exp(-0.3 * l)
            qkv = _mm(h, p['w_in_even'][i], n_out=48 * LANES, tm=tm, tn=1024, seg_out=True, name="mm_qkv")
            past_i = None if past is None else tuple(a[i] for a in past[:4])
            lam = (jnp.exp(jnp.sum(p['lam_q1'][i].astype(F32) * p['lam_k1'][i].astype(F32)))
                   - jnp.exp(jnp.sum(p['lam_q2'][i].astype(F32) * p['lam_k2'][i].astype(F32)))
                   + lam_init)
            if past is None:
                o = _even_attn(qkv, p['rel_bias_table'], lam, p['subln_g'][i], lam_init, bsz=bsz, n=n)
            else:
                o = _even_attn_past(qkv, past_i, p['rel_bias_table'], lam, p['subln_g'][i], lam_init,
                                    bsz=bsz, n=n)
            even_rows.append((qkv[1].reshape(bsz, n, N_MAPS_A, DA_QK),
                              qkv[2].reshape(bsz, n, N_HEADS_A, HEAD_DIM),
                              qkv[4].reshape(bsz, n, N_HEADS_B, HEAD_DIM),
                              qkv[5].reshape(bsz, n, N_HEADS_B, HEAD_DIM)))
            x = _mm(o, p['w_out_even'][i], n_out=d, tm=tm, tn=tn_d, gate=g1, res=x)
        else:
            j = l // 2
            proj = _mm(h, w['w_in_odd'][j], n_out=10 * LANES, tm=tm, tn=10 * LANES)
            c_q = _norm(proj, p['g_q'][j], tm=tmn, width=Q_LORA, col_block=0, out_dtype=MXU)
            kv_lat = _norm(proj, p['g_kv'][j], tm=tmn, width=KV_LORA, col_block=1)
            k_i = proj[:, 1024:1152]
            q = _mm(c_q, p['w_uq'][j], n_out=N_HEADS_C * HEAD_DIM, tm=tm, tn=1024, out_dtype=MXU)
            q_lat = _mm_heads_out(q, w['w_uk_t'][j], tm=_tile(t, TM_HEADS, CHUNK))
            q_i = _mm(c_q, p['w_qidx'][j], n_out=N_IDX_HEADS * D_IDX, tm=tm, tn=1024, out_dtype=MXU)
            kv_past = None if past is None else past[4][j]
            ki_past = None if past is None else past[5][j]
            sel_new, sel_past = _indexer(q_i, proj, ki_past, bsz=bsz, n=n, top_k=top_k)
            o_lat = _sparse_attn(q_lat, kv_lat, sel_new, kv_past, sel_past, p['rel_bias_table'],
                                 bsz=bsz, n=n)
            o = _mm_heads_in(o_lat, w['w_uv_h'][j], tm=_tile(t, TM_HEADS, CHUNK))
            odd_rows.append((kv_lat.reshape(bsz, n, KV_LORA), k_i.reshape(bsz, n, D_IDX)))
            x = _mm(o, p['w_out_odd'][j], n_out=d, tm=tm, tn=tn_d, gate=g1, res=x)
        h2, route = _norm_router(x, p['norm_ffn_g'][l], sc2, sh2, w['w_router'], p['router_bias'], tm=tmn)
        x = _moe(h2, route, w['w_gate'], w['w_up'], w['w_down'], l, g2, x, tm=tmn)
    y = _norm(x, p['final_norm_g'], tm=tmn).reshape(bsz, n, d)
    ev = tuple(jnp.stack([r[m] for r in even_rows]) for m in range(4))
    od = tuple(jnp.stack([r[m] for r in odd_rows]) for m in range(2))
    return y, ev, od


def kernel(x_prompt, x_sample, cache_a_k, cache_a_v, cache_b_k, cache_b_v, cache_c_kv, cache_c_idx,
           c_prompt, c_sample, rel_bias_table, norm_mix_g, norm_ffn_g, final_norm_g, w_ada, b_ada,
           w_in_even, lam_q1, lam_k1, lam_q2, lam_k2, subln_g, w_out_even, w_in_odd, g_q, g_kv,
           w_uq, w_qidx, w_uk, w_uv, w_out_odd, w_router, router_bias, w_gate, w_up, w_down):
    p = dict(rel_bias_table=rel_bias_table, norm_mix_g=norm_mix_g, norm_ffn_g=norm_ffn_g,
             final_norm_g=final_norm_g, w_ada=w_ada, b_ada=b_ada, w_in_even=w_in_even,
             lam_q1=lam_q1, lam_k1=lam_k1, lam_q2=lam_q2, lam_k2=lam_k2, subln_g=subln_g,
             w_out_even=w_out_even, g_q=g_q, g_kv=g_kv, w_uq=w_uq, w_qidx=w_qidx,
             w_out_odd=w_out_odd, router_bias=router_bias)
    d = x_prompt.shape[-1]
    depth = w_ada.shape[0]
    w = dict(
        w_in_odd=jnp.pad(w_in_odd, ((0, 0), (0, 0), (0, 10 * LANES - w_in_odd.shape[-1]))),
        w_uk_t=jnp.transpose(w_uk, (0, 2, 3, 1)),
        w_uv_h=jnp.transpose(w_uv, (0, 2, 1, 3)),
        w_router=jnp.pad(w_router, ((0, 0), (0, LANES - N_EXPERTS))),
        w_gate=w_gate.astype(MXU), w_up=w_up.astype(MXU), w_down=w_down.astype(MXU))
    nb_p, nb_s = c_prompt.shape[0], c_sample.shape[0]
    c_all = jnp.concatenate([c_prompt, c_sample], axis=0)
    rows = -(-(nb_p + nb_s) // 16) * 16
    c_act = jnp.pad(c_all * (1.0 / (1.0 + jnp.exp(-c_all))), ((0, rows - nb_p - nb_s), (0, 0)))
    tn_ada = _tile(6 * d, 1024, LANES)
    mods = [_mm(c_act, w_ada, layer=l, n_out=6 * d, tm=rows, tn=tn_ada, bias=b_ada[l], name="mm_ada")
            for l in range(depth)]
    mod_p = [m[:nb_p] for m in mods]
    mod_s = [m[nb_p:nb_p + nb_s] for m in mods]

    y_prompt, ev_p, od_p = _trunk(x_prompt, mod_p, None, p, w)
    past = (cache_a_k, cache_a_v, cache_b_k, cache_b_v, cache_c_kv, cache_c_idx)
    y_sample, ev_s, od_s = _trunk(x_sample, mod_s, past, p, w)
    return (y_prompt, y_sample) + ev_p + od_p + ev_s + od_s
```

```python
import functools
import math

import jax
import jax.numpy as jnp
import numpy as np
from jax import lax
from jax.experimental import pallas as pl
from jax.experimental.pallas import tpu as pltpu

F32 = jnp.float32
MXU = jnp.bfloat16

CHUNK = 64
HEAD_DIM = 128
N_HEADS_A = 8
N_MAPS_A = 16
DA_QK = 64
N_HEADS_B = 8
N_HEADS_C = 16
Q_LORA = 512
KV_LORA = 512
N_IDX_HEADS = 16
D_IDX = 128
TOPK_MAX = 256
N_BUCKETS = 32
MAX_DISTANCE = 128
N_EXPERTS = 16
N_GROUPS = 4
EXPERTS_PER_GROUP = 4
EPS = 1e-6
LANES = 128
TK = 256
TQ_SPARSE = 64
TQ_SPARSE_PAST = 32
TKC = 512
TM_HEADS = 2048
VMEM_LIMIT = 56 * 1024 * 1024
NEG_INF = float("-inf")


def _cparams(sem):
    return pltpu.CompilerParams(dimension_semantics=sem, vmem_limit_bytes=VMEM_LIMIT)


def _dot(a, b):
    return jnp.dot(a, b, preferred_element_type=F32)


def _dot_nt(a, b):
    return lax.dot_general(a, b, (((1,), (1,)), ((), ())), preferred_element_type=F32)


def _mm_kernel(*refs, has_bias, has_res, tm):
    it = iter(refs)
    a_ref, b_ref = next(it), next(it)
    bias_ref = next(it) if has_bias else None
    gate_ref, res_ref = (next(it), next(it)) if has_res else (None, None)
    o_ref, bsc = next(it), next(it)

    @pl.when(pl.program_id(1) == 0)
    def _():
        bsc[...] = b_ref[...].reshape(bsc.shape).astype(bsc.dtype)

    a = a_ref[...]
    a = a.reshape(a.shape[-2:]).astype(MXU)
    acc = _dot(a, bsc[...])
    if has_bias:
        acc = acc + bias_ref[...]
    if has_res:
        tn = acc.shape[-1]
        acc = (acc.reshape(tm // CHUNK, CHUNK, tn) * gate_ref[...]).reshape(tm, tn)
        acc = acc + res_ref[...]
    o_ref[...] = acc.reshape(o_ref.shape).astype(o_ref.dtype)


def _mm(a, b, *, n_out, tm, tn, bias=None, gate=None, res=None, out_dtype=F32, seg_out=False, layer=0,
        name="mm"):
    m, k = a.shape
    assert m % tm == 0 and n_out % tn == 0 and b.shape[-2] == k
    b_spec = (pl.BlockSpec((k, tn), lambda j, i: (0, j)) if b.ndim == 2
              else pl.BlockSpec((1, k, tn), lambda j, i: (layer, 0, j)))
    in_specs = [pl.BlockSpec((tm, k), lambda j, i: (i, 0)), b_spec]
    args = [a, b]
    if bias is not None:
        in_specs.append(pl.BlockSpec((1, tn), lambda j, i: (0, j)))
        args.append(bias.reshape(1, n_out))
    if res is not None:
        in_specs.append(pl.BlockSpec((tm // CHUNK, 1, tn), lambda j, i: (i, 0, j)))
        in_specs.append(pl.BlockSpec((tm, tn), lambda j, i: (i, j)))
        args += [gate, res]
    return pl.pallas_call(
        functools.partial(_mm_kernel, has_bias=bias is not None, has_res=res is not None, tm=tm),
        grid=(n_out // tn, m // tm), in_specs=in_specs,
        out_specs=(pl.BlockSpec((1, tm, tn), lambda j, i: (j, i, 0)) if seg_out
                   else pl.BlockSpec((tm, tn), lambda j, i: (i, j))),
        out_shape=jax.ShapeDtypeStruct((n_out // tn, m, tn) if seg_out else (m, n_out), out_dtype),
        scratch_shapes=[pltpu.VMEM((k, tn), MXU)],
        compiler_params=_cparams(("arbitrary", "arbitrary")), name=name,
    )(*args)


def _mm_heads_out(a, b_h, *, tm):
    m = a.shape[0]
    nh, ka, n = b_h.shape
    return pl.pallas_call(
        functools.partial(_mm_kernel, has_bias=False, has_res=False, tm=tm),
        grid=(nh, m // tm),
        in_specs=[pl.BlockSpec((tm, ka), lambda h, i: (i, h)),
                  pl.BlockSpec((1, ka, n), lambda h, i: (h, 0, 0))],
        out_specs=pl.BlockSpec((1, tm, n), lambda h, i: (h, i, 0)),
        out_shape=jax.ShapeDtypeStruct((nh, m, n), MXU),
        scratch_shapes=[pltpu.VMEM((ka, n), MXU)],
        compiler_params=_cparams(("arbitrary", "arbitrary")), name="mm_heads_out",
    )(a, b_h)


def _mm_heads_in(a_h, b_h, *, tm):
    nh, m, ka = a_h.shape
    n = b_h.shape[2]
    return pl.pallas_call(
        functools.partial(_mm_kernel, has_bias=False, has_res=False, tm=tm),
        grid=(nh, m // tm),
        in_specs=[pl.BlockSpec((1, tm, ka), lambda h, i: (h, i, 0)),
                  pl.BlockSpec((1, ka, n), lambda h, i: (h, 0, 0))],
        out_specs=pl.BlockSpec((tm, n), lambda h, i: (i, h)),
        out_shape=jax.ShapeDtypeStruct((m, nh * n), MXU),
        scratch_shapes=[pltpu.VMEM((ka, n), MXU)],
        compiler_params=_cparams(("arbitrary", "arbitrary")), name="mm_heads_in",
    )(a_h, b_h)


def _rms(x, g):
    return x * lax.rsqrt(jnp.mean(x * x, axis=-1, keepdims=True) + EPS) * g


def _modulate(y, sc_ref, sh_ref, tm):
    d = y.shape[-1]
    y3 = y.reshape(tm // CHUNK, CHUNK, d)
    y3 = y3 * (1.0 + sc_ref[...]) + sh_ref[...]
    return y3.reshape(tm, d)


def _norm_kernel(*refs, has_mod, tm):
    if has_mod:
        x_ref, g_ref, sc_ref, sh_ref, o_ref = refs
    else:
        x_ref, g_ref, o_ref = refs
    y = _rms(x_ref[...], g_ref[...])
    if has_mod:
        y = _modulate(y, sc_ref, sh_ref, tm)
    o_ref[...] = y.astype(o_ref.dtype)


def _norm(x, g, *, tm, width=None, col_block=0, sc=None, sh=None, out_dtype=F32):
    width = x.shape[1] if width is None else width
    rows = x.shape[0]
    assert rows % tm == 0
    has_mod = sc is not None
    in_specs = [pl.BlockSpec((tm, width), lambda i: (i, col_block)),
                pl.BlockSpec((1, width), lambda i: (0, 0))]
    args = [x, g.reshape(1, width)]
    if has_mod:
        in_specs += [pl.BlockSpec((tm // CHUNK, 1, width), lambda i: (i, 0, 0))] * 2
        args += [sc, sh]
    return pl.pallas_call(
        functools.partial(_norm_kernel, has_mod=has_mod, tm=tm),
        grid=(rows // tm,), in_specs=in_specs,
        out_specs=pl.BlockSpec((tm, width), lambda i: (i, 0)),
        out_shape=jax.ShapeDtypeStruct((rows, width), out_dtype),
        compiler_params=_cparams(("arbitrary",)), name="norm",
    )(*args)


def _split2(x):
    hi = x.astype(MXU)
    lo = (x - hi.astype(F32)).astype(MXU)
    return hi, lo


def _route(logits, rb_ref):
    tm = logits.shape[0]
    lt = logits.T
    aff = [1.0 / (1.0 + jnp.exp(-lt[e:e + 1, :])) for e in range(N_EXPERTS)]
    sc = [aff[e] + rb_ref[e] for e in range(N_EXPERTS)]
    npg = EXPERTS_PER_GROUP
    gscore = []
    for g in range(N_GROUPS):
        v = sc[g * npg:(g + 1) * npg]
        best = None
        for a in range(npg):
            for b in range(a + 1, npg):
                s = v[a] + v[b]
                best = s if best is None else jnp.maximum(best, s)
        gscore.append(best)
    gb = jnp.zeros((1, tm), jnp.int32)
    gv = gscore[0]
    for g in range(1, N_GROUPS):
        better = gscore[g] > gv
        gb = jnp.where(better, g, gb)
        gv = jnp.where(better, gscore[g], gv)
    u = [sc[j] for j in range(npg)]
    a4 = [aff[j] for j in range(npg)]
    for g in range(1, N_GROUPS):
        pick = gb == g
        u = [jnp.where(pick, sc[g * npg + j], u[j]) for j in range(npg)]
        a4 = [jnp.where(pick, aff[g * npg + j], a4[j]) for j in range(npg)]
    i1 = jnp.zeros((1, tm), jnp.int32)
    v1 = u[0]
    for j in range(1, npg):
        better = u[j] > v1
        i1 = jnp.where(better, j, i1)
        v1 = jnp.where(better, u[j], v1)
    i2 = jnp.full((1, tm), -1, jnp.int32)
    v2 = jnp.full((1, tm), NEG_INF, F32)
    for j in range(npg):
        better = (i1 != j) & ((u[j] > v2) | (i2 < 0))
        i2 = jnp.where(better, j, i2)
        v2 = jnp.where(better, u[j], v2)
    w1 = a4[0]
    w2 = a4[0]
    for j in range(1, npg):
        w1 = jnp.where(i1 == j, a4[j], w1)
        w2 = jnp.where(i2 == j, a4[j], w2)
    tot = w1 + w2
    w1 = w1 / tot
    w2 = w2 / tot
    e1 = gb * npg + i1
    e2 = gb * npg + i2
    rows = jnp.concatenate([e1.astype(F32), e2.astype(F32), w1, w2, jnp.zeros((LANES - 4, tm), F32)], axis=0)
    return rows.T


def _norm_router_kernel(x_ref, g_ref, sc_ref, sh_ref, wr_ref, rb_ref, h_ref, route_ref, *, tm):
    h = _modulate(_rms(x_ref[...], g_ref[...]), sc_ref, sh_ref, tm)
    h_ref[...] = h
    hh, hl = _split2(h)
    wh, wl = _split2(wr_ref[...])
    logits = _dot(hh, wh) + (_dot(hl, wh) + _dot(hh, wl))
    route_ref[...] = _route(logits, rb_ref)


def _norm_router(x, g, sc, sh, w_router_pad, router_bias, *, tm):
    t, d = x.shape
    return pl.pallas_call(
        functools.partial(_norm_router_kernel, tm=tm),
        grid=(t // tm,),
        in_specs=[pl.BlockSpec((tm, d), lambda i: (i, 0)),
                  pl.BlockSpec((1, d), lambda i: (0, 0)),
                  pl.BlockSpec((tm // CHUNK, 1, d), lambda i: (i, 0, 0)),
                  pl.BlockSpec((tm // CHUNK, 1, d), lambda i: (i, 0, 0)),
                  pl.BlockSpec((d, LANES), lambda i: (0, 0)),
                  pl.BlockSpec(memory_space=pltpu.SMEM)],
        out_specs=[pl.BlockSpec((tm, d), lambda i: (i, 0)),
                   pl.BlockSpec((tm, LANES), lambda i: (i, 0))],
        out_shape=[jax.ShapeDtypeStruct((t, d), F32), jax.ShapeDtypeStruct((t, LANES), F32)],
        compiler_params=_cparams(("arbitrary",)), name="norm_router",
    )(x, g.reshape(1, d), sc, sh, w_router_pad, router_bias)


TM_EXPERT = 256
TM_DISPATCH = 256


def _row_copy(src, src_row, dst, dst_row, sem):
    return pltpu.make_async_copy(src.at[pl.ds(src_row, 1), :], dst.at[pl.ds(dst_row, 1), :], sem)


def _rows_wait(src, dst, sem, n):
    pltpu.make_async_copy(src.at[pl.ds(0, n), :], dst.at[pl.ds(0, n), :], sem).wait()


def _dispatch_kernel(dest_ref, pad_ref, h_ref, xs_hbm, sems, *, tb, t, n_pad):
    i = pl.program_id(0)
    sem = sems.at[0]

    def body(r, c):
        tok = i * tb + r
        _row_copy(h_ref, r, xs_hbm, dest_ref[tok], sem).start()
        _row_copy(h_ref, r, xs_hbm, dest_ref[t + tok], sem).start()
        return c

    lax.fori_loop(0, tb, body, 0, unroll=8)
    for _ in range(2):
        _rows_wait(h_ref, xs_hbm, sem, tb)

    @pl.when(i == 0)
    def _():
        for c0 in range(0, n_pad, tb):
            nc = min(tb, n_pad - c0)

            def pad_body(r, c, c0=c0):
                _row_copy(h_ref, 0, xs_hbm, pad_ref[c0 + r], sem).start()
                return c

            lax.fori_loop(0, nc, pad_body, 0, unroll=8)
            _rows_wait(h_ref, xs_hbm, sem, nc)


def _expert_kernel(te_ref, nv_ref, x_ref, wg_ref, wu_ref, wd_ref, y_ref):
    del te_ref
    i = pl.program_id(0)

    @pl.when(i < nv_ref[0])
    def _():
        x = x_ref[...].astype(MXU)
        a = _dot(x, wg_ref[0, 0])
        b = _dot(x, wu_ref[0, 0])
        he = (a / (1.0 + jnp.exp(-a))) * b
        y_ref[...] = _dot(he.astype(MXU), wd_ref[0, 0])

    @pl.when(i >= nv_ref[0])
    def _():
        y_ref[...] = jnp.zeros_like(y_ref)


def _combine_kernel(dest_ref, y_hbm, route_ref, res_ref, g2_ref, o_ref, ybuf, sem, *, tm, t):
    i = pl.program_id(0)
    n = pl.num_programs(0)
    slot = i % 2

    def start(step, s):
        for j in range(2):
            def body(r, c):
                _row_copy(y_hbm, dest_ref[j * t + step * tm + r], ybuf.at[s, j], r, sem.at[s]).start()
                return c

            lax.fori_loop(0, tm, body, 0, unroll=8)

    @pl.when(i == 0)
    def _():
        start(0, 0)

    @pl.when(i + 1 < n)
    def _():
        start(i + 1, 1 - slot)

    for j in range(2):
        _rows_wait(y_hbm, ybuf.at[slot, j], sem.at[slot], tm)
    d = o_ref.shape[-1]
    route = route_ref[...]
    y = route[:, 2:3] * ybuf[slot, 0] + route[:, 3:4] * ybuf[slot, 1]
    o_ref[...] = res_ref[...] + (y.reshape(tm // CHUNK, CHUNK, d) * g2_ref[...]).reshape(tm, d)


def _cumsum_rows(x, blk):
    n, e = x.shape
    xb = x.reshape(n // blk, blk, e)
    tri = jnp.asarray(np.tril(np.ones((blk, blk), np.float32)))
    local = jnp.einsum('ij,bje->bie', tri, xb, precision=lax.Precision.HIGHEST)
    tot = local[:, -1, :]
    return (local + (jnp.cumsum(tot, axis=0) - tot)[:, None, :]).reshape(n, e)


def _moe(h, route, w_gate, w_up, w_down, layer, g2, res, *, tm):
    t, d = h.shape
    de = w_gate.shape[-1]
    te = _tile(2 * t, TM_EXPERT, CHUNK)
    n_tiles = 2 * t // te + N_EXPERTS
    r_pad = n_tiles * te
    n_pad = r_pad - 2 * t
    e_flat = route[:, 0:2].T.reshape(-1)
    experts = jnp.arange(N_EXPERTS, dtype=F32)
    onehot = (e_flat[:, None] == experts[None, :]).astype(F32)
    csum = _cumsum_rows(onehot, _tile(2 * t, 256, 8))
    counts = csum[-1]
    tiles_e = jnp.ceil(counts / te)
    tile_end = jnp.cumsum(tiles_e)
    seg_start = (tile_end - tiles_e) * te
    dest = jnp.sum(onehot * (seg_start[None, :] + csum - 1.0), axis=1).astype(jnp.int32)
    gap_start = jnp.concatenate([seg_start + counts, tile_end[-1:] * te])
    gap_len = jnp.concatenate([tiles_e * te - counts, r_pad - tile_end[-1:] * te])
    gap_first = jnp.cumsum(gap_len) - gap_len
    k = jnp.arange(n_pad, dtype=F32)[:, None]
    in_gap = (k >= gap_first[None, :]) & (k < (gap_first + gap_len)[None, :])
    pad_rows = jnp.sum(jnp.where(in_gap, gap_start[None, :] + k - gap_first[None, :], 0.0),
                       axis=1).astype(jnp.int32)
    tile_id = jnp.arange(n_tiles, dtype=F32)[:, None]
    tile_e = jnp.minimum(jnp.sum((tile_end[None, :] <= tile_id).astype(jnp.int32), axis=1),
                         N_EXPERTS - 1)
    n_valid = tile_end[-1:].astype(jnp.int32)

    tb = _tile(t, TM_DISPATCH, CHUNK)
    xs = pl.pallas_call(
        functools.partial(_dispatch_kernel, tb=tb, t=t, n_pad=n_pad),
        grid_spec=pltpu.PrefetchScalarGridSpec(
            num_scalar_prefetch=2, grid=(t // tb,),
            in_specs=[pl.BlockSpec((tb, d), lambda i, dst, pad: (i, 0))],
            out_specs=pl.BlockSpec(memory_space=pl.ANY),
            scratch_shapes=[pltpu.SemaphoreType.DMA((1,))]),
        out_shape=jax.ShapeDtypeStruct((r_pad, d), F32),
        compiler_params=_cparams(("arbitrary",)), name="moe_dispatch",
    )(dest, pad_rows, h)

    y = pl.pallas_call(
        _expert_kernel,
        grid_spec=pltpu.PrefetchScalarGridSpec(
            num_scalar_prefetch=2, grid=(n_tiles,),
            in_specs=[pl.BlockSpec((te, d), lambda i, e, n: (i, 0)),
                      pl.BlockSpec((1, 1, d, de), lambda i, e, n: (layer, e[i], 0, 0)),
                      pl.BlockSpec((1, 1, d, de), lambda i, e, n: (layer, e[i], 0, 0)),
                      pl.BlockSpec((1, 1, de, d), lambda i, e, n: (layer, e[i], 0, 0))],
            out_specs=pl.BlockSpec((te, d), lambda i, e, n: (i, 0))),
        out_shape=jax.ShapeDtypeStruct((r_pad, d), F32),
        compiler_params=_cparams(("arbitrary",)), name="moe_experts",
    )(tile_e, n_valid, xs, w_gate, w_up, w_down)

    return pl.pallas_call(
        functools.partial(_combine_kernel, tm=tm, t=t),
        grid_spec=pltpu.PrefetchScalarGridSpec(
            num_scalar_prefetch=1, grid=(t // tm,),
            in_specs=[pl.BlockSpec(memory_space=pl.ANY),
                      pl.BlockSpec((tm, LANES), lambda i, dst: (i, 0)),
                      pl.BlockSpec((tm, d), lambda i, dst: (i, 0)),
                      pl.BlockSpec((tm // CHUNK, 1, d), lambda i, dst: (i, 0, 0))],
            out_specs=pl.BlockSpec((tm, d), lambda i, dst: (i, 0)),
            scratch_shapes=[pltpu.VMEM((2, 2, tm, d), F32), pltpu.SemaphoreType.DMA((2,))]),
        out_shape=jax.ShapeDtypeStruct((t, d), F32),
        compiler_params=_cparams(("arbitrary",)), name="moe_combine",
    )(dest, y, route, res, g2)


def _rel_bucket(rel):
    half = N_BUCKETS // 2
    max_exact = half // 2
    n = jnp.abs(rel)
    nf = jnp.maximum(n, 1).astype(F32)
    large = max_exact + (jnp.log(nf / max_exact) / math.log(MAX_DISTANCE / max_exact)
                         * (half - max_exact)).astype(jnp.int32)
    large = jnp.minimum(large, half - 1)
    return jnp.where(rel > 0, half, 0) + jnp.where(n < max_exact, n, large)


def _bias_tiles(table, q_offsets, k_offset, tq, tk, n_heads):
    tiles = []
    for q0 in q_offsets:
        rel = (k_offset + np.arange(tk))[None, :] - (q0 + np.arange(tq))[:, None]
        bucket = _rel_bucket(jnp.asarray(rel, jnp.int32))
        onehot = (bucket[..., None] == jnp.arange(N_BUCKETS, dtype=jnp.int32)).astype(F32)
        tiles.append(jnp.einsum('qkb,bh->hqk', onehot, table[:, :n_heads].astype(F32),
                                precision=lax.Precision.HIGHEST))
    return jnp.stack(tiles)


def _bias_far(table, n_heads):
    rel = jnp.full((1,), -(MAX_DISTANCE + 1), jnp.int32)
    return table[_rel_bucket(rel)][0, :n_heads].astype(F32)


def _diff_attend(qr, far, prev, diag, bias_far, bp_ref, bd_ref, vis):
    outs = []
    for r in range(2):
        cols = slice(r * DA_QK, (r + 1) * DA_QK)
        parts = []
        if far is not None:
            s = _dot_nt(qr[r], far[0][:, cols].astype(MXU)) + bias_far[r]
            parts.append((s, far[1]))
        if prev is not None:
            s = _dot_nt(qr[r], prev[0][:, cols].astype(MXU)) + bp_ref[0, r]
            parts.append((s, prev[1]))
        s = _dot_nt(qr[r], diag[0][:, cols].astype(MXU)) + bd_ref[0, r]
        parts.append((jnp.where(vis, s, NEG_INF), diag[1]))
        m = None
        for s, _ in parts:
            mx = jnp.max(s, axis=-1, keepdims=True)
            m = mx if m is None else jnp.maximum(m, mx)
        l = None
        acc = None
        for s, v in parts:
            p = jnp.exp(s - m)
            ps = jnp.sum(p, axis=-1, keepdims=True)
            pv = _dot(p.astype(MXU), v.astype(MXU))
            l = ps if l is None else l + ps
            acc = pv if acc is None else acc + pv
        outs.append(acc / l)
    return outs


def _stick_attend(qb, blocks, tri_ref, trid_ref, carry=None):
    acc = None
    for kb, vb, mask in blocks:
        tk = kb.shape[0]
        z = _dot_nt(qb, kb.astype(MXU)) * HEAD_DIM ** -0.5
        log_beta = jnp.minimum(z, 0.0) - jnp.log(1.0 + jnp.exp(-jnp.abs(z)))
        log_keep = log_beta - z
        if mask is not None:
            log_keep = jnp.where(mask, log_keep, 0.0)
        tri = (trid_ref if tk != TK else tri_ref)[...]
        hi, lo = _split2(log_keep)
        later = _dot(hi, tri) + _dot(lo, tri)
        total = jnp.broadcast_to(later[:, 0:1] + log_keep[:, 0:1], (later.shape[0], LANES))
        if carry is not None:
            later = later + (carry[:, :tk] if tk <= LANES else
                             jnp.concatenate([carry] * (tk // LANES), axis=1))
        a = jnp.exp(log_beta + later)
        if mask is not None:
            a = jnp.where(mask, a, 0.0)
        pv = _dot(a.astype(MXU), vb.astype(MXU))
        acc = pv if acc is None else acc + pv
        carry = total if carry is None else carry + total
    return acc, carry


def _even_attn_kernel(far_ref, lam_ref, q_ref, kn_ref, vn_ref, bd_ref, bp_ref, g_ref, tri_ref, o_ref,
                      *, tq, nq, lam_init):
    u = pl.program_id(1)
    i = pl.program_id(2)
    q = q_ref[0]
    qq = lax.broadcasted_iota(jnp.int32, (tq, TK), 0)
    kk = lax.broadcasted_iota(jnp.int32, (tq, TK), 1)

    def regions(c):
        far = (0, (c - 1) * TK) if c >= 2 else None
        prev = ((c - 1) * TK, c * TK) if c >= 1 else None
        return far, prev, (c * TK, (c + 1) * TK)

    def load(reg, lo=None, hi=None):
        a, b = reg
        lo, hi = (a, b) if lo is None else (a + lo, a + hi)
        return kn_ref[0, lo:hi, :], vn_ref[0, lo:hi, :]

    def diff(c):
        far, prev, diag = regions(c)
        qr = [(q[:, r * DA_QK:(r + 1) * DA_QK] * DA_QK ** -0.5).astype(MXU) for r in range(2)]
        vis = (kk // CHUNK) <= (qq // CHUNK)
        o0, o1 = _diff_attend(qr, None if far is None else load(far), None if prev is None else load(prev),
                              load(diag), [far_ref[2 * u], far_ref[2 * u + 1]], bp_ref, bd_ref, vis)
        o = _rms(o0 - lam_ref[0] * o1, g_ref[...]) * (1.0 - lam_init)
        o_ref[...] = o.astype(o_ref.dtype)

    def stick(c):
        nb = c + 1
        k, v = load((0, nb * TK))
        z = _dot_nt(q.astype(MXU), k.astype(MXU)) * HEAD_DIM ** -0.5
        log_beta = jnp.minimum(z, 0.0) - jnp.log(1.0 + jnp.exp(-jnp.abs(z)))
        log_keep = log_beta - z
        earlier = kk < qq
        tri = tri_ref[...]
        later, total = [], []
        for j in range(nb):
            lk = log_keep[:, j * TK:(j + 1) * TK]
            if j == nb - 1:
                lk = jnp.where(earlier, lk, 0.0)
            hi, lo = _split2(lk)
            lt = _dot(hi, tri) + _dot(lo, tri)
            later.append(lt)
            total.append(jnp.broadcast_to(lt[:, 0:1] + lk[:, 0:1], (tq, LANES)))
        carry = None
        for j in range(nb - 1, -1, -1):
            if carry is not None:
                later[j] = later[j] + jnp.concatenate([carry] * (TK // LANES), axis=1)
            carry = total[j] if carry is None else carry + total[j]
        a = [jnp.exp(log_beta[:, j * TK:(j + 1) * TK] + later[j]) for j in range(nb)]
        a[-1] = jnp.where(earlier, a[-1], 0.0)
        a = a[0] if nb == 1 else jnp.concatenate(a, axis=1)
        o_ref[...] = _dot(a.astype(MXU), v.astype(MXU)).astype(o_ref.dtype)

    for c in range(nq):
        pl.when((u < N_HEADS_A) & (i == c))(functools.partial(diff, c))
        pl.when((u >= N_HEADS_A) & (i == c))(functools.partial(stick, c))


def _tri_later(tk):
    return jnp.asarray(np.arange(tk)[:, None] > np.arange(tk)[None, :], MXU)


def _even_attn(qkv, table, lam, subln_g, lam_init, *, bsz, n):
    tq = TK
    assert n % tq == 0
    nq = n // tq
    far = _bias_far(table, N_MAPS_A)
    bd = _bias_tiles(table, [0], 0, tq, TK, N_MAPS_A)
    bp = _bias_tiles(table, [0], -TK, tq, TK, N_MAPS_A)

    def seg(u, base):
        return jnp.where(u < 8, base, base + 3)

    smem = pl.BlockSpec(memory_space=pltpu.SMEM)
    in_specs = [smem, smem,
                pl.BlockSpec((1, tq, HEAD_DIM), lambda b, u, i: (seg(u, 0), b * nq + i, u % 8)),
                pl.BlockSpec((1, n, HEAD_DIM), lambda b, u, i: (seg(u, 1), b, u % 8)),
                pl.BlockSpec((1, n, HEAD_DIM), lambda b, u, i: (seg(u, 2), b, u % 8)),
                pl.BlockSpec((1, 2, tq, TK), lambda b, u, i: (0, jnp.minimum(u, 7), 0, 0)),
                pl.BlockSpec((1, 2, tq, TK), lambda b, u, i: (0, jnp.minimum(u, 7), 0, 0)),
                pl.BlockSpec((1, HEAD_DIM), lambda b, u, i: (0, 0)),
                pl.BlockSpec((TK, TK), lambda b, u, i: (0, 0))]
    return pl.pallas_call(
        functools.partial(_even_attn_kernel, tq=tq, nq=nq, lam_init=lam_init),
        grid=(bsz, 16, nq), in_specs=in_specs,
        out_specs=pl.BlockSpec((tq, HEAD_DIM), lambda b, u, i: (b * nq + i, u)),
        out_shape=jax.ShapeDtypeStruct((bsz * n, 16 * HEAD_DIM), MXU),
        compiler_params=_cparams(("arbitrary", "arbitrary", "arbitrary")), name="even_attn",
    )(far, lam.reshape(1), qkv, qkv, qkv, bd, bp, subln_g.reshape(1, HEAD_DIM), _tri_later(TK))


def _even_attn_past_kernel(lam_ref, q_ref, kn_ref, vn_ref, ak_hbm, av_hbm, bk_hbm, bv_hbm, bd_ref, bc_ref,
                           g_ref, tri_ref, trid_ref, o_ref, m_ref, l_ref, acc_ref, carry_ref,
                           ka_buf, kb_buf, v_buf, sem, *, n, tkc, nkc, lam_init):
    b = pl.program_id(0)
    mix = pl.program_id(1)
    kc = pl.program_id(2)
    step = (b * 2 + mix) * nkc + kc
    n_steps = pl.num_programs(0) * 2 * nkc
    slot = step % 2
    qq = lax.broadcasted_iota(jnp.int32, (n, n), 0)
    kk = lax.broadcasted_iota(jnp.int32, (n, n), 1)

    def chunk_copies(bb, mm_is_stick, cc, s):
        if mm_is_stick:
            rows = pl.ds(pl.multiple_of((nkc - 1 - cc) * tkc, tkc), tkc)
            return ([pltpu.make_async_copy(bk_hbm.at[bb, rows, h, :], kb_buf.at[s, h], sem.at[s])
                     for h in range(N_HEADS_B)] +
                    [pltpu.make_async_copy(bv_hbm.at[bb, rows, h, :], v_buf.at[s, h], sem.at[s])
                     for h in range(N_HEADS_B)])
        rows = pl.ds(pl.multiple_of(cc * tkc, tkc), tkc)
        return ([pltpu.make_async_copy(ak_hbm.at[bb, rows, pl.ds(h * HEAD_DIM, HEAD_DIM)], ka_buf.at[s, h],
                                       sem.at[s]) for h in range(N_HEADS_A)] +
                [pltpu.make_async_copy(av_hbm.at[bb, rows, h, :], v_buf.at[s, h], sem.at[s])
                 for h in range(N_HEADS_A)])

    def start(bb, mm, cc, s):
        for stick in (False, True):
            @pl.when(mm == int(stick))
            def _(stick=stick):
                for cp in chunk_copies(bb, stick, cc, s):
                    cp.start()

    @pl.when(step == 0)
    def _():
        start(b, mix, kc, slot)

    @pl.when(step + 1 < n_steps)
    def _():
        nxt = step + 1
        start(nxt // (2 * nkc), (nxt // nkc) % 2, nxt % nkc, 1 - slot)

    for stick in (False, True):
        @pl.when(mix == int(stick))
        def _(stick=stick):
            for cp in chunk_copies(b, stick, kc, slot):
                cp.wait()

    def head(ref, h, width=HEAD_DIM, off=0):
        return ref[0, :, h * HEAD_DIM + off:h * HEAD_DIM + off + width]

    def diff_update(s, vs, first):
        mx = jnp.max(s, axis=-1, keepdims=True)
        m_new = mx if first else jnp.maximum(m_ref[...], mx)
        p = jnp.exp(s - m_new)
        ps = jnp.sum(p, axis=-1, keepdims=True)
        pv = jnp.stack([_dot(p[mp].astype(MXU), vs[mp // 2].astype(MXU)) for mp in range(N_MAPS_A)])
        if first:
            l_ref[...] = ps
            acc_ref[...] = pv
        else:
            alpha = jnp.exp(m_ref[...] - m_new)
            l_ref[...] = alpha * l_ref[...] + ps
            acc_ref[...] = alpha * acc_ref[...] + pv
        m_ref[...] = m_new

    def diff_q(mp):
        h, r = divmod(mp, 2)
        return (head(q_ref, h, DA_QK, r * DA_QK) * DA_QK ** -0.5).astype(MXU)

    @pl.when((mix == 0) & (kc == 0))
    def _():
        vis = (kk // CHUNK) <= (qq // CHUNK)
        s = jnp.stack([_dot_nt(diff_q(mp), head(kn_ref, mp // 2, DA_QK, (mp % 2) * DA_QK).astype(MXU))
                       for mp in range(N_MAPS_A)]) + bd_ref[0]
        diff_update(jnp.where(vis[None], s, NEG_INF), [head(vn_ref, h) for h in range(N_HEADS_A)], True)

    @pl.when(mix == 0)
    def _():
        kh = [ka_buf[slot, h].astype(MXU) for h in range(N_HEADS_A)]
        s = jnp.stack([_dot_nt(diff_q(mp), kh[mp // 2][:, (mp % 2) * DA_QK:(mp % 2 + 1) * DA_QK])
                       for mp in range(N_MAPS_A)]) + bc_ref[0]
        diff_update(s, [v_buf[slot, h] for h in range(N_HEADS_A)], False)

    @pl.when((mix == 0) & (kc == nkc - 1))
    def _():
        for h in range(N_HEADS_A):
            o = acc_ref[2 * h] / l_ref[2 * h] - lam_ref[0] * (acc_ref[2 * h + 1] / l_ref[2 * h + 1])
            o = _rms(o, g_ref[...]) * (1.0 - lam_init)
            o_ref[:, h * HEAD_DIM:(h + 1) * HEAD_DIM] = o.astype(o_ref.dtype)

    def stick_q(h):
        return head(q_ref, h).astype(MXU)

    @pl.when((mix == 1) & (kc == 0))
    def _():
        res = [_stick_attend(stick_q(h), [(head(kn_ref, h), head(vn_ref, h), kk < qq)], tri_ref, trid_ref)
               for h in range(N_HEADS_B)]
        acc_ref[:N_HEADS_B] = jnp.stack([r[0] for r in res])
        carry_ref[...] = jnp.stack([r[1] for r in res])

    @pl.when(mix == 1)
    def _():
        carry = carry_ref[...]
        res = []
        for h in range(N_HEADS_B):
            blocks = [(kb_buf[slot, h, j * TK:(j + 1) * TK, :], v_buf[slot, h, j * TK:(j + 1) * TK, :], None)
                      for j in range(tkc // TK - 1, -1, -1)]
            res.append(_stick_attend(stick_q(h), blocks, tri_ref, trid_ref, carry[h]))
        acc_ref[:N_HEADS_B] = acc_ref[:N_HEADS_B] + jnp.stack([r[0] for r in res])
        carry_ref[...] = jnp.stack([r[1] for r in res])

    @pl.when((mix == 1) & (kc == nkc - 1))
    def _():
        for h in range(N_HEADS_B):
            o_ref[:, h * HEAD_DIM:(h + 1) * HEAD_DIM] = acc_ref[h].astype(o_ref.dtype)


def _even_attn_past(qkv, past, table, lam, subln_g, lam_init, *, bsz, n):
    ak, av, bk, bv = past
    p_len = ak.shape[1]
    ak = ak.reshape(bsz, p_len, N_MAPS_A * DA_QK)
    tkc = _tile(p_len, TKC, TK)
    nkc = p_len // tkc
    assert n <= CHUNK and n % 8 == 0
    bd = _bias_tiles(table, [0], 0, n, n, N_MAPS_A)
    far = jnp.broadcast_to(_bias_far(table, N_MAPS_A)[:, None, None], (N_MAPS_A, n, tkc))
    bc = jnp.stack([far, _bias_tiles(table, [0], -tkc, n, tkc, N_MAPS_A)[0]])

    def row(s):
        return lambda b, mix, kc: (3 * mix + s, b, 0)

    width = N_HEADS_A * HEAD_DIM
    in_specs = [pl.BlockSpec(memory_space=pltpu.SMEM),
                pl.BlockSpec((1, n, width), row(0)),
                pl.BlockSpec((1, n, width), row(1)),
                pl.BlockSpec((1, n, width), row(2)),
                pl.BlockSpec(memory_space=pl.ANY), pl.BlockSpec(memory_space=pl.ANY),
                pl.BlockSpec(memory_space=pl.ANY), pl.BlockSpec(memory_space=pl.ANY),
                pl.BlockSpec((1, N_MAPS_A, n, n), lambda b, mix, kc: (0, 0, 0, 0)),
                pl.BlockSpec((1, N_MAPS_A, n, tkc), lambda b, mix, kc: ((kc + 1) // nkc, 0, 0, 0)),
                pl.BlockSpec((1, HEAD_DIM), lambda b, mix, kc: (0, 0)),
                pl.BlockSpec((TK, TK), lambda b, mix, kc: (0, 0)),
                pl.BlockSpec((n, n), lambda b, mix, kc: (0, 0))]
    return pl.pallas_call(
        functools.partial(_even_attn_past_kernel, n=n, tkc=tkc, nkc=nkc, lam_init=lam_init),
        grid=(bsz, 2, nkc), in_specs=in_specs,
        out_specs=pl.BlockSpec((n, width), lambda b, mix, kc: (b, mix)),
        out_shape=jax.ShapeDtypeStruct((bsz * n, 2 * width), MXU),
        scratch_shapes=[pltpu.VMEM((N_MAPS_A, n, 1), F32), pltpu.VMEM((N_MAPS_A, n, 1), F32),
                        pltpu.VMEM((N_MAPS_A, n, HEAD_DIM), F32), pltpu.VMEM((N_HEADS_B, n, LANES), F32),
                        pltpu.VMEM((2, N_HEADS_A, tkc, HEAD_DIM), F32), pltpu.VMEM((2, N_HEADS_B, tkc, HEAD_DIM), F32),
                        pltpu.VMEM((2, N_HEADS_A, tkc, HEAD_DIM), F32), pltpu.SemaphoreType.DMA((2,))],
        compiler_params=_cparams(("arbitrary", "arbitrary", "arbitrary")), name="even_attn_past",
    )(lam.reshape(1), qkv, qkv, qkv, ak, av, bk, bv, bd, bc, subln_g.reshape(1, HEAD_DIM),
      _tri_later(TK), _tri_later(n))


INT_MIN = int(np.iinfo(np.int32).min)
NEG_INF_KEY = int(np.array(-np.inf, np.float32).view(np.int32)) ^ 0x7FFFFFFF
INDEX_BITS = 15


def _sort_key(x):
    b = lax.bitcast_convert_type(x, jnp.int32)
    return b ^ ((b >> 31) & 0x7FFFFFFF)


def _indexer_kernel(*refs, tq, n, p_len, top_k):
    has_past = p_len > 0
    it = iter(refs)
    qi_ref, wi_ref, kin_ref = next(it), next(it), next(it)
    kip_ref = next(it) if has_past else None
    seln_ref = next(it)
    selp_ref = next(it) if has_past else None
    keyn_ref = next(it)
    keyp_ref = next(it) if has_past else None

    i = pl.program_id(1)
    qi = qi_ref[...]
    wi = wi_ref[...] * (N_IDX_HEADS ** -0.5 * D_IDX ** -0.5)

    def scores(kmat):
        kb = kmat.astype(MXU)
        acc = None
        for ih in range(N_IDX_HEADS):
            d = _dot_nt(qi[:, ih * D_IDX:(ih + 1) * D_IDX], kb)
            term = wi[:, ih:ih + 1] * jnp.maximum(d, 0.0)
            acc = term if acc is None else acc + term
        return acc

    def run(w_new):
        s_new = scores(kin_ref[:w_new, :])
        qq = i * tq + lax.broadcasted_iota(jnp.int32, (tq, w_new), 0)
        kk = lax.broadcasted_iota(jnp.int32, (tq, w_new), 1)
        s_new = jnp.where((kk // CHUNK) <= (qq // CHUNK), s_new, NEG_INF)
        keyn_ref[:, :w_new] = _sort_key(s_new)
        parts = [(keyn_ref, p_len, w_new, seln_ref)]
        if has_past:
            keyp_ref[...] = _sort_key(scores(kip_ref[0]))
            parts.append((keyp_ref, 0, p_len, selp_ref))

        def count(pred):
            tot = None
            for ref, base, width, _ in parts:
                idx = base + lax.broadcasted_iota(jnp.int32, (tq, width), 1)
                c = jnp.sum(jnp.where(pred(ref[:, :width], idx), 1.0, 0.0), axis=-1, keepdims=True)
                tot = c if tot is None else tot + c
            return tot

        def body(b, t):
            cand = t + jnp.left_shift(jnp.int32(1), 31 - b)
            c = count(lambda k, idx: k >= cand)
            return jnp.where(c >= top_k, cand, t)

        t = lax.fori_loop(0, 32, body, jnp.full((tq, 1), INT_MIN, jnp.int32))
        n_gt = count(lambda k, idx: k > t)
        n_eq = count(lambda k, idx: k == t)
        need = top_k - n_gt
        tie = jnp.where((n_eq != need) & (t > NEG_INF_KEY), 1.0, 0.0)

        def write(sel_of):
            for ref, base, width, out_ref in parts:
                k = ref[:, :width]
                idx = base + lax.broadcasted_iota(jnp.int32, (tq, width), 1)
                out_ref[:, :width] = jnp.where(sel_of(k, idx) & (k > NEG_INF_KEY), 1.0, 0.0)

        write(lambda k, idx: k >= t)
        if w_new < n:
            seln_ref[:, w_new:] = jnp.zeros((tq, n - w_new), F32)

        @pl.when(jnp.max(tie) > 0.0)
        def _():
            def jbody(b, jv):
                cand = jv + jnp.left_shift(jnp.int32(1), INDEX_BITS - 1 - b)
                c = count(lambda k, idx: (k == t) & (idx < cand))
                return jnp.where(c < need, cand, jv)

            jv = lax.fori_loop(0, INDEX_BITS, jbody, jnp.zeros((tq, 1), jnp.int32))
            write(lambda k, idx: (k > t) | ((k == t) & (idx <= jv)))

    if has_past or n <= TK:
        run(n)
    else:
        for c in range(n // TK):
            pl.when((i * tq) // TK == c)(functools.partial(run, (c + 1) * TK))


def _indexer(q_i, proj, k_idx_past, *, bsz, n, top_k):
    has_past = k_idx_past is not None
    tq = min(n, 128)
    nq = n // tq
    p_len = k_idx_past.shape[1] if has_past else 0
    assert n + p_len < 2 ** (INDEX_BITS - 1) and (n <= TK or n % TK == 0)
    t = bsz * n
    in_specs = [pl.BlockSpec((tq, N_IDX_HEADS * D_IDX), lambda b, i: (b * nq + i, 0)),
                pl.BlockSpec((tq, LANES), lambda b, i: (b * nq + i, 9)),
                pl.BlockSpec((n, D_IDX), lambda b, i: (b, 8))]
    args = [q_i, proj, proj]
    out_specs = [pl.BlockSpec((tq, n), lambda b, i: (b * nq + i, 0))]
    out_shape = [jax.ShapeDtypeStruct((t, n), F32)]
    scratch = [pltpu.VMEM((tq, n), jnp.int32)]
    if has_past:
        in_specs.append(pl.BlockSpec((1, p_len, D_IDX), lambda b, i: (b, 0, 0)))
        args.append(k_idx_past)
        out_specs.append(pl.BlockSpec((tq, p_len), lambda b, i: (b * nq + i, 0)))
        out_shape.append(jax.ShapeDtypeStruct((t, p_len), F32))
        scratch.append(pltpu.VMEM((tq, p_len), jnp.int32))
    res = pl.pallas_call(
        functools.partial(_indexer_kernel, tq=tq, n=n, p_len=p_len, top_k=top_k),
        grid=(bsz, nq), in_specs=in_specs, out_specs=out_specs, out_shape=out_shape,
        scratch_shapes=scratch,
        compiler_params=_cparams(("arbitrary", "arbitrary")),
        name="indexer_past" if has_past else "indexer",
    )(*args)
    return res[0], (res[1] if has_past else None)


def _sparse_attn_kernel(*refs, tq, tkd, n, p_len):
    has_past = p_len > 0
    it = iter(refs)
    far_ref, q_ref, kvn_ref, seln_ref = (next(it) for _ in range(4))
    kvp_ref, selp_ref = (next(it), next(it)) if has_past else (None, None)
    bd_ref, bp_ref, o_ref = (next(it) for _ in range(3))
    nh = N_HEADS_C
    i = pl.program_id(1)
    q = q_ref[...].reshape(nh * tq, KV_LORA)

    def attend(regions):
        parts = []
        for kvb, sel, bias in regions:
            kvb = kvb.astype(MXU)
            w = kvb.shape[0]
            s = _dot_nt(q, kvb).reshape(nh, tq, w) * HEAD_DIM ** -0.5 + bias
            parts.append((jnp.where(sel[None] > 0.0, s, NEG_INF), kvb))
        m = None
        for s, _ in parts:
            mx = jnp.max(s, axis=-1, keepdims=True)
            m = mx if m is None else jnp.maximum(m, mx)
        l = None
        acc = None
        for s, kvb in parts:
            w = kvb.shape[0]
            p = jnp.exp(s - m)
            ps = jnp.sum(p, axis=-1, keepdims=True)
            pv = _dot(p.reshape(nh * tq, w).astype(MXU), kvb)
            l = ps if l is None else l + ps
            acc = pv if acc is None else acc + pv
        o_ref[...] = (acc.reshape(nh, tq, KV_LORA) / l).astype(o_ref.dtype)

    if has_past:
        regions = []
        if p_len > TK:
            regions.append((kvp_ref[0, :p_len - TK, :], selp_ref[:, :p_len - TK], far_ref[...]))
        regions.append((kvp_ref[0, p_len - TK:, :], selp_ref[:, p_len - TK:], bp_ref[0]))
        regions.append((kvn_ref[...], seln_ref[...], bd_ref[0]))
        attend(regions)
    else:
        def run(c):
            regions = []
            if c >= 2:
                regions.append((kvn_ref[:(c - 1) * TK, :], seln_ref[:, :(c - 1) * TK], far_ref[...]))
            if c >= 1:
                regions.append((kvn_ref[(c - 1) * TK:c * TK, :], seln_ref[:, (c - 1) * TK:c * TK], bp_ref[0]))
            regions.append((kvn_ref[c * TK:(c + 1) * TK, :], seln_ref[:, c * TK:(c + 1) * TK], bd_ref[0]))
            attend(regions)

        for c in range(n // TK):
            pl.when((i * tq) // TK == c)(functools.partial(run, c))


def _sparse_attn(q_lat, kv_lat, sel_new, kv_past, sel_past, table, *, bsz, n):
    has_past = kv_past is not None
    nh = N_HEADS_C
    t = bsz * n
    if has_past:
        tq, tkd = min(n, TQ_SPARSE_PAST), n
        p_len = kv_past.shape[1]
        assert p_len % TK == 0 and n <= TK
    else:
        tq, tkd = min(n, TQ_SPARSE), TK
        p_len = 0
        assert n % TK == 0
    q_offsets = list(range(0, tkd, tq))
    nq = n // tq
    npar = len(q_offsets)
    far = _bias_far(table, nh).reshape(nh, 1, 1)
    bd = _bias_tiles(table, q_offsets, 0, tq, tkd, nh)
    bp = _bias_tiles(table, q_offsets, -TK, tq, TK, nh)
    in_specs = [pl.BlockSpec((nh, 1, 1), lambda b, i: (0, 0, 0)),
                pl.BlockSpec((nh, tq, KV_LORA), lambda b, i: (0, b * nq + i, 0)),
                pl.BlockSpec((n, KV_LORA), lambda b, i: (b, 0)),
                pl.BlockSpec((tq, n), lambda b, i: (b * nq + i, 0))]
    args = [far, q_lat, kv_lat, sel_new]
    if has_past:
        in_specs += [pl.BlockSpec((1, p_len, KV_LORA), lambda b, i: (b, 0, 0)),
                     pl.BlockSpec((tq, p_len), lambda b, i: (b * nq + i, 0))]
        args += [kv_past, sel_past]
    in_specs += [pl.BlockSpec((1, nh, tq, tkd), lambda b, i: (i % npar, 0, 0, 0)),
                 pl.BlockSpec((1, nh, tq, TK), lambda b, i: (i % npar, 0, 0, 0))]
    args += [bd, bp]
    return pl.pallas_call(
        functools.partial(_sparse_attn_kernel, tq=tq, tkd=tkd, n=n, p_len=p_len),
        grid=(bsz, nq), in_specs=in_specs,
        out_specs=pl.BlockSpec((nh, tq, KV_LORA), lambda b, i: (0, b * nq + i, 0)),
        out_shape=jax.ShapeDtypeStruct((nh, t, KV_LORA), MXU),
        compiler_params=_cparams(("arbitrary", "arbitrary")),
        name="sparse_attn_past" if has_past else "sparse_attn",
    )(*args)


def _tile(n, cap, mult):
    best = None
    for t in range(mult, min(n, cap) + 1, mult):
        if n % t == 0:
            best = t
    assert best is not None, (n, cap, mult)
    return best


def _per_chunk(v, reps):
    return jnp.repeat(v, reps, axis=0)[:, None, :]


def _trunk(x, mod, past, p, w):
    bsz, n, d = x.shape
    t = bsz * n
    x = x.reshape(t, d)
    tm = _tile(t, 512, CHUNK)
    tmn = _tile(t, 256, CHUNK)
    tn_d = _tile(d, 1024, LANES)
    depth = p['w_ada'].shape[0]
    p_len = 0 if past is None else past[0].shape[2]
    top_k = min(TOPK_MAX, (p_len + n) // 4)
    even_rows, odd_rows = [], []
    for l in range(depth):
        m6 = mod[l].reshape(bsz, 6, d)
        sh1, sc1, g1, sh2, sc2, g2 = (_per_chunk(m6[:, k], n // CHUNK) for k in range(6))
        h = _norm(x, p['norm_mix_g'][l], tm=tmn, sc=sc1, sh=sh1, out_dtype=MXU)
        if l % 2 == 0:
            i = l // 2
            lam_init = 0.8 - 0.6 * math.exp(-0.3 * l)
            qkv = _mm(h, p['w_in_even'][i], n_out=48 * LANES, tm=tm, tn=1024, seg_out=True, name="mm_qkv")
            past_i = None if past is None else tuple(a[i] for a in past[:4])
            lam = (jnp.exp(jnp.sum(p['lam_q1'][i].astype(F32) * p['lam_k1'][i].astype(F32)))
                   - jnp.exp(jnp.sum(p['lam_q2'][i].astype(F32) * p['lam_k2'][i].astype(F32)))
                   + lam_init)
            if past is None:
                o = _even_attn(qkv, p['rel_bias_table'], lam, p['subln_g'][i], lam_init, bsz=bsz, n=n)
            else:
                o = _even_attn_past(qkv, past_i, p['rel_bias_table'], lam, p['subln_g'][i], lam_init,
                                    bsz=bsz, n=n)
            even_rows.append((qkv[1].reshape(bsz, n, N_MAPS_A, DA_QK),
                              qkv[2].reshape(bsz, n, N_HEADS_A, HEAD_DIM),
                              qkv[4].reshape(bsz, n, N_HEADS_B, HEAD_DIM),
                              qkv[5].reshape(bsz, n, N_HEADS_B, HEAD_DIM)))
            x = _mm(o, p['w_out_even'][i], n_out=d, tm=tm, tn=tn_d, gate=g1, res=x)
        else:
            j = l // 2
            proj = _mm(h, w['w_in_odd'][j], n_out=10 * LANES, tm=tm, tn=10 * LANES)
            c_q = _norm(proj, p['g_q'][j], tm=tmn, width=Q_LORA, col_block=0, out_dtype=MXU)
            kv_lat = _norm(proj, p['g_kv'][j], tm=tmn, width=KV_LORA, col_block=1)
            k_i = proj[:, 1024:1152]
            q = _mm(c_q, p['w_uq'][j], n_out=N_HEADS_C * HEAD_DIM, tm=tm, tn=1024, out_dtype=MXU)
            q_lat = _mm_heads_out(q, w['w_uk_t'][j], tm=_tile(t, TM_HEADS, CHUNK))
            q_i = _mm(c_q, p['w_qidx'][j], n_out=N_IDX_HEADS * D_IDX, tm=tm, tn=1024, out_dtype=MXU)
            kv_past = None if past is None else past[4][j]
            ki_past = None if past is None else past[5][j]
            sel_new, sel_past = _indexer(q_i, proj, ki_past, bsz=bsz, n=n, top_k=top_k)
            o_lat = _sparse_attn(q_lat, kv_lat, sel_new, kv_past, sel_past, p['rel_bias_table'],
                                 bsz=bsz, n=n)
            o = _mm_heads_in(o_lat, w['w_uv_h'][j], tm=_tile(t, TM_HEADS, CHUNK))
            odd_rows.append((kv_lat.reshape(bsz, n, KV_LORA), k_i.reshape(bsz, n, D_IDX)))
            x = _mm(o, p['w_out_odd'][j], n_out=d, tm=tm, tn=tn_d, gate=g1, res=x)
        h2, route = _norm_router(x, p['norm_ffn_g'][l], sc2, sh2, w['w_router'], p['router_bias'], tm=tmn)
        x = _moe(h2, route, w['w_gate'], w['w_up'], w['w_down'], l, g2, x, tm=tmn)
    y = _norm(x, p['final_norm_g'], tm=tmn).reshape(bsz, n, d)
    ev = tuple(jnp.stack([r[m] for r in even_rows]) for m in range(4))
    od = tuple(jnp.stack([r[m] for r in odd_rows]) for m in range(2))
    return y, ev, od


def kernel(x_prompt, x_sample, cache_a_k, cache_a_v, cache_b_k, cache_b_v, cache_c_kv, cache_c_idx,
           c_prompt, c_sample, rel_bias_table, norm_mix_g, norm_ffn_g, final_norm_g, w_ada, b_ada,
           w_in_even, lam_q1, lam_k1, lam_q2, lam_k2, subln_g, w_out_even, w_in_odd, g_q, g_kv,
           w_uq, w_qidx, w_uk, w_uv, w_out_odd, w_router, router_bias, w_gate, w_up, w_down):
    p = dict(rel_bias_table=rel_bias_table, norm_mix_g=norm_mix_g, norm_ffn_g=norm_ffn_g,
             final_norm_g=final_norm_g, w_ada=w_ada, b_ada=b_ada, w_in_even=w_in_even,
             lam_q1=lam_q1, lam_k1=lam_k1, lam_q2=lam_q2, lam_k2=lam_k2, subln_g=subln_g,
             w_out_even=w_out_even, g_q=g_q, g_kv=g_kv, w_uq=w_uq, w_qidx=w_qidx,
             w_out_odd=w_out_odd, router_bias=router_bias)
    d = x_prompt.shape[-1]
    depth = w_ada.shape[0]
    w = dict(
        w_in_odd=jnp.pad(w_in_odd, ((0, 0), (0, 0), (0, 10 * LANES - w_in_odd.shape[-1]))),
        w_uk_t=jnp.transpose(w_uk, (0, 2, 3, 1)),
        w_uv_h=jnp.transpose(w_uv, (0, 2, 1, 3)),
        w_router=jnp.pad(w_router, ((0, 0), (0, LANES - N_EXPERTS))),
        w_gate=w_gate.astype(MXU), w_up=w_up.astype(MXU), w_down=w_down.astype(MXU))
    nb_p, nb_s = c_prompt.shape[0], c_sample.shape[0]
    c_all = jnp.concatenate([c_prompt, c_sample], axis=0)
    rows = -(-(nb_p + nb_s) // 16) * 16
    c_act = jnp.pad(c_all * (1.0 / (1.0 + jnp.exp(-c_all))), ((0, rows - nb_p - nb_s), (0, 0)))
    tn_ada = _tile(6 * d, 1024, LANES)
    mods = [_mm(c_act, w_ada, layer=l, n_out=6 * d, tm=rows, tn=tn_ada, bias=b_ada[l], name="mm_ada")
            for l in range(depth)]
    mod_p = [m[:nb_p] for m in mods]
    mod_s = [m[nb_p:nb_p + nb_s] for m in mods]

    y_prompt, ev_p, od_p = _trunk(x_prompt, mod_p, None, p, w)
    past = (cache_a_k, cache_a_v, cache_b_k, cache_b_v, cache_c_kv, cache_c_idx)
    y_sample, ev_s, od_s = _trunk(x_sample, mod_s, past, p, w)
    return (y_prompt, y_sample) + ev_p + od_p + ev_s + od_s
```

```python
import functools
import math

import jax
import jax.numpy as jnp
import numpy as np
from jax import lax
from jax.experimental import pallas as pl
from jax.experimental.pallas import tpu as pltpu

F32 = jnp.float32
MXU = jnp.bfloat16

CHUNK = 64
HEAD_DIM = 128
N_HEADS_A = 8
N_MAPS_A = 16
DA_QK = 64
N_HEADS_B = 8
N_HEADS_C = 16
Q_LORA = 512
KV_LORA = 512
N_IDX_HEADS = 16
D_IDX = 128
TOPK_MAX = 256
N_BUCKETS = 32
MAX_DISTANCE = 128
N_EXPERTS = 16
N_GROUPS = 4
EXPERTS_PER_GROUP = 4
EPS = 1e-6
LANES = 128
TK = 256
TQ_SPARSE = 64
TQ_SPARSE_PAST = 32
TKC = 512
TM_HEADS = 2048
VMEM_LIMIT = 56 * 1024 * 1024
NEG_INF = float("-inf")


def _cparams(sem):
    return pltpu.CompilerParams(dimension_semantics=sem, vmem_limit_bytes=VMEM_LIMIT)


def _dot(a, b):
    return jnp.dot(a, b, preferred_element_type=F32)


def _dot_nt(a, b):
    return lax.dot_general(a, b, (((1,), (1,)), ((), ())), preferred_element_type=F32)


def _mm_kernel(*refs, has_bias, has_res, tm):
    it = iter(refs)
    a_ref, b_ref = next(it), next(it)
    bias_ref = next(it) if has_bias else None
    gate_ref, res_ref = (next(it), next(it)) if has_res else (None, None)
    o_ref, bsc = next(it), next(it)

    @pl.when(pl.program_id(1) == 0)
    def _():
        bsc[...] = b_ref[...].reshape(bsc.shape).astype(bsc.dtype)

    a = a_ref[...]
    a = a.reshape(a.shape[-2:]).astype(MXU)
    acc = _dot(a, bsc[...])
    if has_bias:
        acc = acc + bias_ref[...]
    if has_res:
        tn = acc.shape[-1]
        acc = (acc.reshape(tm // CHUNK, CHUNK, tn) * gate_ref[...]).reshape(tm, tn)
        acc = acc + res_ref[...]
    o_ref[...] = acc.reshape(o_ref.shape).astype(o_ref.dtype)


def _mm(a, b, *, n_out, tm, tn, bias=None, gate=None, res=None, out_dtype=F32, seg_out=False, layer=0,
        name="mm"):
    m, k = a.shape
    assert m % tm == 0 and n_out % tn == 0 and b.shape[-2] == k
    b_spec = (pl.BlockSpec((k, tn), lambda j, i: (0, j)) if b.ndim == 2
              else pl.BlockSpec((1, k, tn), lambda j, i: (layer, 0, j)))
    in_specs = [pl.BlockSpec((tm, k), lambda j, i: (i, 0)), b_spec]
    args = [a, b]
    if bias is not None:
        in_specs.append(pl.BlockSpec((1, tn), lambda j, i: (0, j)))
        args.append(bias.reshape(1, n_out))
    if res is not None:
        in_specs.append(pl.BlockSpec((tm // CHUNK, 1, tn), lambda j, i: (i, 0, j)))
        in_specs.append(pl.BlockSpec((tm, tn), lambda j, i: (i, j)))
        args += [gate, res]
    return pl.pallas_call(
        functools.partial(_mm_kernel, has_bias=bias is not None, has_res=res is not None, tm=tm),
        grid=(n_out // tn, m // tm), in_specs=in_specs,
        out_specs=(pl.BlockSpec((1, tm, tn), lambda j, i: (j, i, 0)) if seg_out
                   else pl.BlockSpec((tm, tn), lambda j, i: (i, j))),
        out_shape=jax.ShapeDtypeStruct((n_out // tn, m, tn) if seg_out else (m, n_out), out_dtype),
        scratch_shapes=[pltpu.VMEM((k, tn), MXU)],
        compiler_params=_cparams(("arbitrary", "arbitrary")), name=name,
    )(*args)


def _mm_heads_out(a, b_h, *, tm):
    m = a.shape[0]
    nh, ka, n = b_h.shape
    return pl.pallas_call(
        functools.partial(_mm_kernel, has_bias=False, has_res=False, tm=tm),
        grid=(nh, m // tm),
        in_specs=[pl.BlockSpec((tm, ka), lambda h, i: (i, h)),
                  pl.BlockSpec((1, ka, n), lambda h, i: (h, 0, 0))],
        out_specs=pl.BlockSpec((1, tm, n), lambda h, i: (h, i, 0)),
        out_shape=jax.ShapeDtypeStruct((nh, m, n), MXU),
        scratch_shapes=[pltpu.VMEM((ka, n), MXU)],
        compiler_params=_cparams(("arbitrary", "arbitrary")), name="mm_heads_out",
    )(a, b_h)


def _mm_heads_in(a_h, b_h, *, tm):
    nh, m, ka = a_h.shape
    n = b_h.shape[2]
    return pl.pallas_call(
        functools.partial(_mm_kernel, has_bias=False, has_res=False, tm=tm),
        grid=(nh, m // tm),
        in_specs=[pl.BlockSpec((1, tm, ka), lambda h, i: (h, i, 0)),
                  pl.BlockSpec((1, ka, n), lambda h, i: (h, 0, 0))],
        out_specs=pl.BlockSpec((tm, n), lambda h, i: (i, h)),
        out_shape=jax.ShapeDtypeStruct((m, nh * n), MXU),
        scratch_shapes=[pltpu.VMEM((ka, n), MXU)],
        compiler_params=_cparams(("arbitrary", "arbitrary")), name="mm_heads_in",
    )(a_h, b_h)


def _rms(x, g):
    return x * lax.rsqrt(jnp.mean(x * x, axis=-1, keepdims=True) + EPS) * g


def _modulate(y, sc_ref, sh_ref, tm):
    d = y.shape[-1]
    y3 = y.reshape(tm // CHUNK, CHUNK, d)
    y3 = y3 * (1.0 + sc_ref[...]) + sh_ref[...]
    return y3.reshape(tm, d)


def _norm_kernel(*refs, has_mod, tm):
    if has_mod:
        x_ref, g_ref, sc_ref, sh_ref, o_ref = refs
    else:
        x_ref, g_ref, o_ref = refs
    y = _rms(x_ref[...], g_ref[...])
    if has_mod:
        y = _modulate(y, sc_ref, sh_ref, tm)
    o_ref[...] = y.astype(o_ref.dtype)


def _norm(x, g, *, tm, width=None, col_block=0, sc=None, sh=None, out_dtype=F32):
    width = x.shape[1] if width is None else width
    rows = x.shape[0]
    assert rows % tm == 0
    has_mod = sc is not None
    in_specs = [pl.BlockSpec((tm, width), lambda i: (i, col_block)),
                pl.BlockSpec((1, width), lambda i: (0, 0))]
    args = [x, g.reshape(1, width)]
    if has_mod:
        in_specs += [pl.BlockSpec((tm // CHUNK, 1, width), lambda i: (i, 0, 0))] * 2
        args += [sc, sh]
    return pl.pallas_call(
        functools.partial(_norm_kernel, has_mod=has_mod, tm=tm),
        grid=(rows // tm,), in_specs=in_specs,
        out_specs=pl.BlockSpec((tm, width), lambda i: (i, 0)),
        out_shape=jax.ShapeDtypeStruct((rows, width), out_dtype),
        compiler_params=_cparams(("arbitrary",)), name="norm",
    )(*args)


def _split2(x):
    hi = x.astype(MXU)
    lo = (x - hi.astype(F32)).astype(MXU)
    return hi, lo


def _route(logits, rb_ref):
    tm = logits.shape[0]
    lt = logits.T
    aff = [1.0 / (1.0 + jnp.exp(-lt[e:e + 1, :])) for e in range(N_EXPERTS)]
    sc = [aff[e] + rb_ref[e] for e in range(N_EXPERTS)]
    npg = EXPERTS_PER_GROUP
    gscore = []
    for g in range(N_GROUPS):
        v = sc[g * npg:(g + 1) * npg]
        best = None
        for a in range(npg):
            for b in range(a + 1, npg):
                s = v[a] + v[b]
                best = s if best is None else jnp.maximum(best, s)
        gscore.append(best)
    gb = jnp.zeros((1, tm), jnp.int32)
    gv = gscore[0]
    for g in range(1, N_GROUPS):
        better = gscore[g] > gv
        gb = jnp.where(better, g, gb)
        gv = jnp.where(better, gscore[g], gv)
    u = [sc[j] for j in range(npg)]
    a4 = [aff[j] for j in range(npg)]
    for g in range(1, N_GROUPS):
        pick = gb == g
        u = [jnp.where(pick, sc[g * npg + j], u[j]) for j in range(npg)]
        a4 = [jnp.where(pick, aff[g * npg + j], a4[j]) for j in range(npg)]
    i1 = jnp.zeros((1, tm), jnp.int32)
    v1 = u[0]
    for j in range(1, npg):
        better = u[j] > v1
        i1 = jnp.where(better, j, i1)
        v1 = jnp.where(better, u[j], v1)
    i2 = jnp.full((1, tm), -1, jnp.int32)
    v2 = jnp.full((1, tm), NEG_INF, F32)
    for j in range(npg):
        better = (i1 != j) & ((u[j] > v2) | (i2 < 0))
        i2 = jnp.where(better, j, i2)
        v2 = jnp.where(better, u[j], v2)
    w1 = a4[0]
    w2 = a4[0]
    for j in range(1, npg):
        w1 = jnp.where(i1 == j, a4[j], w1)
        w2 = jnp.where(i2 == j, a4[j], w2)
    tot = w1 + w2
    w1 = w1 / tot
    w2 = w2 / tot
    e1 = gb * npg + i1
    e2 = gb * npg + i2
    rows = jnp.concatenate([e1.astype(F32), e2.astype(F32), w1, w2, jnp.zeros((LANES - 4, tm), F32)], axis=0)
    return rows.T


def _norm_router_kernel(x_ref, g_ref, sc_ref, sh_ref, wr_ref, rb_ref, h_ref, route_ref, *, tm):
    h = _modulate(_rms(x_ref[...], g_ref[...]), sc_ref, sh_ref, tm)
    h_ref[...] = h
    hh, hl = _split2(h)
    wh, wl = _split2(wr_ref[...])
    logits = _dot(hh, wh) + (_dot(hl, wh) + _dot(hh, wl))
    route_ref[...] = _route(logits, rb_ref)


def _norm_router(x, g, sc, sh, w_router_pad, router_bias, *, tm):
    t, d = x.shape
    return pl.pallas_call(
        functools.partial(_norm_router_kernel, tm=tm),
        grid=(t // tm,),
        in_specs=[pl.BlockSpec((tm, d), lambda i: (i, 0)),
                  pl.BlockSpec((1, d), lambda i: (0, 0)),
                  pl.BlockSpec((tm // CHUNK, 1, d), lambda i: (i, 0, 0)),
                  pl.BlockSpec((tm // CHUNK, 1, d), lambda i: (i, 0, 0)),
                  pl.BlockSpec((d, LANES), lambda i: (0, 0)),
                  pl.BlockSpec(memory_space=pltpu.SMEM)],
        out_specs=[pl.BlockSpec((tm, d), lambda i: (i, 0)),
                   pl.BlockSpec((tm, LANES), lambda i: (i, 0))],
        out_shape=[jax.ShapeDtypeStruct((t, d), F32), jax.ShapeDtypeStruct((t, LANES), F32)],
        compiler_params=_cparams(("arbitrary",)), name="norm_router",
    )(x, g.reshape(1, d), sc, sh, w_router_pad, router_bias)


TM_EXPERT = 256
TM_DISPATCH = 256


def _row_copy(src, src_row, dst, dst_row, sem):
    return pltpu.make_async_copy(src.at[pl.ds(src_row, 1), :], dst.at[pl.ds(dst_row, 1), :], sem)


def _rows_wait(src, dst, sem, n):
    pltpu.make_async_copy(src.at[pl.ds(0, n), :], dst.at[pl.ds(0, n), :], sem).wait()


def _dispatch_kernel(dest_ref, pad_ref, h_ref, xs_hbm, sems, *, tb, t, n_pad):
    i = pl.program_id(0)
    sem = sems.at[0]

    def body(r, c):
        tok = i * tb + r
        _row_copy(h_ref, r, xs_hbm, dest_ref[tok], sem).start()
        _row_copy(h_ref, r, xs_hbm, dest_ref[t + tok], sem).start()
        return c

    lax.fori_loop(0, tb, body, 0, unroll=8)
    for _ in range(2):
        _rows_wait(h_ref, xs_hbm, sem, tb)

    @pl.when(i == 0)
    def _():
        for c0 in range(0, n_pad, tb):
            nc = min(tb, n_pad - c0)

            def pad_body(r, c, c0=c0):
                _row_copy(h_ref, 0, xs_hbm, pad_ref[c0 + r], sem).start()
                return c

            lax.fori_loop(0, nc, pad_body, 0, unroll=8)
            _rows_wait(h_ref, xs_hbm, sem, nc)


def _expert_kernel(te_ref, nv_ref, x_ref, wg_ref, wu_ref, wd_ref, y_ref):
    del te_ref
    i = pl.program_id(0)

    @pl.when(i < nv_ref[0])
    def _():
        x = x_ref[...].astype(MXU)
        a = _dot(x, wg_ref[0, 0])
        b = _dot(x, wu_ref[0, 0])
        he = (a / (1.0 + jnp.exp(-a))) * b
        y_ref[...] = _dot(he.astype(MXU), wd_ref[0, 0])

    @pl.when(i >= nv_ref[0])
    def _():
        y_ref[...] = jnp.zeros_like(y_ref)


def _combine_kernel(dest_ref, y_hbm, route_ref, res_ref, g2_ref, o_ref, ybuf, sem, *, tm, t):
    i = pl.program_id(0)
    n = pl.num_programs(0)
    slot = i % 2

    def start(step, s):
        for j in range(2):
            def body(r, c):
                _row_copy(y_hbm, dest_ref[j * t + step * tm + r], ybuf.at[s, j], r, sem.at[s]).start()
                return c

            lax.fori_loop(0, tm, body, 0, unroll=8)

    @pl.when(i == 0)
    def _():
        start(0, 0)

    @pl.when(i + 1 < n)
    def _():
        start(i + 1, 1 - slot)

    for j in range(2):
        _rows_wait(y_hbm, ybuf.at[slot, j], sem.at[slot], tm)
    d = o_ref.shape[-1]
    route = route_ref[...]
    y = route[:, 2:3] * ybuf[slot, 0] + route[:, 3:4] * ybuf[slot, 1]
    o_ref[...] = res_ref[...] + (y.reshape(tm // CHUNK, CHUNK, d) * g2_ref[...]).reshape(tm, d)


def _cumsum_rows(x, blk):
    n, e = x.shape
    xb = x.reshape(n // blk, blk, e)
    tri = jnp.asarray(np.tril(np.ones((blk, blk), np.float32)))
    local = jnp.einsum('ij,bje->bie', tri, xb, precision=lax.Precision.HIGHEST)
    tot = local[:, -1, :]
    return (local + (jnp.cumsum(tot, axis=0) - tot)[:, None, :]).reshape(n, e)


def _moe(h, route, w_gate, w_up, w_down, layer, g2, res, *, tm):
    t, d = h.shape
    de = w_gate.shape[-1]
    te = _tile(2 * t, TM_EXPERT, CHUNK)
    n_tiles = 2 * t // te + N_EXPERTS
    r_pad = n_tiles * te
    n_pad = r_pad - 2 * t
    e_flat = route[:, 0:2].T.reshape(-1)
    experts = jnp.arange(N_EXPERTS, dtype=F32)
    onehot = (e_flat[:, None] == experts[None, :]).astype(F32)
    csum = _cumsum_rows(onehot, _tile(2 * t, 256, 8))
    counts = csum[-1]
    tiles_e = jnp.ceil(counts / te)
    tile_end = jnp.cumsum(tiles_e)
    seg_start = (tile_end - tiles_e) * te
    dest = jnp.sum(onehot * (seg_start[None, :] + csum - 1.0), axis=1).astype(jnp.int32)
    gap_start = jnp.concatenate([seg_start + counts, tile_end[-1:] * te])
    gap_len = jnp.concatenate([tiles_e * te - counts, r_pad - tile_end[-1:] * te])
    gap_first = jnp.cumsum(gap_len) - gap_len
    k = jnp.arange(n_pad, dtype=F32)[:, None]
    in_gap = (k >= gap_first[None, :]) & (k < (gap_first + gap_len)[None, :])
    pad_rows = jnp.sum(jnp.where(in_gap, gap_start[None, :] + k - gap_first[None, :], 0.0),
                       axis=1).astype(jnp.int32)
    tile_id = jnp.arange(n_tiles, dtype=F32)[:, None]
    tile_e = jnp.minimum(jnp.sum((tile_end[None, :] <= tile_id).astype(jnp.int32), axis=1),
                         N_EXPERTS - 1)
    n_valid = tile_end[-1:].astype(jnp.int32)

    tb = _tile(t, TM_DISPATCH, CHUNK)
    xs = pl.pallas_call(
        functools.partial(_dispatch_kernel, tb=tb, t=t, n_pad=n_pad),
        grid_spec=pltpu.PrefetchScalarGridSpec(
            num_scalar_prefetch=2, grid=(t // tb,),
            in_specs=[pl.BlockSpec((tb, d), lambda i, dst, pad: (i, 0))],
            out_specs=pl.BlockSpec(memory_space=pl.ANY),
            scratch_shapes=[pltpu.SemaphoreType.DMA((1,))]),
        out_shape=jax.ShapeDtypeStruct((r_pad, d), F32),
        compiler_params=_cparams(("arbitrary",)), name="moe_dispatch",
    )(dest, pad_rows, h)

    y = pl.pallas_call(
        _expert_kernel,
        grid_spec=pltpu.PrefetchScalarGridSpec(
            num_scalar_prefetch=2, grid=(n_tiles,),
            in_specs=[pl.BlockSpec((te, d), lambda i, e, n: (i, 0)),
                      pl.BlockSpec((1, 1, d, de), lambda i, e, n: (layer, e[i], 0, 0)),
                      pl.BlockSpec((1, 1, d, de), lambda i, e, n: (layer, e[i], 0, 0)),
                      pl.BlockSpec((1, 1, de, d), lambda i, e, n: (layer, e[i], 0, 0))],
            out_specs=pl.BlockSpec((te, d), lambda i, e, n: (i, 0))),
        out_shape=jax.ShapeDtypeStruct((r_pad, d), F32),
        compiler_params=_cparams(("arbitrary",)), name="moe_experts",
    )(tile_e, n_valid, xs, w_gate, w_up, w_down)

    return pl.pallas_call(
        functools.partial(_combine_kernel, tm=tm, t=t),
        grid_spec=pltpu.PrefetchScalarGridSpec(
            num_scalar_prefetch=1, grid=(t // tm,),
            in_specs=[pl.BlockSpec(memory_space=pl.ANY),
                      pl.BlockSpec((tm, LANES), lambda i, dst: (i, 0)),
                      pl.BlockSpec((tm, d), lambda i, dst: (i, 0)),
                      pl.BlockSpec((tm // CHUNK, 1, d), lambda i, dst: (i, 0, 0))],
            out_specs=pl.BlockSpec((tm, d), lambda i, dst: (i, 0)),
            scratch_shapes=[pltpu.VMEM((2, 2, tm, d), F32), pltpu.SemaphoreType.DMA((2,))]),
        out_shape=jax.ShapeDtypeStruct((t, d), F32),
        compiler_params=_cparams(("arbitrary",)), name="moe_combine",
    )(dest, y, route, res, g2)


def _rel_bucket(rel):
    half = N_BUCKETS // 2
    max_exact = half // 2
    n = jnp.abs(rel)
    nf = jnp.maximum(n, 1).astype(F32)
    large = max_exact + (jnp.log(nf / max_exact) / math.log(MAX_DISTANCE / max_exact)
                         * (half - max_exact)).astype(jnp.int32)
    large = jnp.minimum(large, half - 1)
    return jnp.where(rel > 0, half, 0) + jnp.where(n < max_exact, n, large)


def _bias_tiles(table, q_offsets, k_offset, tq, tk, n_heads):
    tiles = []
    for q0 in q_offsets:
        rel = (k_offset + np.arange(tk))[None, :] - (q0 + np.arange(tq))[:, None]
        bucket = _rel_bucket(jnp.asarray(rel, jnp.int32))
        onehot = (bucket[..., None] == jnp.arange(N_BUCKETS, dtype=jnp.int32)).astype(F32)
        tiles.append(jnp.einsum('qkb,bh->hqk', onehot, table[:, :n_heads].astype(F32),
                                precision=lax.Precision.HIGHEST))
    return jnp.stack(tiles)


def _bias_far(table, n_heads):
    rel = jnp.full((1,), -(MAX_DISTANCE + 1), jnp.int32)
    return table[_rel_bucket(rel)][0, :n_heads].astype(F32)


def _diff_attend(qr, far, prev, diag, bias_far, bp_ref, bd_ref, vis):
    outs = []
    for r in range(2):
        cols = slice(r * DA_QK, (r + 1) * DA_QK)
        parts = []
        if far is not None:
            s = _dot_nt(qr[r], far[0][:, cols].astype(MXU)) + bias_far[r]
            parts.append((s, far[1]))
        if prev is not None:
            s = _dot_nt(qr[r], prev[0][:, cols].astype(MXU)) + bp_ref[0, r]
            parts.append((s, prev[1]))
        s = _dot_nt(qr[r], diag[0][:, cols].astype(MXU)) + bd_ref[0, r]
        parts.append((jnp.where(vis, s, NEG_INF), diag[1]))
        m = None
        for s, _ in parts:
            mx = jnp.max(s, axis=-1, keepdims=True)
            m = mx if m is None else jnp.maximum(m, mx)
        l = None
        acc = None
        for s, v in parts:
            p = jnp.exp(s - m)
            ps = jnp.sum(p, axis=-1, keepdims=True)
            pv = _dot(p.astype(MXU), v.astype(MXU))
            l = ps if l is None else l + ps
            acc = pv if acc is None else acc + pv
        outs.append(acc / l)
    return outs


def _stick_attend(qbs, blocks, tri_ref, trid_ref, carries):
    nh = len(qbs)
    pairs = [(h, j) for h in range(nh) for j in range(len(blocks[h]))]
    log_beta, log_keep, later, total = {}, {}, {}, {}
    for h, j in pairs:
        kb, _, mask = blocks[h][j]
        z = _dot_nt(qbs[h], kb.astype(MXU)) * HEAD_DIM ** -0.5
        log_beta[h, j] = jnp.minimum(z, 0.0) - jnp.log(1.0 + jnp.exp(-jnp.abs(z)))
        lk = log_beta[h, j] - z
        log_keep[h, j] = lk if mask is None else jnp.where(mask, lk, 0.0)
    for h, j in pairs:
        lk = log_keep[h, j]
        tri = (trid_ref if lk.shape[1] != TK else tri_ref)[...]
        hi, lo = _split2(lk)
        later[h, j] = _dot(hi, tri) + _dot(lo, tri)
        total[h, j] = jnp.broadcast_to(later[h, j][:, 0:1] + lk[:, 0:1], (lk.shape[0], LANES))
    accs, new_carries = [], []
    for h in range(nh):
        carry = carries[h]
        acc = None
        for j, (_, vb, mask) in enumerate(blocks[h]):
            lt = later[h, j]
            tk = lt.shape[1]
            if carry is not None:
                lt = lt + (carry[:, :tk] if tk <= LANES else jnp.concatenate([carry] * (tk // LANES), axis=1))
            a = jnp.exp(log_beta[h, j] + lt)
            if mask is not None:
                a = jnp.where(mask, a, 0.0)
            pv = _dot(a.astype(MXU), vb.astype(MXU))
            acc = pv if acc is None else acc + pv
            carry = total[h, j] if carry is None else carry + total[h, j]
        accs.append(acc)
        new_carries.append(carry)
    return accs, new_carries


def _even_attn_kernel(far_ref, lam_ref, q_ref, kn_ref, vn_ref, bd_ref, bp_ref, g_ref, tri_ref, o_ref,
                      *, tq, nq, lam_init):
    u = pl.program_id(1)
    i = pl.program_id(2)
    q = q_ref[0]
    qq = lax.broadcasted_iota(jnp.int32, (tq, TK), 0)
    kk = lax.broadcasted_iota(jnp.int32, (tq, TK), 1)

    def regions(c):
        far = (0, (c - 1) * TK) if c >= 2 else None
        prev = ((c - 1) * TK, c * TK) if c >= 1 else None
        return far, prev, (c * TK, (c + 1) * TK)

    def load(reg, lo=None, hi=None):
        a, b = reg
        lo, hi = (a, b) if lo is None else (a + lo, a + hi)
        return kn_ref[0, lo:hi, :], vn_ref[0, lo:hi, :]

    def diff(c):
        far, prev, diag = regions(c)
        qr = [(q[:, r * DA_QK:(r + 1) * DA_QK] * DA_QK ** -0.5).astype(MXU) for r in range(2)]
        vis = (kk // CHUNK) <= (qq // CHUNK)
        o0, o1 = _diff_attend(qr, None if far is None else load(far), None if prev is None else load(prev),
                              load(diag), [far_ref[2 * u], far_ref[2 * u + 1]], bp_ref, bd_ref, vis)
        o = _rms(o0 - lam_ref[0] * o1, g_ref[...]) * (1.0 - lam_init)
        o_ref[...] = o.astype(o_ref.dtype)

    def stick(c):
        nb = c + 1
        k, v = load((0, nb * TK))
        z = _dot_nt(q.astype(MXU), k.astype(MXU)) * HEAD_DIM ** -0.5
        log_beta = jnp.minimum(z, 0.0) - jnp.log(1.0 + jnp.exp(-jnp.abs(z)))
        log_keep = log_beta - z
        earlier = kk < qq
        tri = tri_ref[...]
        later, total = [], []
        for j in range(nb):
            lk = log_keep[:, j * TK:(j + 1) * TK]
            if j == nb - 1:
                lk = jnp.where(earlier, lk, 0.0)
            hi, lo = _split2(lk)
            lt = _dot(hi, tri) + _dot(lo, tri)
            later.append(lt)
            total.append(jnp.broadcast_to(lt[:, 0:1] + lk[:, 0:1], (tq, LANES)))
        carry = None
        for j in range(nb - 1, -1, -1):
            if carry is not None:
                later[j] = later[j] + jnp.concatenate([carry] * (TK // LANES), axis=1)
            carry = total[j] if carry is None else carry + total[j]
        a = [jnp.exp(log_beta[:, j * TK:(j + 1) * TK] + later[j]) for j in range(nb)]
        a[-1] = jnp.where(earlier, a[-1], 0.0)
        a = a[0] if nb == 1 else jnp.concatenate(a, axis=1)
        o_ref[...] = _dot(a.astype(MXU), v.astype(MXU)).astype(o_ref.dtype)

    for c in range(nq):
        pl.when((u < N_HEADS_A) & (i == c))(functools.partial(diff, c))
        pl.when((u >= N_HEADS_A) & (i == c))(functools.partial(stick, c))


def _tri_later(tk):
    return jnp.asarray(np.arange(tk)[:, None] > np.arange(tk)[None, :], MXU)


def _even_attn(qkv, table, lam, subln_g, lam_init, *, bsz, n):
    tq = TK
    assert n % tq == 0
    nq = n // tq
    far = _bias_far(table, N_MAPS_A)
    bd = _bias_tiles(table, [0], 0, tq, TK, N_MAPS_A)
    bp = _bias_tiles(table, [0], -TK, tq, TK, N_MAPS_A)

    def seg(u, base):
        return jnp.where(u < 8, base, base + 3)

    smem = pl.BlockSpec(memory_space=pltpu.SMEM)
    in_specs = [smem, smem,
                pl.BlockSpec((1, tq, HEAD_DIM), lambda b, u, i: (seg(u, 0), b * nq + i, u % 8)),
                pl.BlockSpec((1, n, HEAD_DIM), lambda b, u, i: (seg(u, 1), b, u % 8)),
                pl.BlockSpec((1, n, HEAD_DIM), lambda b, u, i: (seg(u, 2), b, u % 8)),
                pl.BlockSpec((1, 2, tq, TK), lambda b, u, i: (0, jnp.minimum(u, 7), 0, 0)),
                pl.BlockSpec((1, 2, tq, TK), lambda b, u, i: (0, jnp.minimum(u, 7), 0, 0)),
                pl.BlockSpec((1, HEAD_DIM), lambda b, u, i: (0, 0)),
                pl.BlockSpec((TK, TK), lambda b, u, i: (0, 0))]
    return pl.pallas_call(
        functools.partial(_even_attn_kernel, tq=tq, nq=nq, lam_init=lam_init),
        grid=(bsz, 16, nq), in_specs=in_specs,
        out_specs=pl.BlockSpec((tq, HEAD_DIM), lambda b, u, i: (b * nq + i, u)),
        out_shape=jax.ShapeDtypeStruct((bsz * n, 16 * HEAD_DIM), MXU),
        compiler_params=_cparams(("arbitrary", "arbitrary", "arbitrary")), name="even_attn",
    )(far, lam.reshape(1), qkv, qkv, qkv, bd, bp, subln_g.reshape(1, HEAD_DIM), _tri_later(TK))


def _even_attn_past_kernel(lam_ref, q_ref, kn_ref, vn_ref, ak_hbm, av_hbm, bk_hbm, bv_hbm, bd_ref, bc_ref,
                           g_ref, tri_ref, trid_ref, o_ref, m_ref, l_ref, acc_ref, carry_ref,
                           ka_buf, kb_buf, v_buf, sem, *, n, tkc, nkc, lam_init):
    b = pl.program_id(0)
    mix = pl.program_id(1)
    kc = pl.program_id(2)
    step = (b * 2 + mix) * nkc + kc
    n_steps = pl.num_programs(0) * 2 * nkc
    slot = step % 2
    qq = lax.broadcasted_iota(jnp.int32, (n, n), 0)
    kk = lax.broadcasted_iota(jnp.int32, (n, n), 1)

    def chunk_copies(bb, mm_is_stick, cc, s):
        if mm_is_stick:
            rows = pl.ds(pl.multiple_of((nkc - 1 - cc) * tkc, tkc), tkc)
            return ([pltpu.make_async_copy(bk_hbm.at[bb, rows, h, :], kb_buf.at[s, h], sem.at[s])
                     for h in range(N_HEADS_B)] +
                    [pltpu.make_async_copy(bv_hbm.at[bb, rows, h, :], v_buf.at[s, h], sem.at[s])
                     for h in range(N_HEADS_B)])
        rows = pl.ds(pl.multiple_of(cc * tkc, tkc), tkc)
        return ([pltpu.make_async_copy(ak_hbm.at[bb, rows, pl.ds(h * HEAD_DIM, HEAD_DIM)], ka_buf.at[s, h],
                                       sem.at[s]) for h in range(N_HEADS_A)] +
                [pltpu.make_async_copy(av_hbm.at[bb, rows, h, :], v_buf.at[s, h], sem.at[s])
                 for h in range(N_HEADS_A)])

    def start(bb, mm, cc, s):
        for stick in (False, True):
            @pl.when(mm == int(stick))
            def _(stick=stick):
                for cp in chunk_copies(bb, stick, cc, s):
                    cp.start()

    @pl.when(step == 0)
    def _():
        start(b, mix, kc, slot)

    @pl.when(step + 1 < n_steps)
    def _():
        nxt = step + 1
        start(nxt // (2 * nkc), (nxt // nkc) % 2, nxt % nkc, 1 - slot)

    for stick in (False, True):
        @pl.when(mix == int(stick))
        def _(stick=stick):
            for cp in chunk_copies(b, stick, kc, slot):
                cp.wait()

    def head(ref, h, width=HEAD_DIM, off=0):
        return ref[0, :, h * HEAD_DIM + off:h * HEAD_DIM + off + width]

    def diff_update(s, vs, first):
        mx = jnp.max(s, axis=-1, keepdims=True)
        m_new = mx if first else jnp.maximum(m_ref[...], mx)
        p = jnp.exp(s - m_new)
        ps = jnp.sum(p, axis=-1, keepdims=True)
        pv = jnp.stack([_dot(p[mp].astype(MXU), vs[mp // 2].astype(MXU)) for mp in range(N_MAPS_A)])
        if first:
            l_ref[...] = ps
            acc_ref[...] = pv
        else:
            alpha = jnp.exp(m_ref[...] - m_new)
            l_ref[...] = alpha * l_ref[...] + ps
            acc_ref[...] = alpha * acc_ref[...] + pv
        m_ref[...] = m_new

    def diff_q(mp):
        h, r = divmod(mp, 2)
        return (head(q_ref, h, DA_QK, r * DA_QK) * DA_QK ** -0.5).astype(MXU)

    @pl.when((mix == 0) & (kc == 0))
    def _():
        vis = (kk // CHUNK) <= (qq // CHUNK)
        s = jnp.stack([_dot_nt(diff_q(mp), head(kn_ref, mp // 2, DA_QK, (mp % 2) * DA_QK).astype(MXU))
                       for mp in range(N_MAPS_A)]) + bd_ref[0]
        diff_update(jnp.where(vis[None], s, NEG_INF), [head(vn_ref, h) for h in range(N_HEADS_A)], True)

    @pl.when(mix == 0)
    def _():
        kh = [ka_buf[slot, h].astype(MXU) for h in range(N_HEADS_A)]
        s = jnp.stack([_dot_nt(diff_q(mp), kh[mp // 2][:, (mp % 2) * DA_QK:(mp % 2 + 1) * DA_QK])
                       for mp in range(N_MAPS_A)]) + bc_ref[0]
        diff_update(s, [v_buf[slot, h] for h in range(N_HEADS_A)], False)

    @pl.when((mix == 0) & (kc == nkc - 1))
    def _():
        for h in range(N_HEADS_A):
            o = acc_ref[2 * h] / l_ref[2 * h] - lam_ref[0] * (acc_ref[2 * h + 1] / l_ref[2 * h + 1])
            o = _rms(o, g_ref[...]) * (1.0 - lam_init)
            o_ref[:, h * HEAD_DIM:(h + 1) * HEAD_DIM] = o.astype(o_ref.dtype)

    def stick_q(h):
        return head(q_ref, h).astype(MXU)

    @pl.when((mix == 1) & (kc == 0))
    def _():
        heads = range(N_HEADS_B)
        accs, carries = _stick_attend([stick_q(h) for h in heads],
                                      [[(head(kn_ref, h), head(vn_ref, h), kk < qq)] for h in heads],
                                      tri_ref, trid_ref, [None] * N_HEADS_B)
        acc_ref[:N_HEADS_B] = jnp.stack(accs)
        carry_ref[...] = jnp.stack(carries)

    @pl.when(mix == 1)
    def _():
        carry = carry_ref[...]
        heads = range(N_HEADS_B)
        blocks = [[(kb_buf[slot, h, j * TK:(j + 1) * TK, :], v_buf[slot, h, j * TK:(j + 1) * TK, :], None)
                   for j in range(tkc // TK - 1, -1, -1)] for h in heads]
        accs, carries = _stick_attend([stick_q(h) for h in heads], blocks, tri_ref, trid_ref,
                                      [carry[h] for h in heads])
        acc_ref[:N_HEADS_B] = acc_ref[:N_HEADS_B] + jnp.stack(accs)
        carry_ref[...] = jnp.stack(carries)

    @pl.when((mix == 1) & (kc == nkc - 1))
    def _():
        for h in range(N_HEADS_B):
            o_ref[:, h * HEAD_DIM:(h + 1) * HEAD_DIM] = acc_ref[h].astype(o_ref.dtype)


def _even_attn_past(qkv, past, table, lam, subln_g, lam_init, *, bsz, n):
    ak, av, bk, bv = past
    p_len = ak.shape[1]
    ak = ak.reshape(bsz, p_len, N_MAPS_A * DA_QK)
    tkc = _tile(p_len, TKC, TK)
    nkc = p_len // tkc
    assert n <= CHUNK and n % 8 == 0
    bd = _bias_tiles(table, [0], 0, n, n, N_MAPS_A)
    far = jnp.broadcast_to(_bias_far(table, N_MAPS_A)[:, None, None], (N_MAPS_A, n, tkc))
    bc = jnp.stack([far, _bias_tiles(table, [0], -tkc, n, tkc, N_MAPS_A)[0]])

    def row(s):
        return lambda b, mix, kc: (3 * mix + s, b, 0)

    width = N_HEADS_A * HEAD_DIM
    in_specs = [pl.BlockSpec(memory_space=pltpu.SMEM),
                pl.BlockSpec((1, n, width), row(0)),
                pl.BlockSpec((1, n, width), row(1)),
                pl.BlockSpec((1, n, width), row(2)),
                pl.BlockSpec(memory_space=pl.ANY), pl.BlockSpec(memory_space=pl.ANY),
                pl.BlockSpec(memory_space=pl.ANY), pl.BlockSpec(memory_space=pl.ANY),
                pl.BlockSpec((1, N_MAPS_A, n, n), lambda b, mix, kc: (0, 0, 0, 0)),
                pl.BlockSpec((1, N_MAPS_A, n, tkc), lambda b, mix, kc: ((kc + 1) // nkc, 0, 0, 0)),
                pl.BlockSpec((1, HEAD_DIM), lambda b, mix, kc: (0, 0)),
                pl.BlockSpec((TK, TK), lambda b, mix, kc: (0, 0)),
                pl.BlockSpec((n, n), lambda b, mix, kc: (0, 0))]
    return pl.pallas_call(
        functools.partial(_even_attn_past_kernel, n=n, tkc=tkc, nkc=nkc, lam_init=lam_init),
        grid=(bsz, 2, nkc), in_specs=in_specs,
        out_specs=pl.BlockSpec((n, width), lambda b, mix, kc: (b, mix)),
        out_shape=jax.ShapeDtypeStruct((bsz * n, 2 * width), MXU),
        scratch_shapes=[pltpu.VMEM((N_MAPS_A, n, 1), F32), pltpu.VMEM((N_MAPS_A, n, 1), F32),
                        pltpu.VMEM((N_MAPS_A, n, HEAD_DIM), F32), pltpu.VMEM((N_HEADS_B, n, LANES), F32),
                        pltpu.VMEM((2, N_HEADS_A, tkc, HEAD_DIM), F32), pltpu.VMEM((2, N_HEADS_B, tkc, HEAD_DIM), F32),
                        pltpu.VMEM((2, N_HEADS_A, tkc, HEAD_DIM), F32), pltpu.SemaphoreType.DMA((2,))],
        compiler_params=_cparams(("arbitrary", "arbitrary", "arbitrary")), name="even_attn_past",
    )(lam.reshape(1), qkv, qkv, qkv, ak, av, bk, bv, bd, bc, subln_g.reshape(1, HEAD_DIM),
      _tri_later(TK), _tri_later(n))


INT_MIN = int(np.iinfo(np.int32).min)
NEG_INF_KEY = int(np.array(-np.inf, np.float32).view(np.int32)) ^ 0x7FFFFFFF
INDEX_BITS = 15


def _sort_key(x):
    b = lax.bitcast_convert_type(x, jnp.int32)
    return b ^ ((b >> 31) & 0x7FFFFFFF)


def _indexer_kernel(*refs, tq, n, p_len, top_k):
    has_past = p_len > 0
    it = iter(refs)
    qi_ref, wi_ref, kin_ref = next(it), next(it), next(it)
    kip_ref = next(it) if has_past else None
    seln_ref = next(it)
    selp_ref = next(it) if has_past else None
    keyn_ref = next(it)
    keyp_ref = next(it) if has_past else None

    i = pl.program_id(1)
    qi = qi_ref[...]
    wi = wi_ref[...] * (N_IDX_HEADS ** -0.5 * D_IDX ** -0.5)

    def scores(kmat):
        kb = kmat.astype(MXU)
        acc = None
        for ih in range(N_IDX_HEADS):
            d = _dot_nt(qi[:, ih * D_IDX:(ih + 1) * D_IDX], kb)
            term = wi[:, ih:ih + 1] * jnp.maximum(d, 0.0)
            acc = term if acc is None else acc + term
        return acc

    def run(w_new):
        s_new = scores(kin_ref[:w_new, :])
        qq = i * tq + lax.broadcasted_iota(jnp.int32, (tq, w_new), 0)
        kk = lax.broadcasted_iota(jnp.int32, (tq, w_new), 1)
        s_new = jnp.where((kk // CHUNK) <= (qq // CHUNK), s_new, NEG_INF)
        keyn_ref[:, :w_new] = _sort_key(s_new)
        parts = [(keyn_ref, p_len, w_new, seln_ref)]
        if has_past:
            keyp_ref[...] = _sort_key(scores(kip_ref[0]))
            parts.append((keyp_ref, 0, p_len, selp_ref))

        def count(pred):
            tot = None
            for ref, base, width, _ in parts:
                idx = base + lax.broadcasted_iota(jnp.int32, (tq, width), 1)
                c = jnp.sum(jnp.where(pred(ref[:, :width], idx), 1.0, 0.0), axis=-1, keepdims=True)
                tot = c if tot is None else tot + c
            return tot

        def body(b, t):
            lo_bit = jnp.left_shift(jnp.int32(1), 30 - 2 * b)
            cands = [t + lo_bit, t + 2 * lo_bit, t + 3 * lo_bit]
            counts = [None] * 3
            for ref, _, width, _ in parts:
                k = ref[:, :width]
                for j in range(3):
                    c = jnp.sum(jnp.where(k >= cands[j], 1.0, 0.0), axis=-1, keepdims=True)
                    counts[j] = c if counts[j] is None else counts[j] + c
            for j in range(3):
                t = jnp.where(counts[j] >= top_k, cands[j], t)
            return t

        t = lax.fori_loop(0, 16, body, jnp.full((tq, 1), INT_MIN, jnp.int32))
        n_gt = count(lambda k, idx: k > t)
        n_eq = count(lambda k, idx: k == t)
        need = top_k - n_gt
        tie = jnp.where((n_eq != need) & (t > NEG_INF_KEY), 1.0, 0.0)

        def write(sel_of):
            for ref, base, width, out_ref in parts:
                k = ref[:, :width]
                idx = base + lax.broadcasted_iota(jnp.int32, (tq, width), 1)
                out_ref[:, :width] = jnp.where(sel_of(k, idx) & (k > NEG_INF_KEY), 1.0, 0.0)

        write(lambda k, idx: k >= t)
        if w_new < n:
            seln_ref[:, w_new:] = jnp.zeros((tq, n - w_new), F32)

        @pl.when(jnp.max(tie) > 0.0)
        def _():
            def jbody(b, jv):
                cand = jv + jnp.left_shift(jnp.int32(1), INDEX_BITS - 1 - b)
                c = count(lambda k, idx: (k == t) & (idx < cand))
                return jnp.where(c < need, cand, jv)

            jv = lax.fori_loop(0, INDEX_BITS, jbody, jnp.zeros((tq, 1), jnp.int32))
            write(lambda k, idx: (k > t) | ((k == t) & (idx <= jv)))

    if has_past or n <= TK:
        run(n)
    else:
        for c in range(n // TK):
            pl.when((i * tq) // TK == c)(functools.partial(run, (c + 1) * TK))


def _indexer(q_i, proj, k_idx_past, *, bsz, n, top_k):
    has_past = k_idx_past is not None
    tq = min(n, 128)
    nq = n // tq
    p_len = k_idx_past.shape[1] if has_past else 0
    assert n + p_len < 2 ** (INDEX_BITS - 1) and (n <= TK or n % TK == 0)
    t = bsz * n
    in_specs = [pl.BlockSpec((tq, N_IDX_HEADS * D_IDX), lambda b, i: (b * nq + i, 0)),
                pl.BlockSpec((tq, LANES), lambda b, i: (b * nq + i, 9)),
                pl.BlockSpec((n, D_IDX), lambda b, i: (b, 8))]
    args = [q_i, proj, proj]
    out_specs = [pl.BlockSpec((tq, n), lambda b, i: (b * nq + i, 0))]
    out_shape = [jax.ShapeDtypeStruct((t, n), F32)]
    scratch = [pltpu.VMEM((tq, n), jnp.int32)]
    if has_past:
        in_specs.append(pl.BlockSpec((1, p_len, D_IDX), lambda b, i: (b, 0, 0)))
        args.append(k_idx_past)
        out_specs.append(pl.BlockSpec((tq, p_len), lambda b, i: (b * nq + i, 0)))
        out_shape.append(jax.ShapeDtypeStruct((t, p_len), F32))
        scratch.append(pltpu.VMEM((tq, p_len), jnp.int32))
    res = pl.pallas_call(
        functools.partial(_indexer_kernel, tq=tq, n=n, p_len=p_len, top_k=top_k),
        grid=(bsz, nq), in_specs=in_specs, out_specs=out_specs, out_shape=out_shape,
        scratch_shapes=scratch,
        compiler_params=_cparams(("arbitrary", "arbitrary")),
        name="indexer_past" if has_past else "indexer",
    )(*args)
    return res[0], (res[1] if has_past else None)


def _sparse_attn_kernel(*refs, tq, tkd, n, p_len):
    has_past = p_len > 0
    it = iter(refs)
    far_ref, q_ref, kvn_ref, seln_ref = (next(it) for _ in range(4))
    kvp_ref, selp_ref = (next(it), next(it)) if has_past else (None, None)
    bd_ref, bp_ref, o_ref = (next(it) for _ in range(3))
    nh = N_HEADS_C
    i = pl.program_id(1)
    q = q_ref[...].reshape(nh * tq, KV_LORA)

    def attend(regions):
        parts = []
        for kvb, sel, bias in regions:
            kvb = kvb.astype(MXU)
            w = kvb.shape[0]
            s = _dot_nt(q, kvb).reshape(nh, tq, w) * HEAD_DIM ** -0.5 + bias
            parts.append((jnp.where(sel[None] > 0.0, s, NEG_INF), kvb))
        m = None
        for s, _ in parts:
            mx = jnp.max(s, axis=-1, keepdims=True)
            m = mx if m is None else jnp.maximum(m, mx)
        l = None
        acc = None
        for s, kvb in parts:
            w = kvb.shape[0]
            p = jnp.exp(s - m)
            ps = jnp.sum(p, axis=-1, keepdims=True)
            pv = _dot(p.reshape(nh * tq, w).astype(MXU), kvb)
            l = ps if l is None else l + ps
            acc = pv if acc is None else acc + pv
        o_ref[...] = (acc.reshape(nh, tq, KV_LORA) / l).astype(o_ref.dtype)

    if has_past:
        regions = []
        if p_len > TK:
            regions.append((kvp_ref[0, :p_len - TK, :], selp_ref[:, :p_len - TK], far_ref[...]))
        regions.append((kvp_ref[0, p_len - TK:, :], selp_ref[:, p_len - TK:], bp_ref[0]))
        regions.append((kvn_ref[...], seln_ref[...], bd_ref[0]))
        attend(regions)
    else:
        def run(c):
            regions = []
            if c >= 2:
                regions.append((kvn_ref[:(c - 1) * TK, :], seln_ref[:, :(c - 1) * TK], far_ref[...]))
            if c >= 1:
                regions.append((kvn_ref[(c - 1) * TK:c * TK, :], seln_ref[:, (c - 1) * TK:c * TK], bp_ref[0]))
            regions.append((kvn_ref[c * TK:(c + 1) * TK, :], seln_ref[:, c * TK:(c + 1) * TK], bd_ref[0]))
            attend(regions)

        for c in range(n // TK):
            pl.when((i * tq) // TK == c)(functools.partial(run, c))


def _sparse_attn(q_lat, kv_lat, sel_new, kv_past, sel_past, table, *, bsz, n):
    has_past = kv_past is not None
    nh = N_HEADS_C
    t = bsz * n
    if has_past:
        tq, tkd = min(n, TQ_SPARSE_PAST), n
        p_len = kv_past.shape[1]
        assert p_len % TK == 0 and n <= TK
    else:
        tq, tkd = min(n, TQ_SPARSE), TK
        p_len = 0
        assert n % TK == 0
    q_offsets = list(range(0, tkd, tq))
    nq = n // tq
    npar = len(q_offsets)
    far = _bias_far(table, nh).reshape(nh, 1, 1)
    bd = _bias_tiles(table, q_offsets, 0, tq, tkd, nh)
    bp = _bias_tiles(table, q_offsets, -TK, tq, TK, nh)
    in_specs = [pl.BlockSpec((nh, 1, 1), lambda b, i: (0, 0, 0)),
                pl.BlockSpec((nh, tq, KV_LORA), lambda b, i: (0, b * nq + i, 0)),
                pl.BlockSpec((n, KV_LORA), lambda b, i: (b, 0)),
                pl.BlockSpec((tq, n), lambda b, i: (b * nq + i, 0))]
    args = [far, q_lat, kv_lat, sel_new]
    if has_past:
        in_specs += [pl.BlockSpec((1, p_len, KV_LORA), lambda b, i: (b, 0, 0)),
                     pl.BlockSpec((tq, p_len), lambda b, i: (b * nq + i, 0))]
        args += [kv_past, sel_past]
    in_specs += [pl.BlockSpec((1, nh, tq, tkd), lambda b, i: (i % npar, 0, 0, 0)),
                 pl.BlockSpec((1, nh, tq, TK), lambda b, i: (i % npar, 0, 0, 0))]
    args += [bd, bp]
    return pl.pallas_call(
        functools.partial(_sparse_attn_kernel, tq=tq, tkd=tkd, n=n, p_len=p_len),
        grid=(bsz, nq), in_specs=in_specs,
        out_specs=pl.BlockSpec((nh, tq, KV_LORA), lambda b, i: (0, b * nq + i, 0)),
        out_shape=jax.ShapeDtypeStruct((nh, t, KV_LORA), MXU),
        compiler_params=_cparams(("arbitrary", "arbitrary")),
        name="sparse_attn_past" if has_past else "sparse_attn",
    )(*args)


def _tile(n, cap, mult):
    best = None
    for t in range(mult, min(n, cap) + 1, mult):
        if n % t == 0:
            best = t
    assert best is not None, (n, cap, mult)
    return best


def _per_chunk(v, reps):
    return jnp.repeat(v, reps, axis=0)[:, None, :]


def _trunk(x, mod, past, p, w):
    bsz, n, d = x.shape
    t = bsz * n
    x = x.reshape(t, d)
    tm = _tile(t, 512, CHUNK)
    tmn = _tile(t, 256, CHUNK)
    tn_d = _tile(d, 1024, LANES)
    depth = p['w_ada'].shape[0]
    p_len = 0 if past is None else past[0].shape[2]
    top_k = min(TOPK_MAX, (p_len + n) // 4)
    even_rows, odd_rows = [], []
    for l in range(depth):
        m6 = mod[l].reshape(bsz, 6, d)
        sh1, sc1, g1, sh2, sc2, g2 = (_per_chunk(m6[:, k], n // CHUNK) for k in range(6))
        h = _norm(x, p['norm_mix_g'][l], tm=tmn, sc=sc1, sh=sh1, out_dtype=MXU)
        if l % 2 == 0:
            i = l // 2
            lam_init = 0.8 - 0.6 * math.exp(-0.3 * l)
            qkv = _mm(h, p['w_in_even'][i], n_out=48 * LANES, tm=tm, tn=1024, seg_out=True, name="mm_qkv")
            past_i = None if past is None else tuple(a[i] for a in past[:4])
            lam = (jnp.exp(jnp.sum(p['lam_q1'][i].astype(F32) * p['lam_k1'][i].astype(F32)))
                   - jnp.exp(jnp.sum(p['lam_q2'][i].astype(F32) * p['lam_k2'][i].astype(F32)))
                   + lam_init)
            if past is None:
                o = _even_attn(qkv, p['rel_bias_table'], lam, p['subln_g'][i], lam_init, bsz=bsz, n=n)
            else:
                o = _even_attn_past(qkv, past_i, p['rel_bias_table'], lam, p['subln_g'][i], lam_init,
                                    bsz=bsz, n=n)
            even_rows.append((qkv[1].reshape(bsz, n, N_MAPS_A, DA_QK),
                              qkv[2].reshape(bsz, n, N_HEADS_A, HEAD_DIM),
                              qkv[4].reshape(bsz, n, N_HEADS_B, HEAD_DIM),
                              qkv[5].reshape(bsz, n, N_HEADS_B, HEAD_DIM)))
            x = _mm(o, p['w_out_even'][i], n_out=d, tm=tm, tn=tn_d, gate=g1, res=x)
        else:
            j = l // 2
            proj = _mm(h, w['w_in_odd'][j], n_out=10 * LANES, tm=tm, tn=10 * LANES)
            c_q = _norm(proj, p['g_q'][j], tm=tmn, width=Q_LORA, col_block=0, out_dtype=MXU)
            kv_lat = _norm(proj, p['g_kv'][j], tm=tmn, width=KV_LORA, col_block=1)
            k_i = proj[:, 1024:1152]
            q = _mm(c_q, p['w_uq'][j], n_out=N_HEADS_C * HEAD_DIM, tm=tm, tn=1024, out_dtype=MXU)
            q_lat = _mm_heads_out(q, w['w_uk_t'][j], tm=_tile(t, TM_HEADS, CHUNK))
            q_i = _mm(c_q, p['w_qidx'][j], n_out=N_IDX_HEADS * D_IDX, tm=tm, tn=1024, out_dtype=MXU)
            kv_past = None if past is None else past[4][j]
            ki_past = None if past is None else past[5][j]
            sel_new, sel_past = _indexer(q_i, proj, ki_past, bsz=bsz, n=n, top_k=top_k)
            o_lat = _sparse_attn(q_lat, kv_lat, sel_new, kv_past, sel_past, p['rel_bias_table'],
                                 bsz=bsz, n=n)
            o = _mm_heads_in(o_lat, w['w_uv_h'][j], tm=_tile(t, TM_HEADS, CHUNK))
            odd_rows.append((kv_lat.reshape(bsz, n, KV_LORA), k_i.reshape(bsz, n, D_IDX)))
            x = _mm(o, p['w_out_odd'][j], n_out=d, tm=tm, tn=tn_d, gate=g1, res=x)
        h2, route = _norm_router(x, p['norm_ffn_g'][l], sc2, sh2, w['w_router'], p['router_bias'], tm=tmn)
        x = _moe(h2, route, w['w_gate'], w['w_up'], w['w_down'], l, g2, x, tm=tmn)
    y = _norm(x, p['final_norm_g'], tm=tmn).reshape(bsz, n, d)
    ev = tuple(jnp.stack([r[m] for r in even_rows]) for m in range(4))
    od = tuple(jnp.stack([r[m] for r in odd_rows]) for m in range(2))
    return y, ev, od


def kernel(x_prompt, x_sample, cache_a_k, cache_a_v, cache_b_k, cache_b_v, cache_c_kv, cache_c_idx,
           c_prompt, c_sample, rel_bias_table, norm_mix_g, norm_ffn_g, final_norm_g, w_ada, b_ada,
           w_in_even, lam_q1, lam_k1, lam_q2, lam_k2, subln_g, w_out_even, w_in_odd, g_q, g_kv,
           w_uq, w_qidx, w_uk, w_uv, w_out_odd, w_router, router_bias, w_gate, w_up, w_down):
    p = dict(rel_bias_table=rel_bias_table, norm_mix_g=norm_mix_g, norm_ffn_g=norm_ffn_g,
             final_norm_g=final_norm_g, w_ada=w_ada, b_ada=b_ada, w_in_even=w_in_even,
             lam_q1=lam_q1, lam_k1=lam_k1, lam_q2=lam_q2, lam_k2=lam_k2, subln_g=subln_g,
             w_out_even=w_out_even, g_q=g_q, g_kv=g_kv, w_uq=w_uq, w_qidx=w_qidx,
             w_out_odd=w_out_odd, router_bias=router_bias)
    d = x_prompt.shape[-1]
    depth = w_ada.shape[0]
    w = dict(
        w_in_odd=jnp.pad(w_in_odd, ((0, 0), (0, 0), (0, 10 * LANES - w_in_odd.shape[-1]))),
        w_uk_t=jnp.transpose(w_uk, (0, 2, 3, 1)),
        w_uv_h=jnp.transpose(w_uv, (0, 2, 1, 3)),
        w_router=jnp.pad(w_router, ((0, 0), (0, LANES - N_EXPERTS))),
        w_gate=w_gate.astype(MXU), w_up=w_up.astype(MXU), w_down=w_down.astype(MXU))
    nb_p, nb_s = c_prompt.shape[0], c_sample.shape[0]
    c_all = jnp.concatenate([c_prompt, c_sample], axis=0)
    rows = -(-(nb_p + nb_s) // 16) * 16
    c_act = jnp.pad(c_all * (1.0 / (1.0 + jnp.exp(-c_all))), ((0, rows - nb_p - nb_s), (0, 0)))
    tn_ada = _tile(6 * d, 1024, LANES)
    mods = [_mm(c_act, w_ada, layer=l, n_out=6 * d, tm=rows, tn=tn_ada, bias=b_ada[l], name="mm_ada")
            for l in range(depth)]
    mod_p = [m[:nb_p] for m in mods]
    mod_s = [m[nb_p:nb_p + nb_s] for m in mods]

    y_prompt, ev_p, od_p = _trunk(x_prompt, mod_p, None, p, w)
    past = (cache_a_k, cache_a_v, cache_b_k, cache_b_v, cache_c_kv, cache_c_idx)
    y_sample, ev_s, od_s = _trunk(x_sample, mod_s, past, p, w)
    return (y_prompt, y_sample) + ev_p + od_p + ev_s + od_s
```

```python
import functools
import math

import jax
import jax.numpy as jnp
import numpy as np
from jax import lax
from jax.experimental import pallas as pl
from jax.experimental.pallas import tpu as pltpu

F32 = jnp.float32
MXU = jnp.bfloat16

CHUNK = 64
HEAD_DIM = 128
N_HEADS_A = 8
N_MAPS_A = 16
DA_QK = 64
N_HEADS_B = 8
N_HEADS_C = 16
Q_LORA = 512
KV_LORA = 512
N_IDX_HEADS = 16
D_IDX = 128
TOPK_MAX = 256
N_BUCKETS = 32
MAX_DISTANCE = 128
N_EXPERTS = 16
N_GROUPS = 4
EXPERTS_PER_GROUP = 4
EPS = 1e-6
LANES = 128
TK = 256
TQ_SPARSE = 64
TQ_SPARSE_PAST = 32
TQ_INDEXER = 256
TKC = 512
TM_HEADS = 2048
VMEM_LIMIT = 56 * 1024 * 1024
NEG_INF = float("-inf")


def _cparams(sem):
    return pltpu.CompilerParams(dimension_semantics=sem, vmem_limit_bytes=VMEM_LIMIT)


def _dot(a, b):
    return jnp.dot(a, b, preferred_element_type=F32)


def _dot_nt(a, b):
    return lax.dot_general(a, b, (((1,), (1,)), ((), ())), preferred_element_type=F32)


def _mm_kernel(*refs, has_bias, has_res, tm):
    it = iter(refs)
    a_ref, b_ref = next(it), next(it)
    bias_ref = next(it) if has_bias else None
    gate_ref, res_ref = (next(it), next(it)) if has_res else (None, None)
    o_ref, bsc = next(it), next(it)

    @pl.when(pl.program_id(1) == 0)
    def _():
        bsc[...] = b_ref[...].reshape(bsc.shape).astype(bsc.dtype)

    a = a_ref[...]
    a = a.reshape(a.shape[-2:]).astype(MXU)
    acc = _dot(a, bsc[...])
    if has_bias:
        acc = acc + bias_ref[...]
    if has_res:
        tn = acc.shape[-1]
        acc = (acc.reshape(tm // CHUNK, CHUNK, tn) * gate_ref[...]).reshape(tm, tn)
        acc = acc + res_ref[...]
    o_ref[...] = acc.reshape(o_ref.shape).astype(o_ref.dtype)


def _mm(a, b, *, n_out, tm, tn, bias=None, gate=None, res=None, out_dtype=F32, seg_out=False, layer=0,
        name="mm"):
    m, k = a.shape
    assert m % tm == 0 and n_out % tn == 0 and b.shape[-2] == k
    b_spec = (pl.BlockSpec((k, tn), lambda j, i: (0, j)) if b.ndim == 2
              else pl.BlockSpec((1, k, tn), lambda j, i: (layer, 0, j)))
    in_specs = [pl.BlockSpec((tm, k), lambda j, i: (i, 0)), b_spec]
    args = [a, b]
    if bias is not None:
        in_specs.append(pl.BlockSpec((1, tn), lambda j, i: (0, j)))
        args.append(bias.reshape(1, n_out))
    if res is not None:
        in_specs.append(pl.BlockSpec((tm // CHUNK, 1, tn), lambda j, i: (i, 0, j)))
        in_specs.append(pl.BlockSpec((tm, tn), lambda j, i: (i, j)))
        args += [gate, res]
    return pl.pallas_call(
        functools.partial(_mm_kernel, has_bias=bias is not None, has_res=res is not None, tm=tm),
        grid=(n_out // tn, m // tm), in_specs=in_specs,
        out_specs=(pl.BlockSpec((1, tm, tn), lambda j, i: (j, i, 0)) if seg_out
                   else pl.BlockSpec((tm, tn), lambda j, i: (i, j))),
        out_shape=jax.ShapeDtypeStruct((n_out // tn, m, tn) if seg_out else (m, n_out), out_dtype),
        scratch_shapes=[pltpu.VMEM((k, tn), MXU)],
        compiler_params=_cparams(("arbitrary", "arbitrary")), name=name,
    )(*args)


def _mm_heads_out(a, b_h, *, tm):
    m = a.shape[0]
    nh, ka, n = b_h.shape
    return pl.pallas_call(
        functools.partial(_mm_kernel, has_bias=False, has_res=False, tm=tm),
        grid=(nh, m // tm),
        in_specs=[pl.BlockSpec((tm, ka), lambda h, i: (i, h)),
                  pl.BlockSpec((1, ka, n), lambda h, i: (h, 0, 0))],
        out_specs=pl.BlockSpec((1, tm, n), lambda h, i: (h, i, 0)),
        out_shape=jax.ShapeDtypeStruct((nh, m, n), MXU),
        scratch_shapes=[pltpu.VMEM((ka, n), MXU)],
        compiler_params=_cparams(("arbitrary", "arbitrary")), name="mm_heads_out",
    )(a, b_h)


def _mm_heads_in(a_h, b_h, *, tm):
    nh, m, ka = a_h.shape
    n = b_h.shape[2]
    return pl.pallas_call(
        functools.partial(_mm_kernel, has_bias=False, has_res=False, tm=tm),
        grid=(nh, m // tm),
        in_specs=[pl.BlockSpec((1, tm, ka), lambda h, i: (h, i, 0)),
                  pl.BlockSpec((1, ka, n), lambda h, i: (h, 0, 0))],
        out_specs=pl.BlockSpec((tm, n), lambda h, i: (i, h)),
        out_shape=jax.ShapeDtypeStruct((m, nh * n), MXU),
        scratch_shapes=[pltpu.VMEM((ka, n), MXU)],
        compiler_params=_cparams(("arbitrary", "arbitrary")), name="mm_heads_in",
    )(a_h, b_h)


def _rms(x, g):
    return x * lax.rsqrt(jnp.mean(x * x, axis=-1, keepdims=True) + EPS) * g


def _modulate(y, sc_ref, sh_ref, tm):
    d = y.shape[-1]
    y3 = y.reshape(tm // CHUNK, CHUNK, d)
    y3 = y3 * (1.0 + sc_ref[...]) + sh_ref[...]
    return y3.reshape(tm, d)


def _norm_kernel(*refs, has_mod, tm):
    if has_mod:
        x_ref, g_ref, sc_ref, sh_ref, o_ref = refs
    else:
        x_ref, g_ref, o_ref = refs
    y = _rms(x_ref[...], g_ref[...])
    if has_mod:
        y = _modulate(y, sc_ref, sh_ref, tm)
    o_ref[...] = y.astype(o_ref.dtype)


def _norm(x, g, *, tm, width=None, col_block=0, sc=None, sh=None, out_dtype=F32):
    width = x.shape[1] if width is None else width
    rows = x.shape[0]
    assert rows % tm == 0
    has_mod = sc is not None
    in_specs = [pl.BlockSpec((tm, width), lambda i: (i, col_block)),
                pl.BlockSpec((1, width), lambda i: (0, 0))]
    args = [x, g.reshape(1, width)]
    if has_mod:
        in_specs += [pl.BlockSpec((tm // CHUNK, 1, width), lambda i: (i, 0, 0))] * 2
        args += [sc, sh]
    return pl.pallas_call(
        functools.partial(_norm_kernel, has_mod=has_mod, tm=tm),
        grid=(rows // tm,), in_specs=in_specs,
        out_specs=pl.BlockSpec((tm, width), lambda i: (i, 0)),
        out_shape=jax.ShapeDtypeStruct((rows, width), out_dtype),
        compiler_params=_cparams(("arbitrary",)), name="norm",
    )(*args)


def _split2(x):
    hi = x.astype(MXU)
    lo = (x - hi.astype(F32)).astype(MXU)
    return hi, lo


def _route(logits, rb_ref):
    tm = logits.shape[0]
    lt = logits.T
    aff = [1.0 / (1.0 + jnp.exp(-lt[e:e + 1, :])) for e in range(N_EXPERTS)]
    sc = [aff[e] + rb_ref[e] for e in range(N_EXPERTS)]
    npg = EXPERTS_PER_GROUP
    gscore = []
    for g in range(N_GROUPS):
        v = sc[g * npg:(g + 1) * npg]
        best = None
        for a in range(npg):
            for b in range(a + 1, npg):
                s = v[a] + v[b]
                best = s if best is None else jnp.maximum(best, s)
        gscore.append(best)
    gb = jnp.zeros((1, tm), jnp.int32)
    gv = gscore[0]
    for g in range(1, N_GROUPS):
        better = gscore[g] > gv
        gb = jnp.where(better, g, gb)
        gv = jnp.where(better, gscore[g], gv)
    u = [sc[j] for j in range(npg)]
    a4 = [aff[j] for j in range(npg)]
    for g in range(1, N_GROUPS):
        pick = gb == g
        u = [jnp.where(pick, sc[g * npg + j], u[j]) for j in range(npg)]
        a4 = [jnp.where(pick, aff[g * npg + j], a4[j]) for j in range(npg)]
    i1 = jnp.zeros((1, tm), jnp.int32)
    v1 = u[0]
    for j in range(1, npg):
        better = u[j] > v1
        i1 = jnp.where(better, j, i1)
        v1 = jnp.where(better, u[j], v1)
    i2 = jnp.full((1, tm), -1, jnp.int32)
    v2 = jnp.full((1, tm), NEG_INF, F32)
    for j in range(npg):
        better = (i1 != j) & ((u[j] > v2) | (i2 < 0))
        i2 = jnp.where(better, j, i2)
        v2 = jnp.where(better, u[j], v2)
    w1 = a4[0]
    w2 = a4[0]
    for j in range(1, npg):
        w1 = jnp.where(i1 == j, a4[j], w1)
        w2 = jnp.where(i2 == j, a4[j], w2)
    tot = w1 + w2
    w1 = w1 / tot
    w2 = w2 / tot
    e1 = gb * npg + i1
    e2 = gb * npg + i2
    rows = jnp.concatenate([e1.astype(F32), e2.astype(F32), w1, w2, jnp.zeros((LANES - 4, tm), F32)], axis=0)
    return rows.T


def _norm_router_kernel(x_ref, g_ref, sc_ref, sh_ref, wr_ref, rb_ref, h_ref, route_ref, *, tm):
    h = _modulate(_rms(x_ref[...], g_ref[...]), sc_ref, sh_ref, tm)
    h_ref[...] = h
    hh, hl = _split2(h)
    wh, wl = _split2(wr_ref[...])
    logits = _dot(hh, wh) + (_dot(hl, wh) + _dot(hh, wl))
    route_ref[...] = _route(logits, rb_ref)


def _norm_router(x, g, sc, sh, w_router_pad, router_bias, *, tm):
    t, d = x.shape
    return pl.pallas_call(
        functools.partial(_norm_router_kernel, tm=tm),
        grid=(t // tm,),
        in_specs=[pl.BlockSpec((tm, d), lambda i: (i, 0)),
                  pl.BlockSpec((1, d), lambda i: (0, 0)),
                  pl.BlockSpec((tm // CHUNK, 1, d), lambda i: (i, 0, 0)),
                  pl.BlockSpec((tm // CHUNK, 1, d), lambda i: (i, 0, 0)),
                  pl.BlockSpec((d, LANES), lambda i: (0, 0)),
                  pl.BlockSpec(memory_space=pltpu.SMEM)],
        out_specs=[pl.BlockSpec((tm, d), lambda i: (i, 0)),
                   pl.BlockSpec((tm, LANES), lambda i: (i, 0))],
        out_shape=[jax.ShapeDtypeStruct((t, d), F32), jax.ShapeDtypeStruct((t, LANES), F32)],
        compiler_params=_cparams(("arbitrary",)), name="norm_router",
    )(x, g.reshape(1, d), sc, sh, w_router_pad, router_bias)


TM_EXPERT = 256
TM_DISPATCH = 256


def _row_copy(src, src_row, dst, dst_row, sem):
    return pltpu.make_async_copy(src.at[pl.ds(src_row, 1), :], dst.at[pl.ds(dst_row, 1), :], sem)


def _rows_wait(src, dst, sem, n):
    pltpu.make_async_copy(src.at[pl.ds(0, n), :], dst.at[pl.ds(0, n), :], sem).wait()


def _dispatch_kernel(dest_ref, pad_ref, h_ref, xs_hbm, sems, *, tb, t, n_pad):
    i = pl.program_id(0)
    sem = sems.at[0]

    def body(r, c):
        tok = i * tb + r
        _row_copy(h_ref, r, xs_hbm, dest_ref[tok], sem).start()
        _row_copy(h_ref, r, xs_hbm, dest_ref[t + tok], sem).start()
        return c

    lax.fori_loop(0, tb, body, 0, unroll=8)
    for _ in range(2):
        _rows_wait(h_ref, xs_hbm, sem, tb)

    @pl.when(i == 0)
    def _():
        for c0 in range(0, n_pad, tb):
            nc = min(tb, n_pad - c0)

            def pad_body(r, c, c0=c0):
                _row_copy(h_ref, 0, xs_hbm, pad_ref[c0 + r], sem).start()
                return c

            lax.fori_loop(0, nc, pad_body, 0, unroll=8)
            _rows_wait(h_ref, xs_hbm, sem, nc)


def _expert_kernel(te_ref, nv_ref, x_ref, wg_ref, wu_ref, wd_ref, y_ref):
    del te_ref
    i = pl.program_id(0)

    @pl.when(i < nv_ref[0])
    def _():
        x = x_ref[...].astype(MXU)
        a = _dot(x, wg_ref[0, 0])
        b = _dot(x, wu_ref[0, 0])
        he = (a / (1.0 + jnp.exp(-a))) * b
        y_ref[...] = _dot(he.astype(MXU), wd_ref[0, 0])

    @pl.when(i >= nv_ref[0])
    def _():
        y_ref[...] = jnp.zeros_like(y_ref)


def _combine_kernel(dest_ref, y_hbm, route_ref, res_ref, g2_ref, o_ref, ybuf, sem, *, tm, t):
    i = pl.program_id(0)
    n = pl.num_programs(0)
    slot = i % 2

    def start(step, s):
        for j in range(2):
            def body(r, c):
                _row_copy(y_hbm, dest_ref[j * t + step * tm + r], ybuf.at[s, j], r, sem.at[s]).start()
                return c

            lax.fori_loop(0, tm, body, 0, unroll=8)

    @pl.when(i == 0)
    def _():
        start(0, 0)

    @pl.when(i + 1 < n)
    def _():
        start(i + 1, 1 - slot)

    for j in range(2):
        _rows_wait(y_hbm, ybuf.at[slot, j], sem.at[slot], tm)
    d = o_ref.shape[-1]
    route = route_ref[...]
    y = route[:, 2:3] * ybuf[slot, 0] + route[:, 3:4] * ybuf[slot, 1]
    o_ref[...] = res_ref[...] + (y.reshape(tm // CHUNK, CHUNK, d) * g2_ref[...]).reshape(tm, d)


def _cumsum_rows(x, blk):
    n, e = x.shape
    xb = x.reshape(n // blk, blk, e)
    tri = jnp.asarray(np.tril(np.ones((blk, blk), np.float32)))
    local = jnp.einsum('ij,bje->bie', tri, xb, precision=lax.Precision.HIGHEST)
    tot = local[:, -1, :]
    return (local + (jnp.cumsum(tot, axis=0) - tot)[:, None, :]).reshape(n, e)


def _moe(h, route, w_gate, w_up, w_down, layer, g2, res, *, tm):
    t, d = h.shape
    de = w_gate.shape[-1]
    te = _tile(2 * t, TM_EXPERT, CHUNK)
    n_tiles = 2 * t // te + N_EXPERTS
    r_pad = n_tiles * te
    n_pad = r_pad - 2 * t
    e_flat = route[:, 0:2].T.reshape(-1)
    experts = jnp.arange(N_EXPERTS, dtype=F32)
    onehot = (e_flat[:, None] == experts[None, :]).astype(F32)
    csum = _cumsum_rows(onehot, _tile(2 * t, 256, 8))
    counts = csum[-1]
    tiles_e = jnp.ceil(counts / te)
    tile_end = jnp.cumsum(tiles_e)
    seg_start = (tile_end - tiles_e) * te
    dest = jnp.sum(onehot * (seg_start[None, :] + csum - 1.0), axis=1).astype(jnp.int32)
    gap_start = jnp.concatenate([seg_start + counts, tile_end[-1:] * te])
    gap_len = jnp.concatenate([tiles_e * te - counts, r_pad - tile_end[-1:] * te])
    gap_first = jnp.cumsum(gap_len) - gap_len
    k = jnp.arange(n_pad, dtype=F32)[:, None]
    in_gap = (k >= gap_first[None, :]) & (k < (gap_first + gap_len)[None, :])
    pad_rows = jnp.sum(jnp.where(in_gap, gap_start[None, :] + k - gap_first[None, :], 0.0),
                       axis=1).astype(jnp.int32)
    tile_id = jnp.arange(n_tiles, dtype=F32)[:, None]
    tile_e = jnp.minimum(jnp.sum((tile_end[None, :] <= tile_id).astype(jnp.int32), axis=1),
                         N_EXPERTS - 1)
    n_valid = tile_end[-1:].astype(jnp.int32)

    tb = _tile(t, TM_DISPATCH, CHUNK)
    xs = pl.pallas_call(
        functools.partial(_dispatch_kernel, tb=tb, t=t, n_pad=n_pad),
        grid_spec=pltpu.PrefetchScalarGridSpec(
            num_scalar_prefetch=2, grid=(t // tb,),
            in_specs=[pl.BlockSpec((tb, d), lambda i, dst, pad: (i, 0))],
            out_specs=pl.BlockSpec(memory_space=pl.ANY),
            scratch_shapes=[pltpu.SemaphoreType.DMA((1,))]),
        out_shape=jax.ShapeDtypeStruct((r_pad, d), F32),
        compiler_params=_cparams(("arbitrary",)), name="moe_dispatch",
    )(dest, pad_rows, h)

    y = pl.pallas_call(
        _expert_kernel,
        grid_spec=pltpu.PrefetchScalarGridSpec(
            num_scalar_prefetch=2, grid=(n_tiles,),
            in_specs=[pl.BlockSpec((te, d), lambda i, e, n: (i, 0)),
                      pl.BlockSpec((1, 1, d, de), lambda i, e, n: (layer, e[i], 0, 0)),
                      pl.BlockSpec((1, 1, d, de), lambda i, e, n: (layer, e[i], 0, 0)),
                      pl.BlockSpec((1, 1, de, d), lambda i, e, n: (layer, e[i], 0, 0))],
            out_specs=pl.BlockSpec((te, d), lambda i, e, n: (i, 0))),
        out_shape=jax.ShapeDtypeStruct((r_pad, d), F32),
        compiler_params=_cparams(("arbitrary",)), name="moe_experts",
    )(tile_e, n_valid, xs, w_gate, w_up, w_down)

    return pl.pallas_call(
        functools.partial(_combine_kernel, tm=tm, t=t),
        grid_spec=pltpu.PrefetchScalarGridSpec(
            num_scalar_prefetch=1, grid=(t // tm,),
            in_specs=[pl.BlockSpec(memory_space=pl.ANY),
                      pl.BlockSpec((tm, LANES), lambda i, dst: (i, 0)),
                      pl.BlockSpec((tm, d), lambda i, dst: (i, 0)),
                      pl.BlockSpec((tm // CHUNK, 1, d), lambda i, dst: (i, 0, 0))],
            out_specs=pl.BlockSpec((tm, d), lambda i, dst: (i, 0)),
            scratch_shapes=[pltpu.VMEM((2, 2, tm, d), F32), pltpu.SemaphoreType.DMA((2,))]),
        out_shape=jax.ShapeDtypeStruct((t, d), F32),
        compiler_params=_cparams(("arbitrary",)), name="moe_combine",
    )(dest, y, route, res, g2)


def _rel_bucket(rel):
    half = N_BUCKETS // 2
    max_exact = half // 2
    n = jnp.abs(rel)
    nf = jnp.maximum(n, 1).astype(F32)
    large = max_exact + (jnp.log(nf / max_exact) / math.log(MAX_DISTANCE / max_exact)
                         * (half - max_exact)).astype(jnp.int32)
    large = jnp.minimum(large, half - 1)
    return jnp.where(rel > 0, half, 0) + jnp.where(n < max_exact, n, large)


def _bias_tiles(table, q_offsets, k_offset, tq, tk, n_heads):
    tiles = []
    for q0 in q_offsets:
        rel = (k_offset + np.arange(tk))[None, :] - (q0 + np.arange(tq))[:, None]
        bucket = _rel_bucket(jnp.asarray(rel, jnp.int32))
        onehot = (bucket[..., None] == jnp.arange(N_BUCKETS, dtype=jnp.int32)).astype(F32)
        tiles.append(jnp.einsum('qkb,bh->hqk', onehot, table[:, :n_heads].astype(F32),
                                precision=lax.Precision.HIGHEST))
    return jnp.stack(tiles)


def _bias_far(table, n_heads):
    rel = jnp.full((1,), -(MAX_DISTANCE + 1), jnp.int32)
    return table[_rel_bucket(rel)][0, :n_heads].astype(F32)


def _diff_attend(qr, far, prev, diag, bias_far, bp_ref, bd_ref, vis):
    parts = [[], []]
    for r in range(2):
        cols = slice(r * DA_QK, (r + 1) * DA_QK)
        if far is not None:
            parts[r].append(_dot_nt(qr[r], far[0][:, cols].astype(MXU)) + bias_far[r])
        if prev is not None:
            parts[r].append(_dot_nt(qr[r], prev[0][:, cols].astype(MXU)) + bp_ref[0, r])
        s = _dot_nt(qr[r], diag[0][:, cols].astype(MXU)) + bd_ref[0, r]
        parts[r].append(jnp.where(vis, s, NEG_INF))
    values = [reg[1].astype(MXU) for reg in (far, prev, diag) if reg is not None]
    m = []
    for r in range(2):
        mx = None
        for s in parts[r]:
            pm = jnp.max(s, axis=-1, keepdims=True)
            mx = pm if mx is None else jnp.maximum(mx, pm)
        m.append(mx)
    outs = []
    for r in range(2):
        l = None
        acc = None
        for s, v in zip(parts[r], values):
            p = jnp.exp(s - m[r])
            ps = jnp.sum(p, axis=-1, keepdims=True)
            pv = _dot(p.astype(MXU), v)
            l = ps if l is None else l + ps
            acc = pv if acc is None else acc + pv
        outs.append(acc / l)
    return outs


def _stick_attend(qbs, blocks, tri_ref, trid_ref, carries):
    nh = len(qbs)
    pairs = [(h, j) for h in range(nh) for j in range(len(blocks[h]))]
    log_beta, log_keep, later, total = {}, {}, {}, {}
    for h, j in pairs:
        kb, _, mask = blocks[h][j]
        z = _dot_nt(qbs[h], kb.astype(MXU)) * HEAD_DIM ** -0.5
        log_beta[h, j] = jnp.minimum(z, 0.0) - jnp.log(1.0 + jnp.exp(-jnp.abs(z)))
        lk = log_beta[h, j] - z
        log_keep[h, j] = lk if mask is None else jnp.where(mask, lk, 0.0)
    for h, j in pairs:
        lk = log_keep[h, j]
        tri = (trid_ref if lk.shape[1] != TK else tri_ref)[...]
        hi, lo = _split2(lk)
        later[h, j] = _dot(hi, tri) + _dot(lo, tri)
        total[h, j] = jnp.broadcast_to(later[h, j][:, 0:1] + lk[:, 0:1], (lk.shape[0], LANES))
    accs, new_carries = [], []
    for h in range(nh):
        carry = carries[h]
        acc = None
        for j, (_, vb, mask) in enumerate(blocks[h]):
            lt = later[h, j]
            tk = lt.shape[1]
            if carry is not None:
                lt = lt + (carry[:, :tk] if tk <= LANES else jnp.concatenate([carry] * (tk // LANES), axis=1))
            a = jnp.exp(log_beta[h, j] + lt)
            if mask is not None:
                a = jnp.where(mask, a, 0.0)
            pv = _dot(a.astype(MXU), vb.astype(MXU))
            acc = pv if acc is None else acc + pv
            carry = total[h, j] if carry is None else carry + total[h, j]
        accs.append(acc)
        new_carries.append(carry)
    return accs, new_carries


def _even_attn_kernel(far_ref, lam_ref, q_ref, kn_ref, vn_ref, bd_ref, bp_ref, g_ref, tri_ref, o_ref,
                      *, tq, nq, lam_init):
    u = pl.program_id(1)
    i = pl.program_id(2)
    q = q_ref[0]
    qq = lax.broadcasted_iota(jnp.int32, (tq, TK), 0)
    kk = lax.broadcasted_iota(jnp.int32, (tq, TK), 1)

    def regions(c):
        far = (0, (c - 1) * TK) if c >= 2 else None
        prev = ((c - 1) * TK, c * TK) if c >= 1 else None
        return far, prev, (c * TK, (c + 1) * TK)

    def load(reg, lo=None, hi=None):
        a, b = reg
        lo, hi = (a, b) if lo is None else (a + lo, a + hi)
        return kn_ref[0, lo:hi, :], vn_ref[0, lo:hi, :]

    def diff(c):
        far, prev, diag = regions(c)
        qr = [(q[:, r * DA_QK:(r + 1) * DA_QK] * DA_QK ** -0.5).astype(MXU) for r in range(2)]
        vis = (kk // CHUNK) <= (qq // CHUNK)
        o0, o1 = _diff_attend(qr, None if far is None else load(far), None if prev is None else load(prev),
                              load(diag), [far_ref[2 * u], far_ref[2 * u + 1]], bp_ref, bd_ref, vis)
        o = _rms(o0 - lam_ref[0] * o1, g_ref[...]) * (1.0 - lam_init)
        o_ref[...] = o.astype(o_ref.dtype)

    def stick(c):
        nb = c + 1
        k, v = load((0, nb * TK))
        z = _dot_nt(q.astype(MXU), k.astype(MXU)) * HEAD_DIM ** -0.5
        log_beta = jnp.minimum(z, 0.0) - jnp.log(1.0 + jnp.exp(-jnp.abs(z)))
        log_keep = log_beta - z
        earlier = kk < qq
        tri = tri_ref[...]
        later, total = [], []
        for j in range(nb):
            lk = log_keep[:, j * TK:(j + 1) * TK]
            if j == nb - 1:
                lk = jnp.where(earlier, lk, 0.0)
            hi, lo = _split2(lk)
            lt = _dot(hi, tri) + _dot(lo, tri)
            later.append(lt)
            total.append(jnp.broadcast_to(lt[:, 0:1] + lk[:, 0:1], (tq, LANES)))
        carry = None
        for j in range(nb - 1, -1, -1):
            if carry is not None:
                later[j] = later[j] + jnp.concatenate([carry] * (TK // LANES), axis=1)
            carry = total[j] if carry is None else carry + total[j]
        a = [jnp.exp(log_beta[:, j * TK:(j + 1) * TK] + later[j]) for j in range(nb)]
        a[-1] = jnp.where(earlier, a[-1], 0.0)
        a = a[0] if nb == 1 else jnp.concatenate(a, axis=1)
        o_ref[...] = _dot(a.astype(MXU), v.astype(MXU)).astype(o_ref.dtype)

    for c in range(nq):
        pl.when((u < N_HEADS_A) & (i == c))(functools.partial(diff, c))
        pl.when((u >= N_HEADS_A) & (i == c))(functools.partial(stick, c))


def _tri_later(tk):
    return jnp.asarray(np.arange(tk)[:, None] > np.arange(tk)[None, :], MXU)


def _even_attn(qkv, table, lam, subln_g, lam_init, *, bsz, n):
    tq = TK
    assert n % tq == 0
    nq = n // tq
    far = _bias_far(table, N_MAPS_A)
    bd = _bias_tiles(table, [0], 0, tq, TK, N_MAPS_A)
    bp = _bias_tiles(table, [0], -TK, tq, TK, N_MAPS_A)

    def seg(u, base):
        return jnp.where(u < 8, base, base + 3)

    smem = pl.BlockSpec(memory_space=pltpu.SMEM)
    in_specs = [smem, smem,
                pl.BlockSpec((1, tq, HEAD_DIM), lambda b, u, i: (seg(u, 0), b * nq + i, u % 8)),
                pl.BlockSpec((1, n, HEAD_DIM), lambda b, u, i: (seg(u, 1), b, u % 8)),
                pl.BlockSpec((1, n, HEAD_DIM), lambda b, u, i: (seg(u, 2), b, u % 8)),
                pl.BlockSpec((1, 2, tq, TK), lambda b, u, i: (0, jnp.minimum(u, 7), 0, 0)),
                pl.BlockSpec((1, 2, tq, TK), lambda b, u, i: (0, jnp.minimum(u, 7), 0, 0)),
                pl.BlockSpec((1, HEAD_DIM), lambda b, u, i: (0, 0)),
                pl.BlockSpec((TK, TK), lambda b, u, i: (0, 0))]
    return pl.pallas_call(
        functools.partial(_even_attn_kernel, tq=tq, nq=nq, lam_init=lam_init),
        grid=(bsz, 16, nq), in_specs=in_specs,
        out_specs=pl.BlockSpec((tq, HEAD_DIM), lambda b, u, i: (b * nq + i, u)),
        out_shape=jax.ShapeDtypeStruct((bsz * n, 16 * HEAD_DIM), MXU),
        compiler_params=_cparams(("arbitrary", "arbitrary", "arbitrary")), name="even_attn",
    )(far, lam.reshape(1), qkv, qkv, qkv, bd, bp, subln_g.reshape(1, HEAD_DIM), _tri_later(TK))


def _even_attn_past_kernel(lam_ref, q_ref, kn_ref, vn_ref, ak_hbm, av_hbm, bk_hbm, bv_hbm, bd_ref, bc_ref,
                           g_ref, tri_ref, trid_ref, o_ref, m_ref, l_ref, acc_ref, carry_ref,
                           ka_buf, kb_buf, v_buf, sem, *, n, tkc, nkc, lam_init):
    b = pl.program_id(0)
    mix = pl.program_id(1)
    kc = pl.program_id(2)
    step = (b * 2 + mix) * nkc + kc
    n_steps = pl.num_programs(0) * 2 * nkc
    slot = step % 2
    qq = lax.broadcasted_iota(jnp.int32, (n, n), 0)
    kk = lax.broadcasted_iota(jnp.int32, (n, n), 1)

    def chunk_copies(bb, mm_is_stick, cc, s):
        if mm_is_stick:
            rows = pl.ds(pl.multiple_of((nkc - 1 - cc) * tkc, tkc), tkc)
            return ([pltpu.make_async_copy(bk_hbm.at[bb, rows, h, :], kb_buf.at[s, h], sem.at[s])
                     for h in range(N_HEADS_B)] +
                    [pltpu.make_async_copy(bv_hbm.at[bb, rows, h, :], v_buf.at[s, h], sem.at[s])
                     for h in range(N_HEADS_B)])
        rows = pl.ds(pl.multiple_of(cc * tkc, tkc), tkc)
        return ([pltpu.make_async_copy(ak_hbm.at[bb, rows, pl.ds(h * HEAD_DIM, HEAD_DIM)], ka_buf.at[s, h],
                                       sem.at[s]) for h in range(N_HEADS_A)] +
                [pltpu.make_async_copy(av_hbm.at[bb, rows, h, :], v_buf.at[s, h], sem.at[s])
                 for h in range(N_HEADS_A)])

    def start(bb, mm, cc, s):
        for stick in (False, True):
            @pl.when(mm == int(stick))
            def _(stick=stick):
                for cp in chunk_copies(bb, stick, cc, s):
                    cp.start()

    @pl.when(step == 0)
    def _():
        start(b, mix, kc, slot)

    @pl.when(step + 1 < n_steps)
    def _():
        nxt = step + 1
        start(nxt // (2 * nkc), (nxt // nkc) % 2, nxt % nkc, 1 - slot)

    for stick in (False, True):
        @pl.when(mix == int(stick))
        def _(stick=stick):
            for cp in chunk_copies(b, stick, kc, slot):
                cp.wait()

    def head(ref, h, width=HEAD_DIM, off=0):
        return ref[0, :, h * HEAD_DIM + off:h * HEAD_DIM + off + width]

    def diff_update(s, vs, first):
        mx = jnp.max(s, axis=-1, keepdims=True)
        m_new = mx if first else jnp.maximum(m_ref[...], mx)
        p = jnp.exp(s - m_new)
        ps = jnp.sum(p, axis=-1, keepdims=True)
        pv = jnp.stack([_dot(p[mp].astype(MXU), vs[mp // 2].astype(MXU)) for mp in range(N_MAPS_A)])
        if first:
            l_ref[...] = ps
            acc_ref[...] = pv
        else:
            alpha = jnp.exp(m_ref[...] - m_new)
            l_ref[...] = alpha * l_ref[...] + ps
            acc_ref[...] = alpha * acc_ref[...] + pv
        m_ref[...] = m_new

    def diff_q(mp):
        h, r = divmod(mp, 2)
        return (head(q_ref, h, DA_QK, r * DA_QK) * DA_QK ** -0.5).astype(MXU)

    @pl.when((mix == 0) & (kc == 0))
    def _():
        vis = (kk // CHUNK) <= (qq // CHUNK)
        s = jnp.stack([_dot_nt(diff_q(mp), head(kn_ref, mp // 2, DA_QK, (mp % 2) * DA_QK).astype(MXU))
                       for mp in range(N_MAPS_A)]) + bd_ref[0]
        diff_update(jnp.where(vis[None], s, NEG_INF), [head(vn_ref, h) for h in range(N_HEADS_A)], True)

    @pl.when(mix == 0)
    def _():
        kh = [ka_buf[slot, h].astype(MXU) for h in range(N_HEADS_A)]
        s = jnp.stack([_dot_nt(diff_q(mp), kh[mp // 2][:, (mp % 2) * DA_QK:(mp % 2 + 1) * DA_QK])
                       for mp in range(N_MAPS_A)]) + bc_ref[0]
        diff_update(s, [v_buf[slot, h] for h in range(N_HEADS_A)], False)

    @pl.when((mix == 0) & (kc == nkc - 1))
    def _():
        for h in range(N_HEADS_A):
            o = acc_ref[2 * h] / l_ref[2 * h] - lam_ref[0] * (acc_ref[2 * h + 1] / l_ref[2 * h + 1])
            o = _rms(o, g_ref[...]) * (1.0 - lam_init)
            o_ref[:, h * HEAD_DIM:(h + 1) * HEAD_DIM] = o.astype(o_ref.dtype)

    def stick_q(h):
        return head(q_ref, h).astype(MXU)

    @pl.when((mix == 1) & (kc == 0))
    def _():
        heads = range(N_HEADS_B)
        accs, carries = _stick_attend([stick_q(h) for h in heads],
                                      [[(head(kn_ref, h), head(vn_ref, h), kk < qq)] for h in heads],
                                      tri_ref, trid_ref, [None] * N_HEADS_B)
        acc_ref[:N_HEADS_B] = jnp.stack(accs)
        carry_ref[...] = jnp.stack(carries)

    @pl.when(mix == 1)
    def _():
        carry = carry_ref[...]
        heads = range(N_HEADS_B)
        blocks = [[(kb_buf[slot, h, j * TK:(j + 1) * TK, :], v_buf[slot, h, j * TK:(j + 1) * TK, :], None)
                   for j in range(tkc // TK - 1, -1, -1)] for h in heads]
        accs, carries = _stick_attend([stick_q(h) for h in heads], blocks, tri_ref, trid_ref,
                                      [carry[h] for h in heads])
        acc_ref[:N_HEADS_B] = acc_ref[:N_HEADS_B] + jnp.stack(accs)
        carry_ref[...] = jnp.stack(carries)

    @pl.when((mix == 1) & (kc == nkc - 1))
    def _():
        for h in range(N_HEADS_B):
            o_ref[:, h * HEAD_DIM:(h + 1) * HEAD_DIM] = acc_ref[h].astype(o_ref.dtype)


def _even_attn_past(qkv, past, table, lam, subln_g, lam_init, *, bsz, n):
    ak, av, bk, bv = past
    p_len = ak.shape[1]
    ak = ak.reshape(bsz, p_len, N_MAPS_A * DA_QK)
    tkc = _tile(p_len, TKC, TK)
    nkc = p_len // tkc
    assert n <= CHUNK and n % 8 == 0
    bd = _bias_tiles(table, [0], 0, n, n, N_MAPS_A)
    far = jnp.broadcast_to(_bias_far(table, N_MAPS_A)[:, None, None], (N_MAPS_A, n, tkc))
    bc = jnp.stack([far, _bias_tiles(table, [0], -tkc, n, tkc, N_MAPS_A)[0]])

    def row(s):
        return lambda b, mix, kc: (3 * mix + s, b, 0)

    width = N_HEADS_A * HEAD_DIM
    in_specs = [pl.BlockSpec(memory_space=pltpu.SMEM),
                pl.BlockSpec((1, n, width), row(0)),
                pl.BlockSpec((1, n, width), row(1)),
                pl.BlockSpec((1, n, width), row(2)),
                pl.BlockSpec(memory_space=pl.ANY), pl.BlockSpec(memory_space=pl.ANY),
                pl.BlockSpec(memory_space=pl.ANY), pl.BlockSpec(memory_space=pl.ANY),
                pl.BlockSpec((1, N_MAPS_A, n, n), lambda b, mix, kc: (0, 0, 0, 0)),
                pl.BlockSpec((1, N_MAPS_A, n, tkc), lambda b, mix, kc: ((kc + 1) // nkc, 0, 0, 0)),
                pl.BlockSpec((1, HEAD_DIM), lambda b, mix, kc: (0, 0)),
                pl.BlockSpec((TK, TK), lambda b, mix, kc: (0, 0)),
                pl.BlockSpec((n, n), lambda b, mix, kc: (0, 0))]
    return pl.pallas_call(
        functools.partial(_even_attn_past_kernel, n=n, tkc=tkc, nkc=nkc, lam_init=lam_init),
        grid=(bsz, 2, nkc), in_specs=in_specs,
        out_specs=pl.BlockSpec((n, width), lambda b, mix, kc: (b, mix)),
        out_shape=jax.ShapeDtypeStruct((bsz * n, 2 * width), MXU),
        scratch_shapes=[pltpu.VMEM((N_MAPS_A, n, 1), F32), pltpu.VMEM((N_MAPS_A, n, 1), F32),
                        pltpu.VMEM((N_MAPS_A, n, HEAD_DIM), F32), pltpu.VMEM((N_HEADS_B, n, LANES), F32),
                        pltpu.VMEM((2, N_HEADS_A, tkc, HEAD_DIM), F32), pltpu.VMEM((2, N_HEADS_B, tkc, HEAD_DIM), F32),
                        pltpu.VMEM((2, N_HEADS_A, tkc, HEAD_DIM), F32), pltpu.SemaphoreType.DMA((2,))],
        compiler_params=_cparams(("arbitrary", "arbitrary", "arbitrary")), name="even_attn_past",
    )(lam.reshape(1), qkv, qkv, qkv, ak, av, bk, bv, bd, bc, subln_g.reshape(1, HEAD_DIM),
      _tri_later(TK), _tri_later(n))


INT_MIN = int(np.iinfo(np.int32).min)
NEG_INF_KEY = int(np.array(-np.inf, np.float32).view(np.int32)) ^ 0x7FFFFFFF
INDEX_BITS = 15


def _sort_key(x):
    b = lax.bitcast_convert_type(x, jnp.int32)
    return b ^ ((b >> 31) & 0x7FFFFFFF)


def _indexer_kernel(*refs, tq, n, p_len, top_k):
    has_past = p_len > 0
    it = iter(refs)
    qi_ref, wi_ref, kin_ref = next(it), next(it), next(it)
    kip_ref = next(it) if has_past else None
    seln_ref = next(it)
    selp_ref = next(it) if has_past else None
    keyn_ref = next(it)
    keyp_ref = next(it) if has_past else None

    i = pl.program_id(1)
    qi = qi_ref[...]
    wi = wi_ref[...] * (N_IDX_HEADS ** -0.5 * D_IDX ** -0.5)

    def scores(kmat):
        kb = kmat.astype(MXU)
        acc = None
        for ih in range(N_IDX_HEADS):
            d = _dot_nt(qi[:, ih * D_IDX:(ih + 1) * D_IDX], kb)
            term = wi[:, ih:ih + 1] * jnp.maximum(d, 0.0)
            acc = term if acc is None else acc + term
        return acc

    def run(w_new):
        s_new = scores(kin_ref[:w_new, :])
        qq = i * tq + lax.broadcasted_iota(jnp.int32, (tq, w_new), 0)
        kk = lax.broadcasted_iota(jnp.int32, (tq, w_new), 1)
        s_new = jnp.where((kk // CHUNK) <= (qq // CHUNK), s_new, NEG_INF)
        keyn_ref[:, :w_new] = _sort_key(s_new)
        parts = [(keyn_ref, p_len, w_new, seln_ref)]
        if has_past:
            keyp_ref[...] = _sort_key(scores(kip_ref[0]))
            parts.append((keyp_ref, 0, p_len, selp_ref))

        def count(pred):
            tot = None
            for ref, base, width, _ in parts:
                idx = base + lax.broadcasted_iota(jnp.int32, (tq, width), 1)
                c = jnp.sum(jnp.where(pred(ref[:, :width], idx), 1.0, 0.0), axis=-1, keepdims=True)
                tot = c if tot is None else tot + c
            return tot

        def body(b, t):
            lo_bit = jnp.left_shift(jnp.int32(1), 30 - 2 * b)
            cands = [t + lo_bit, t + 2 * lo_bit, t + 3 * lo_bit]
            counts = [None] * 3
            for ref, _, width, _ in parts:
                k = ref[:, :width]
                for j in range(3):
                    c = jnp.sum(jnp.where(k >= cands[j], 1.0, 0.0), axis=-1, keepdims=True)
                    counts[j] = c if counts[j] is None else counts[j] + c
            for j in range(3):
                t = jnp.where(counts[j] >= top_k, cands[j], t)
            return t

        t = lax.fori_loop(0, 16, body, jnp.full((tq, 1), INT_MIN, jnp.int32))
        n_gt = count(lambda k, idx: k > t)
        n_eq = count(lambda k, idx: k == t)
        need = top_k - n_gt
        tie = jnp.where((n_eq != need) & (t > NEG_INF_KEY), 1.0, 0.0)

        def write(sel_of):
            for ref, base, width, out_ref in parts:
                k = ref[:, :width]
                idx = base + lax.broadcasted_iota(jnp.int32, (tq, width), 1)
                out_ref[:, :width] = jnp.where(sel_of(k, idx) & (k > NEG_INF_KEY), 1.0, 0.0)

        write(lambda k, idx: k >= t)
        if w_new < n:
            seln_ref[:, w_new:] = jnp.zeros((tq, n - w_new), F32)

        @pl.when(jnp.max(tie) > 0.0)
        def _():
            def jbody(b, jv):
                cand = jv + jnp.left_shift(jnp.int32(1), INDEX_BITS - 1 - b)
                c = count(lambda k, idx: (k == t) & (idx < cand))
                return jnp.where(c < need, cand, jv)

            jv = lax.fori_loop(0, INDEX_BITS, jbody, jnp.zeros((tq, 1), jnp.int32))
            write(lambda k, idx: (k > t) | ((k == t) & (idx <= jv)))

    if has_past or n <= TK:
        run(n)
    else:
        for c in range(n // TK):
            pl.when((i * tq) // TK == c)(functools.partial(run, (c + 1) * TK))


def _indexer(q_i, proj, k_idx_past, *, bsz, n, top_k):
    has_past = k_idx_past is not None
    tq = min(n, TQ_INDEXER)
    nq = n // tq
    p_len = k_idx_past.shape[1] if has_past else 0
    assert n + p_len < 2 ** (INDEX_BITS - 1) and (n <= TK or n % TK == 0)
    t = bsz * n
    in_specs = [pl.BlockSpec((tq, N_IDX_HEADS * D_IDX), lambda b, i: (b * nq + i, 0)),
                pl.BlockSpec((tq, LANES), lambda b, i: (b * nq + i, 9)),
                pl.BlockSpec((n, D_IDX), lambda b, i: (b, 8))]
    args = [q_i, proj, proj]
    out_specs = [pl.BlockSpec((tq, n), lambda b, i: (b * nq + i, 0))]
    out_shape = [jax.ShapeDtypeStruct((t, n), F32)]
    scratch = [pltpu.VMEM((tq, n), jnp.int32)]
    if has_past:
        in_specs.append(pl.BlockSpec((1, p_len, D_IDX), lambda b, i: (b, 0, 0)))
        args.append(k_idx_past)
        out_specs.append(pl.BlockSpec((tq, p_len), lambda b, i: (b * nq + i, 0)))
        out_shape.append(jax.ShapeDtypeStruct((t, p_len), F32))
        scratch.append(pltpu.VMEM((tq, p_len), jnp.int32))
    res = pl.pallas_call(
        functools.partial(_indexer_kernel, tq=tq, n=n, p_len=p_len, top_k=top_k),
        grid=(bsz, nq), in_specs=in_specs, out_specs=out_specs, out_shape=out_shape,
        scratch_shapes=scratch,
        compiler_params=_cparams(("arbitrary", "arbitrary")),
        name="indexer_past" if has_past else "indexer",
    )(*args)
    return res[0], (res[1] if has_past else None)


def _sparse_attn_kernel(*refs, tq, tkd, n, p_len):
    has_past = p_len > 0
    it = iter(refs)
    far_ref, q_ref, kvn_ref, seln_ref = (next(it) for _ in range(4))
    kvp_ref, selp_ref = (next(it), next(it)) if has_past else (None, None)
    bd_ref, bp_ref, o_ref = (next(it) for _ in range(3))
    nh = N_HEADS_C
    i = pl.program_id(1)
    q = q_ref[...].reshape(nh * tq, KV_LORA)

    def attend(regions):
        parts = []
        for kvb, sel, bias in regions:
            kvb = kvb.astype(MXU)
            w = kvb.shape[0]
            s = _dot_nt(q, kvb).reshape(nh, tq, w) * HEAD_DIM ** -0.5 + bias
            parts.append((jnp.where(sel[None] > 0.0, s, NEG_INF), kvb))
        m = None
        for s, _ in parts:
            mx = jnp.max(s, axis=-1, keepdims=True)
            m = mx if m is None else jnp.maximum(m, mx)
        l = None
        acc = None
        for s, kvb in parts:
            w = kvb.shape[0]
            p = jnp.exp(s - m)
            ps = jnp.sum(p, axis=-1, keepdims=True)
            pv = _dot(p.reshape(nh * tq, w).astype(MXU), kvb)
            l = ps if l is None else l + ps
            acc = pv if acc is None else acc + pv
        o_ref[...] = (acc.reshape(nh, tq, KV_LORA) / l).astype(o_ref.dtype)

    if has_past:
        regions = []
        if p_len > TK:
            regions.append((kvp_ref[0, :p_len - TK, :], selp_ref[:, :p_len - TK], far_ref[...]))
        regions.append((kvp_ref[0, p_len - TK:, :], selp_ref[:, p_len - TK:], bp_ref[0]))
        regions.append((kvn_ref[...], seln_ref[...], bd_ref[0]))
        attend(regions)
    else:
        def run(c):
            regions = []
            if c >= 2:
                regions.append((kvn_ref[:(c - 1) * TK, :], seln_ref[:, :(c - 1) * TK], far_ref[...]))
            if c >= 1:
                regions.append((kvn_ref[(c - 1) * TK:c * TK, :], seln_ref[:, (c - 1) * TK:c * TK], bp_ref[0]))
            regions.append((kvn_ref[c * TK:(c + 1) * TK, :], seln_ref[:, c * TK:(c + 1) * TK], bd_ref[0]))
            attend(regions)

        for c in range(n // TK):
            pl.when((i * tq) // TK == c)(functools.partial(run, c))


def _sparse_attn(q_lat, kv_lat, sel_new, kv_past, sel_past, table, *, bsz, n):
    has_past = kv_past is not None
    nh = N_HEADS_C
    t = bsz * n
    if has_past:
        tq, tkd = min(n, TQ_SPARSE_PAST), n
        p_len = kv_past.shape[1]
        assert p_len % TK == 0 and n <= TK
    else:
        tq, tkd = min(n, TQ_SPARSE), TK
        p_len = 0
        assert n % TK == 0
    q_offsets = list(range(0, tkd, tq))
    nq = n // tq
    npar = len(q_offsets)
    far = _bias_far(table, nh).reshape(nh, 1, 1)
    bd = _bias_tiles(table, q_offsets, 0, tq, tkd, nh)
    bp = _bias_tiles(table, q_offsets, -TK, tq, TK, nh)
    in_specs = [pl.BlockSpec((nh, 1, 1), lambda b, i: (0, 0, 0)),
                pl.BlockSpec((nh, tq, KV_LORA), lambda b, i: (0, b * nq + i, 0)),
                pl.BlockSpec((n, KV_LORA), lambda b, i: (b, 0)),
                pl.BlockSpec((tq, n), lambda b, i: (b * nq + i, 0))]
    args = [far, q_lat, kv_lat, sel_new]
    if has_past:
        in_specs += [pl.BlockSpec((1, p_len, KV_LORA), lambda b, i: (b, 0, 0)),
                     pl.BlockSpec((tq, p_len), lambda b, i: (b * nq + i, 0))]
        args += [kv_past, sel_past]
    in_specs += [pl.BlockSpec((1, nh, tq, tkd), lambda b, i: (i % npar, 0, 0, 0)),
                 pl.BlockSpec((1, nh, tq, TK), lambda b, i: (i % npar, 0, 0, 0))]
    args += [bd, bp]
    return pl.pallas_call(
        functools.partial(_sparse_attn_kernel, tq=tq, tkd=tkd, n=n, p_len=p_len),
        grid=(bsz, nq), in_specs=in_specs,
        out_specs=pl.BlockSpec((nh, tq, KV_LORA), lambda b, i: (0, b * nq + i, 0)),
        out_shape=jax.ShapeDtypeStruct((nh, t, KV_LORA), MXU),
        compiler_params=_cparams(("arbitrary", "arbitrary")),
        name="sparse_attn_past" if has_past else "sparse_attn",
    )(*args)


def _tile(n, cap, mult):
    best = None
    for t in range(mult, min(n, cap) + 1, mult):
        if n % t == 0:
            best = t
    assert best is not None, (n, cap, mult)
    return best


def _per_chunk(v, reps):
    return jnp.repeat(v, reps, axis=0)[:, None, :]


def _trunk(x, mod, past, p, w):
    bsz, n, d = x.shape
    t = bsz * n
    x = x.reshape(t, d)
    tm = _tile(t, 512, CHUNK)
    tmn = _tile(t, 256, CHUNK)
    tn_d = _tile(d, 1024, LANES)
    depth = p['w_ada'].shape[0]
    p_len = 0 if past is None else past[0].shape[2]
    top_k = min(TOPK_MAX, (p_len + n) // 4)
    even_rows, odd_rows = [], []
    for l in range(depth):
        m6 = mod[l].reshape(bsz, 6, d)
        sh1, sc1, g1, sh2, sc2, g2 = (_per_chunk(m6[:, k], n // CHUNK) for k in range(6))
        h = _norm(x, p['norm_mix_g'][l], tm=tmn, sc=sc1, sh=sh1, out_dtype=MXU)
        if l % 2 == 0:
            i = l // 2
            lam_init = 0.8 - 0.6 * math.exp(-0.3 * l)
            qkv = _mm(h, p['w_in_even'][i], n_out=48 * LANES, tm=tm, tn=1024, seg_out=True, name="mm_qkv")
            past_i = None if past is None else tuple(a[i] for a in past[:4])
            lam = (jnp.exp(jnp.sum(p['lam_q1'][i].astype(F32) * p['lam_k1'][i].astype(F32)))
                   - jnp.exp(jnp.sum(p['lam_q2'][i].astype(F32) * p['lam_k2'][i].astype(F32)))
                   + lam_init)
            if past is None:
                o = _even_attn(qkv, p['rel_bias_table'], lam, p['subln_g'][i], lam_init, bsz=bsz, n=n)
            else:
                o = _even_attn_past(qkv, past_i, p['rel_bias_table'], lam, p['subln_g'][i], lam_init,
                                    bsz=bsz, n=n)
            even_rows.append((qkv[1].reshape(bsz, n, N_MAPS_A, DA_QK),
                              qkv[2].reshape(bsz, n, N_HEADS_A, HEAD_DIM),
                              qkv[4].reshape(bsz, n, N_HEADS_B, HEAD_DIM),
                              qkv[5].reshape(bsz, n, N_HEADS_B, HEAD_DIM)))
            x = _mm(o, p['w_out_even'][i], n_out=d, tm=tm, tn=tn_d, gate=g1, res=x)
        else:
            j = l // 2
            proj = _mm(h, w['w_in_odd'][j], n_out=10 * LANES, tm=tm, tn=10 * LANES)
            c_q = _norm(proj, p['g_q'][j], tm=tmn, width=Q_LORA, col_block=0, out_dtype=MXU)
            kv_lat = _norm(proj, p['g_kv'][j], tm=tmn, width=KV_LORA, col_block=1)
            k_i = proj[:, 1024:1152]
            q = _mm(c_q, p['w_uq'][j], n_out=N_HEADS_C * HEAD_DIM, tm=tm, tn=1024, out_dtype=MXU)
            q_lat = _mm_heads_out(q, w['w_uk_t'][j], tm=_tile(t, TM_HEADS, CHUNK))
            q_i = _mm(c_q, p['w_qidx'][j], n_out=N_IDX_HEADS * D_IDX, tm=tm, tn=1024, out_dtype=MXU)
            kv_past = None if past is None else past[4][j]
            ki_past = None if past is None else past[5][j]
            sel_new, sel_past = _indexer(q_i, proj, ki_past, bsz=bsz, n=n, top_k=top_k)
            o_lat = _sparse_attn(q_lat, kv_lat, sel_new, kv_past, sel_past, p['rel_bias_table'],
                                 bsz=bsz, n=n)
            o = _mm_heads_in(o_lat, w['w_uv_h'][j], tm=_tile(t, TM_HEADS, CHUNK))
            odd_rows.append((kv_lat.reshape(bsz, n, KV_LORA), k_i.reshape(bsz, n, D_IDX)))
            x = _mm(o, p['w_out_odd'][j], n_out=d, tm=tm, tn=tn_d, gate=g1, res=x)
        h2, route = _norm_router(x, p['norm_ffn_g'][l], sc2, sh2, w['w_router'], p['router_bias'], tm=tmn)
        x = _moe(h2, route, w['w_gate'], w['w_up'], w['w_down'], l, g2, x, tm=tmn)
    y = _norm(x, p['final_norm_g'], tm=tmn).reshape(bsz, n, d)
    ev = tuple(jnp.stack([r[m] for r in even_rows]) for m in range(4))
    od = tuple(jnp.stack([r[m] for r in odd_rows]) for m in range(2))
    return y, ev, od


def kernel(x_prompt, x_sample, cache_a_k, cache_a_v, cache_b_k, cache_b_v, cache_c_kv, cache_c_idx,
           c_prompt, c_sample, rel_bias_table, norm_mix_g, norm_ffn_g, final_norm_g, w_ada, b_ada,
           w_in_even, lam_q1, lam_k1, lam_q2, lam_k2, subln_g, w_out_even, w_in_odd, g_q, g_kv,
           w_uq, w_qidx, w_uk, w_uv, w_out_odd, w_router, router_bias, w_gate, w_up, w_down):
    p = dict(rel_bias_table=rel_bias_table, norm_mix_g=norm_mix_g, norm_ffn_g=norm_ffn_g,
             final_norm_g=final_norm_g, w_ada=w_ada, b_ada=b_ada, w_in_even=w_in_even,
             lam_q1=lam_q1, lam_k1=lam_k1, lam_q2=lam_q2, lam_k2=lam_k2, subln_g=subln_g,
             w_out_even=w_out_even, g_q=g_q, g_kv=g_kv, w_uq=w_uq, w_qidx=w_qidx,
             w_out_odd=w_out_odd, router_bias=router_bias)
    d = x_prompt.shape[-1]
    depth = w_ada.shape[0]
    w = dict(
        w_in_odd=jnp.pad(w_in_odd, ((0, 0), (0, 0), (0, 10 * LANES - w_in_odd.shape[-1]))),
        w_uk_t=jnp.transpose(w_uk, (0, 2, 3, 1)),
        w_uv_h=jnp.transpose(w_uv, (0, 2, 1, 3)),
        w_router=jnp.pad(w_router, ((0, 0), (0, LANES - N_EXPERTS))),
        w_gate=w_gate.astype(MXU), w_up=w_up.astype(MXU), w_down=w_down.astype(MXU))
    nb_p, nb_s = c_prompt.shape[0], c_sample.shape[0]
    c_all = jnp.concatenate([c_prompt, c_sample], axis=0)
    rows = -(-(nb_p + nb_s) // 16) * 16
    c_act = jnp.pad(c_all * (1.0 / (1.0 + jnp.exp(-c_all))), ((0, rows - nb_p - nb_s), (0, 0)))
    tn_ada = _tile(6 * d, 1024, LANES)
    mods = [_mm(c_act, w_ada, layer=l, n_out=6 * d, tm=rows, tn=tn_ada, bias=b_ada[l], name="mm_ada")
            for l in range(depth)]
    mod_p = [m[:nb_p] for m in mods]
    mod_s = [m[nb_p:nb_p + nb_s] for m in mods]

    y_prompt, ev_p, od_p = _trunk(x_prompt, mod_p, None, p, w)
    past = (cache_a_k, cache_a_v, cache_b_k, cache_b_v, cache_c_kv, cache_c_idx)
    y_sample, ev_s, od_s = _trunk(x_sample, mod_s, past, p, w)
    return (y_prompt, y_sample) + ev_p + od_p + ev_s + od_s
```

```python
import functools
import math

import jax
import jax.numpy as jnp
import numpy as np
from jax import lax
from jax.experimental import pallas as pl
from jax.experimental.pallas import tpu as pltpu

F32 = jnp.float32
MXU = jnp.bfloat16

CHUNK = 64
HEAD_DIM = 128
N_HEADS_A = 8
N_MAPS_A = 16
DA_QK = 64
N_HEADS_B = 8
N_HEADS_C = 16
Q_LORA = 512
KV_LORA = 512
N_IDX_HEADS = 16
D_IDX = 128
TOPK_MAX = 256
N_BUCKETS = 32
MAX_DISTANCE = 128
N_EXPERTS = 16
N_GROUPS = 4
EXPERTS_PER_GROUP = 4
EPS = 1e-6
LANES = 128
TK = 256
TQ_SPARSE = 64
TQ_SPARSE_PAST = 32
TKC = 512
TM_HEADS = 2048
VMEM_LIMIT = 56 * 1024 * 1024
NEG_INF = float("-inf")


def _cparams(sem):
    return pltpu.CompilerParams(dimension_semantics=sem, vmem_limit_bytes=VMEM_LIMIT)


def _dot(a, b):
    return jnp.dot(a, b, preferred_element_type=F32)


def _dot_nt(a, b):
    return lax.dot_general(a, b, (((1,), (1,)), ((), ())), preferred_element_type=F32)


def _mm_kernel(*refs, has_bias, has_res, tm):
    it = iter(refs)
    a_ref, b_ref = next(it), next(it)
    bias_ref = next(it) if has_bias else None
    gate_ref, res_ref = (next(it), next(it)) if has_res else (None, None)
    o_ref, bsc = next(it), next(it)

    @pl.when(pl.program_id(1) == 0)
    def _():
        bsc[...] = b_ref[...].reshape(bsc.shape).astype(bsc.dtype)

    a = a_ref[...]
    a = a.reshape(a.shape[-2:]).astype(MXU)
    acc = _dot(a, bsc[...])
    if has_bias:
        acc = acc + bias_ref[...]
    if has_res:
        tn = acc.shape[-1]
        acc = (acc.reshape(tm // CHUNK, CHUNK, tn) * gate_ref[...]).reshape(tm, tn)
        acc = acc + res_ref[...]
    o_ref[...] = acc.reshape(o_ref.shape).astype(o_ref.dtype)


def _mm(a, b, *, n_out, tm, tn, bias=None, gate=None, res=None, out_dtype=F32, seg_out=False, layer=0,
        name="mm"):
    m, k = a.shape
    assert m % tm == 0 and n_out % tn == 0 and b.shape[-2] == k
    b_spec = (pl.BlockSpec((k, tn), lambda j, i: (0, j)) if b.ndim == 2
              else pl.BlockSpec((1, k, tn), lambda j, i: (layer, 0, j)))
    in_specs = [pl.BlockSpec((tm, k), lambda j, i: (i, 0)), b_spec]
    args = [a, b]
    if bias is not None:
        in_specs.append(pl.BlockSpec((1, tn), lambda j, i: (0, j)))
        args.append(bias.reshape(1, n_out))
    if res is not None:
        in_specs.append(pl.BlockSpec((tm // CHUNK, 1, tn), lambda j, i: (i, 0, j)))
        in_specs.append(pl.BlockSpec((tm, tn), lambda j, i: (i, j)))
        args += [gate, res]
    return pl.pallas_call(
        functools.partial(_mm_kernel, has_bias=bias is not None, has_res=res is not None, tm=tm),
        grid=(n_out // tn, m // tm), in_specs=in_specs,
        out_specs=(pl.BlockSpec((1, tm, tn), lambda j, i: (j, i, 0)) if seg_out
                   else pl.BlockSpec((tm, tn), lambda j, i: (i, j))),
        out_shape=jax.ShapeDtypeStruct((n_out // tn, m, tn) if seg_out else (m, n_out), out_dtype),
        scratch_shapes=[pltpu.VMEM((k, tn), MXU)],
        compiler_params=_cparams(("arbitrary", "arbitrary")), name=name,
    )(*args)


def _mm_heads_out(a, b_h, *, tm):
    m = a.shape[0]
    nh, ka, n = b_h.shape
    return pl.pallas_call(
        functools.partial(_mm_kernel, has_bias=False, has_res=False, tm=tm),
        grid=(nh, m // tm),
        in_specs=[pl.BlockSpec((tm, ka), lambda h, i: (i, h)),
                  pl.BlockSpec((1, ka, n), lambda h, i: (h, 0, 0))],
        out_specs=pl.BlockSpec((1, tm, n), lambda h, i: (h, i, 0)),
        out_shape=jax.ShapeDtypeStruct((nh, m, n), MXU),
        scratch_shapes=[pltpu.VMEM((ka, n), MXU)],
        compiler_params=_cparams(("arbitrary", "arbitrary")), name="mm_heads_out",
    )(a, b_h)


def _mm_heads_in(a_h, b_h, *, tm):
    nh, m, ka = a_h.shape
    n = b_h.shape[2]
    return pl.pallas_call(
        functools.partial(_mm_kernel, has_bias=False, has_res=False, tm=tm),
        grid=(nh, m // tm),
        in_specs=[pl.BlockSpec((1, tm, ka), lambda h, i: (h, i, 0)),
                  pl.BlockSpec((1, ka, n), lambda h, i: (h, 0, 0))],
        out_specs=pl.BlockSpec((tm, n), lambda h, i: (i, h)),
        out_shape=jax.ShapeDtypeStruct((m, nh * n), MXU),
        scratch_shapes=[pltpu.VMEM((ka, n), MXU)],
        compiler_params=_cparams(("arbitrary", "arbitrary")), name="mm_heads_in",
    )(a_h, b_h)


def _rms(x, g):
    return x * lax.rsqrt(jnp.mean(x * x, axis=-1, keepdims=True) + EPS) * g


def _modulate(y, sc_ref, sh_ref, tm):
    d = y.shape[-1]
    y3 = y.reshape(tm // CHUNK, CHUNK, d)
    y3 = y3 * (1.0 + sc_ref[...]) + sh_ref[...]
    return y3.reshape(tm, d)


def _norm_kernel(*refs, has_mod, tm):
    if has_mod:
        x_ref, g_ref, sc_ref, sh_ref, o_ref = refs
    else:
        x_ref, g_ref, o_ref = refs
    y = _rms(x_ref[...], g_ref[...])
    if has_mod:
        y = _modulate(y, sc_ref, sh_ref, tm)
    o_ref[...] = y.astype(o_ref.dtype)


def _norm(x, g, *, tm, width=None, col_block=0, sc=None, sh=None, out_dtype=F32):
    width = x.shape[1] if width is None else width
    rows = x.shape[0]
    assert rows % tm == 0
    has_mod = sc is not None
    in_specs = [pl.BlockSpec((tm, width), lambda i: (i, col_block)),
                pl.BlockSpec((1, width), lambda i: (0, 0))]
    args = [x, g.reshape(1, width)]
    if has_mod:
        in_specs += [pl.BlockSpec((tm // CHUNK, 1, width), lambda i: (i, 0, 0))] * 2
        args += [sc, sh]
    return pl.pallas_call(
        functools.partial(_norm_kernel, has_mod=has_mod, tm=tm),
        grid=(rows // tm,), in_specs=in_specs,
        out_specs=pl.BlockSpec((tm, width), lambda i: (i, 0)),
        out_shape=jax.ShapeDtypeStruct((rows, width), out_dtype),
        compiler_params=_cparams(("arbitrary",)), name="norm",
    )(*args)


def _split2(x):
    hi = x.astype(MXU)
    lo = (x - hi.astype(F32)).astype(MXU)
    return hi, lo


def _route(logits, rb_ref):
    tm = logits.shape[0]
    lt = logits.T
    aff = [1.0 / (1.0 + jnp.exp(-lt[e:e + 1, :])) for e in range(N_EXPERTS)]
    sc = [aff[e] + rb_ref[e] for e in range(N_EXPERTS)]
    npg = EXPERTS_PER_GROUP
    gscore = []
    for g in range(N_GROUPS):
        v = sc[g * npg:(g + 1) * npg]
        best = None
        for a in range(npg):
            for b in range(a + 1, npg):
                s = v[a] + v[b]
                best = s if best is None else jnp.maximum(best, s)
        gscore.append(best)
    gb = jnp.zeros((1, tm), jnp.int32)
    gv = gscore[0]
    for g in range(1, N_GROUPS):
        better = gscore[g] > gv
        gb = jnp.where(better, g, gb)
        gv = jnp.where(better, gscore[g], gv)
    u = [sc[j] for j in range(npg)]
    a4 = [aff[j] for j in range(npg)]
    for g in range(1, N_GROUPS):
        pick = gb == g
        u = [jnp.where(pick, sc[g * npg + j], u[j]) for j in range(npg)]
        a4 = [jnp.where(pick, aff[g * npg + j], a4[j]) for j in range(npg)]
    i1 = jnp.zeros((1, tm), jnp.int32)
    v1 = u[0]
    for j in range(1, npg):
        better = u[j] > v1
        i1 = jnp.where(better, j, i1)
        v1 = jnp.where(better, u[j], v1)
    i2 = jnp.full((1, tm), -1, jnp.int32)
    v2 = jnp.full((1, tm), NEG_INF, F32)
    for j in range(npg):
        better = (i1 != j) & ((u[j] > v2) | (i2 < 0))
        i2 = jnp.where(better, j, i2)
        v2 = jnp.where(better, u[j], v2)
    w1 = a4[0]
    w2 = a4[0]
    for j in range(1, npg):
        w1 = jnp.where(i1 == j, a4[j], w1)
        w2 = jnp.where(i2 == j, a4[j], w2)
    tot = w1 + w2
    w1 = w1 / tot
    w2 = w2 / tot
    e1 = gb * npg + i1
    e2 = gb * npg + i2
    rows = jnp.concatenate([e1.astype(F32), e2.astype(F32), w1, w2, jnp.zeros((LANES - 4, tm), F32)], axis=0)
    return rows.T


def _norm_router_kernel(x_ref, g_ref, sc_ref, sh_ref, wr_ref, rb_ref, h_ref, route_ref, *, tm):
    h = _modulate(_rms(x_ref[...], g_ref[...]), sc_ref, sh_ref, tm)
    h_ref[...] = h
    hh, hl = _split2(h)
    wh, wl = _split2(wr_ref[...])
    logits = _dot(hh, wh) + (_dot(hl, wh) + _dot(hh, wl))
    route_ref[...] = _route(logits, rb_ref)


def _norm_router(x, g, sc, sh, w_router_pad, router_bias, *, tm):
    t, d = x.shape
    return pl.pallas_call(
        functools.partial(_norm_router_kernel, tm=tm),
        grid=(t // tm,),
        in_specs=[pl.BlockSpec((tm, d), lambda i: (i, 0)),
                  pl.BlockSpec((1, d), lambda i: (0, 0)),
                  pl.BlockSpec((tm // CHUNK, 1, d), lambda i: (i, 0, 0)),
                  pl.BlockSpec((tm // CHUNK, 1, d), lambda i: (i, 0, 0)),
                  pl.BlockSpec((d, LANES), lambda i: (0, 0)),
                  pl.BlockSpec(memory_space=pltpu.SMEM)],
        out_specs=[pl.BlockSpec((tm, d), lambda i: (i, 0)),
                   pl.BlockSpec((tm, LANES), lambda i: (i, 0))],
        out_shape=[jax.ShapeDtypeStruct((t, d), F32), jax.ShapeDtypeStruct((t, LANES), F32)],
        compiler_params=_cparams(("arbitrary",)), name="norm_router",
    )(x, g.reshape(1, d), sc, sh, w_router_pad, router_bias)


TM_EXPERT = 256
TM_DISPATCH = 256


def _row_copy(src, src_row, dst, dst_row, sem):
    return pltpu.make_async_copy(src.at[pl.ds(src_row, 1), :], dst.at[pl.ds(dst_row, 1), :], sem)


def _rows_wait(src, dst, sem, n):
    pltpu.make_async_copy(src.at[pl.ds(0, n), :], dst.at[pl.ds(0, n), :], sem).wait()


def _dispatch_kernel(dest_ref, pad_ref, h_ref, xs_hbm, sems, *, tb, t, n_pad):
    i = pl.program_id(0)
    sem = sems.at[0]

    def body(r, c):
        tok = i * tb + r
        _row_copy(h_ref, r, xs_hbm, dest_ref[tok], sem).start()
        _row_copy(h_ref, r, xs_hbm, dest_ref[t + tok], sem).start()
        return c

    lax.fori_loop(0, tb, body, 0, unroll=8)
    for _ in range(2):
        _rows_wait(h_ref, xs_hbm, sem, tb)

    @pl.when(i == 0)
    def _():
        for c0 in range(0, n_pad, tb):
            nc = min(tb, n_pad - c0)

            def pad_body(r, c, c0=c0):
                _row_copy(h_ref, 0, xs_hbm, pad_ref[c0 + r], sem).start()
                return c

            lax.fori_loop(0, nc, pad_body, 0, unroll=8)
            _rows_wait(h_ref, xs_hbm, sem, nc)


def _expert_kernel(te_ref, nv_ref, x_ref, wg_ref, wu_ref, wd_ref, y_ref):
    del te_ref
    i = pl.program_id(0)

    @pl.when(i < nv_ref[0])
    def _():
        x = x_ref[...].astype(MXU)
        a = _dot(x, wg_ref[0, 0])
        b = _dot(x, wu_ref[0, 0])
        he = (a / (1.0 + jnp.exp(-a))) * b
        y_ref[...] = _dot(he.astype(MXU), wd_ref[0, 0])

    @pl.when(i >= nv_ref[0])
    def _():
        y_ref[...] = jnp.zeros_like(y_ref)


def _combine_kernel(dest_ref, y_hbm, route_ref, res_ref, g2_ref, o_ref, ybuf, sem, *, tm, t):
    i = pl.program_id(0)
    n = pl.num_programs(0)
    slot = i % 2

    def start(step, s):
        for j in range(2):
            def body(r, c):
                _row_copy(y_hbm, dest_ref[j * t + step * tm + r], ybuf.at[s, j], r, sem.at[s]).start()
                return c

            lax.fori_loop(0, tm, body, 0, unroll=8)

    @pl.when(i == 0)
    def _():
        start(0, 0)

    @pl.when(i + 1 < n)
    def _():
        start(i + 1, 1 - slot)

    for j in range(2):
        _rows_wait(y_hbm, ybuf.at[slot, j], sem.at[slot], tm)
    d = o_ref.shape[-1]
    route = route_ref[...]
    y = route[:, 2:3] * ybuf[slot, 0] + route[:, 3:4] * ybuf[slot, 1]
    o_ref[...] = res_ref[...] + (y.reshape(tm // CHUNK, CHUNK, d) * g2_ref[...]).reshape(tm, d)


def _cumsum_rows(x, blk):
    n, e = x.shape
    xb = x.reshape(n // blk, blk, e)
    tri = jnp.asarray(np.tril(np.ones((blk, blk), np.float32)))
    local = jnp.einsum('ij,bje->bie', tri, xb, precision=lax.Precision.HIGHEST)
    tot = local[:, -1, :]
    return (local + (jnp.cumsum(tot, axis=0) - tot)[:, None, :]).reshape(n, e)


def _moe(h, route, w_gate, w_up, w_down, layer, g2, res, *, tm):
    t, d = h.shape
    de = w_gate.shape[-1]
    te = _tile(2 * t, TM_EXPERT, CHUNK)
    n_tiles = 2 * t // te + N_EXPERTS
    r_pad = n_tiles * te
    n_pad = r_pad - 2 * t
    e_flat = route[:, 0:2].T.reshape(-1)
    experts = jnp.arange(N_EXPERTS, dtype=F32)
    onehot = (e_flat[:, None] == experts[None, :]).astype(F32)
    csum = _cumsum_rows(onehot, _tile(2 * t, 256, 8))
    counts = csum[-1]
    tiles_e = jnp.ceil(counts / te)
    tile_end = jnp.cumsum(tiles_e)
    seg_start = (tile_end - tiles_e) * te
    dest = jnp.sum(onehot * (seg_start[None, :] + csum - 1.0), axis=1).astype(jnp.int32)
    gap_start = jnp.concatenate([seg_start + counts, tile_end[-1:] * te])
    gap_len = jnp.concatenate([tiles_e * te - counts, r_pad - tile_end[-1:] * te])
    gap_first = jnp.cumsum(gap_len) - gap_len
    k = jnp.arange(n_pad, dtype=F32)[:, None]
    in_gap = (k >= gap_first[None, :]) & (k < (gap_first + gap_len)[None, :])
    pad_rows = jnp.sum(jnp.where(in_gap, gap_start[None, :] + k - gap_first[None, :], 0.0),
                       axis=1).astype(jnp.int32)
    tile_id = jnp.arange(n_tiles, dtype=F32)[:, None]
    tile_e = jnp.minimum(jnp.sum((tile_end[None, :] <= tile_id).astype(jnp.int32), axis=1),
                         N_EXPERTS - 1)
    n_valid = tile_end[-1:].astype(jnp.int32)

    tb = _tile(t, TM_DISPATCH, CHUNK)
    xs = pl.pallas_call(
        functools.partial(_dispatch_kernel, tb=tb, t=t, n_pad=n_pad),
        grid_spec=pltpu.PrefetchScalarGridSpec(
            num_scalar_prefetch=2, grid=(t // tb,),
            in_specs=[pl.BlockSpec((tb, d), lambda i, dst, pad: (i, 0))],
            out_specs=pl.BlockSpec(memory_space=pl.ANY),
            scratch_shapes=[pltpu.SemaphoreType.DMA((1,))]),
        out_shape=jax.ShapeDtypeStruct((r_pad, d), F32),
        compiler_params=_cparams(("arbitrary",)), name="moe_dispatch",
    )(dest, pad_rows, h)

    y = pl.pallas_call(
        _expert_kernel,
        grid_spec=pltpu.PrefetchScalarGridSpec(
            num_scalar_prefetch=2, grid=(n_tiles,),
            in_specs=[pl.BlockSpec((te, d), lambda i, e, n: (i, 0)),
                      pl.BlockSpec((1, 1, d, de), lambda i, e, n: (layer, e[i], 0, 0)),
                      pl.BlockSpec((1, 1, d, de), lambda i, e, n: (layer, e[i], 0, 0)),
                      pl.BlockSpec((1, 1, de, d), lambda i, e, n: (layer, e[i], 0, 0))],
            out_specs=pl.BlockSpec((te, d), lambda i, e, n: (i, 0))),
        out_shape=jax.ShapeDtypeStruct((r_pad, d), F32),
        compiler_params=_cparams(("arbitrary",)), name="moe_experts",
    )(tile_e, n_valid, xs, w_gate, w_up, w_down)

    return pl.pallas_call(
        functools.partial(_combine_kernel, tm=tm, t=t),
        grid_spec=pltpu.PrefetchScalarGridSpec(
            num_scalar_prefetch=1, grid=(t // tm,),
            in_specs=[pl.BlockSpec(memory_space=pl.ANY),
                      pl.BlockSpec((tm, LANES), lambda i, dst: (i, 0)),
                      pl.BlockSpec((tm, d), lambda i, dst: (i, 0)),
                      pl.BlockSpec((tm // CHUNK, 1, d), lambda i, dst: (i, 0, 0))],
            out_specs=pl.BlockSpec((tm, d), lambda i, dst: (i, 0)),
            scratch_shapes=[pltpu.VMEM((2, 2, tm, d), F32), pltpu.SemaphoreType.DMA((2,))]),
        out_shape=jax.ShapeDtypeStruct((t, d), F32),
        compiler_params=_cparams(("arbitrary",)), name="moe_combine",
    )(dest, y, route, res, g2)


def _rel_bucket(rel):
    half = N_BUCKETS // 2
    max_exact = half // 2
    n = jnp.abs(rel)
    nf = jnp.maximum(n, 1).astype(F32)
    large = max_exact + (jnp.log(nf / max_exact) / math.log(MAX_DISTANCE / max_exact)
                         * (half - max_exact)).astype(jnp.int32)
    large = jnp.minimum(large, half - 1)
    return jnp.where(rel > 0, half, 0) + jnp.where(n < max_exact, n, large)


def _bias_tiles(table, q_offsets, k_offset, tq, tk, n_heads):
    tiles = []
    for q0 in q_offsets:
        rel = (k_offset + np.arange(tk))[None, :] - (q0 + np.arange(tq))[:, None]
        bucket = _rel_bucket(jnp.asarray(rel, jnp.int32))
        onehot = (bucket[..., None] == jnp.arange(N_BUCKETS, dtype=jnp.int32)).astype(F32)
        tiles.append(jnp.einsum('qkb,bh->hqk', onehot, table[:, :n_heads].astype(F32),
                                precision=lax.Precision.HIGHEST))
    return jnp.stack(tiles)


def _bias_far(table, n_heads):
    rel = jnp.full((1,), -(MAX_DISTANCE + 1), jnp.int32)
    return table[_rel_bucket(rel)][0, :n_heads].astype(F32)


def _diff_attend(qr, far, prev, diag, bias_far, bp_ref, bd_ref, vis):
    parts = [[], []]
    for r in range(2):
        cols = slice(r * DA_QK, (r + 1) * DA_QK)
        if far is not None:
            parts[r].append(_dot_nt(qr[r], far[0][:, cols].astype(MXU)) + bias_far[r])
        if prev is not None:
            parts[r].append(_dot_nt(qr[r], prev[0][:, cols].astype(MXU)) + bp_ref[0, r])
        s = _dot_nt(qr[r], diag[0][:, cols].astype(MXU)) + bd_ref[0, r]
        parts[r].append(jnp.where(vis, s, NEG_INF))
    values = [reg[1].astype(MXU) for reg in (far, prev, diag) if reg is not None]
    m = []
    for r in range(2):
        mx = None
        for s in parts[r]:
            pm = jnp.max(s, axis=-1, keepdims=True)
            mx = pm if mx is None else jnp.maximum(mx, pm)
        m.append(mx)
    outs = []
    for r in range(2):
        l = None
        acc = None
        for s, v in zip(parts[r], values):
            p = jnp.exp(s - m[r])
            ps = jnp.sum(p, axis=-1, keepdims=True)
            pv = _dot(p.astype(MXU), v)
            l = ps if l is None else l + ps
            acc = pv if acc is None else acc + pv
        outs.append(acc / l)
    return outs


def _stick_attend(qbs, blocks, tri_ref, trid_ref, carries):
    nh = len(qbs)
    pairs = [(h, j) for h in range(nh) for j in range(len(blocks[h]))]
    log_beta, log_keep, later, total = {}, {}, {}, {}
    for h, j in pairs:
        kb, _, mask = blocks[h][j]
        z = _dot_nt(qbs[h], kb.astype(MXU)) * HEAD_DIM ** -0.5
        log_beta[h, j] = jnp.minimum(z, 0.0) - jnp.log(1.0 + jnp.exp(-jnp.abs(z)))
        lk = log_beta[h, j] - z
        log_keep[h, j] = lk if mask is None else jnp.where(mask, lk, 0.0)
    for h, j in pairs:
        lk = log_keep[h, j]
        tri = (trid_ref if lk.shape[1] != TK else tri_ref)[...]
        hi, lo = _split2(lk)
        later[h, j] = _dot(hi, tri) + _dot(lo, tri)
        total[h, j] = jnp.broadcast_to(later[h, j][:, 0:1] + lk[:, 0:1], (lk.shape[0], LANES))
    accs, new_carries = [], []
    for h in range(nh):
        carry = carries[h]
        acc = None
        for j, (_, vb, mask) in enumerate(blocks[h]):
            lt = later[h, j]
            tk = lt.shape[1]
            if carry is not None:
                lt = lt + (carry[:, :tk] if tk <= LANES else jnp.concatenate([carry] * (tk // LANES), axis=1))
            a = jnp.exp(log_beta[h, j] + lt)
            if mask is not None:
                a = jnp.where(mask, a, 0.0)
            pv = _dot(a.astype(MXU), vb.astype(MXU))
            acc = pv if acc is None else acc + pv
            carry = total[h, j] if carry is None else carry + total[h, j]
        accs.append(acc)
        new_carries.append(carry)
    return accs, new_carries


def _even_attn_kernel(far_ref, lam_ref, q_ref, kn_ref, vn_ref, bd_ref, bp_ref, g_ref, tri_ref, o_ref,
                      *, tq, nq, lam_init):
    u = pl.program_id(1)
    i = pl.program_id(2)
    q = q_ref[0]
    qq = lax.broadcasted_iota(jnp.int32, (tq, TK), 0)
    kk = lax.broadcasted_iota(jnp.int32, (tq, TK), 1)

    def regions(c):
        far = (0, (c - 1) * TK) if c >= 2 else None
        prev = ((c - 1) * TK, c * TK) if c >= 1 else None
        return far, prev, (c * TK, (c + 1) * TK)

    def load(reg, lo=None, hi=None):
        a, b = reg
        lo, hi = (a, b) if lo is None else (a + lo, a + hi)
        return kn_ref[0, lo:hi, :], vn_ref[0, lo:hi, :]

    def diff(c):
        far, prev, diag = regions(c)
        qr = [(q[:, r * DA_QK:(r + 1) * DA_QK] * DA_QK ** -0.5).astype(MXU) for r in range(2)]
        vis = (kk // CHUNK) <= (qq // CHUNK)
        o0, o1 = _diff_attend(qr, None if far is None else load(far), None if prev is None else load(prev),
                              load(diag), [far_ref[2 * u], far_ref[2 * u + 1]], bp_ref, bd_ref, vis)
        o = _rms(o0 - lam_ref[0] * o1, g_ref[...]) * (1.0 - lam_init)
        o_ref[...] = o.astype(o_ref.dtype)

    def stick(c):
        nb = c + 1
        k, v = load((0, nb * TK))
        z = _dot_nt(q.astype(MXU), k.astype(MXU)) * HEAD_DIM ** -0.5
        log_beta = jnp.minimum(z, 0.0) - jnp.log(1.0 + jnp.exp(-jnp.abs(z)))
        log_keep = log_beta - z
        earlier = kk < qq
        tri = tri_ref[...]
        later, total = [], []
        for j in range(nb):
            lk = log_keep[:, j * TK:(j + 1) * TK]
            if j == nb - 1:
                lk = jnp.where(earlier, lk, 0.0)
            hi, lo = _split2(lk)
            lt = _dot(hi, tri) + _dot(lo, tri)
            later.append(lt)
            total.append(jnp.broadcast_to(lt[:, 0:1] + lk[:, 0:1], (tq, LANES)))
        carry = None
        for j in range(nb - 1, -1, -1):
            if carry is not None:
                later[j] = later[j] + jnp.concatenate([carry] * (TK // LANES), axis=1)
            carry = total[j] if carry is None else carry + total[j]
        a = [jnp.exp(log_beta[:, j * TK:(j + 1) * TK] + later[j]) for j in range(nb)]
        a[-1] = jnp.where(earlier, a[-1], 0.0)
        a = a[0] if nb == 1 else jnp.concatenate(a, axis=1)
        o_ref[...] = _dot(a.astype(MXU), v.astype(MXU)).astype(o_ref.dtype)

    for c in range(nq):
        pl.when((u < N_HEADS_A) & (i == c))(functools.partial(diff, c))
        pl.when((u >= N_HEADS_A) & (i == c))(functools.partial(stick, c))


def _tri_later(tk):
    return jnp.asarray(np.arange(tk)[:, None] > np.arange(tk)[None, :], MXU)


def _even_attn(qkv, table, lam, subln_g, lam_init, *, bsz, n):
    tq = TK
    assert n % tq == 0
    nq = n // tq
    far = _bias_far(table, N_MAPS_A)
    bd = _bias_tiles(table, [0], 0, tq, TK, N_MAPS_A)
    bp = _bias_tiles(table, [0], -TK, tq, TK, N_MAPS_A)

    def seg(u, base):
        return jnp.where(u < 8, base, base + 3)

    smem = pl.BlockSpec(memory_space=pltpu.SMEM)
    in_specs = [smem, smem,
                pl.BlockSpec((1, tq, HEAD_DIM), lambda b, u, i: (seg(u, 0), b * nq + i, u % 8)),
                pl.BlockSpec((1, n, HEAD_DIM), lambda b, u, i: (seg(u, 1), b, u % 8)),
                pl.BlockSpec((1, n, HEAD_DIM), lambda b, u, i: (seg(u, 2), b, u % 8)),
                pl.BlockSpec((1, 2, tq, TK), lambda b, u, i: (0, jnp.minimum(u, 7), 0, 0)),
                pl.BlockSpec((1, 2, tq, TK), lambda b, u, i: (0, jnp.minimum(u, 7), 0, 0)),
                pl.BlockSpec((1, HEAD_DIM), lambda b, u, i: (0, 0)),
                pl.BlockSpec((TK, TK), lambda b, u, i: (0, 0))]
    return pl.pallas_call(
        functools.partial(_even_attn_kernel, tq=tq, nq=nq, lam_init=lam_init),
        grid=(bsz, 16, nq), in_specs=in_specs,
        out_specs=pl.BlockSpec((tq, HEAD_DIM), lambda b, u, i: (b * nq + i, u)),
        out_shape=jax.ShapeDtypeStruct((bsz * n, 16 * HEAD_DIM), MXU),
        compiler_params=_cparams(("arbitrary", "arbitrary", "arbitrary")), name="even_attn",
    )(far, lam.reshape(1), qkv, qkv, qkv, bd, bp, subln_g.reshape(1, HEAD_DIM), _tri_later(TK))


def _even_attn_past_kernel(lam_ref, q_ref, kn_ref, vn_ref, ak_hbm, av_hbm, bk_hbm, bv_hbm, bd_ref, bc_ref,
                           g_ref, tri_ref, trid_ref, o_ref, m_ref, l_ref, acc_ref, carry_ref,
                           ka_buf, kb_buf, v_buf, sem, *, n, tkc, nkc, lam_init):
    b = pl.program_id(0)
    mix = pl.program_id(1)
    kc = pl.program_id(2)
    step = (b * 2 + mix) * nkc + kc
    n_steps = pl.num_programs(0) * 2 * nkc
    slot = step % 2
    qq = lax.broadcasted_iota(jnp.int32, (n, n), 0)
    kk = lax.broadcasted_iota(jnp.int32, (n, n), 1)

    def chunk_copies(bb, mm_is_stick, cc, s):
        if mm_is_stick:
            rows = pl.ds(pl.multiple_of((nkc - 1 - cc) * tkc, tkc), tkc)
            return ([pltpu.make_async_copy(bk_hbm.at[bb, rows, h, :], kb_buf.at[s, h], sem.at[s])
                     for h in range(N_HEADS_B)] +
                    [pltpu.make_async_copy(bv_hbm.at[bb, rows, h, :], v_buf.at[s, h], sem.at[s])
                     for h in range(N_HEADS_B)])
        rows = pl.ds(pl.multiple_of(cc * tkc, tkc), tkc)
        return ([pltpu.make_async_copy(ak_hbm.at[bb, rows, pl.ds(h * HEAD_DIM, HEAD_DIM)], ka_buf.at[s, h],
                                       sem.at[s]) for h in range(N_HEADS_A)] +
                [pltpu.make_async_copy(av_hbm.at[bb, rows, h, :], v_buf.at[s, h], sem.at[s])
                 for h in range(N_HEADS_A)])

    def start(bb, mm, cc, s):
        for stick in (False, True):
            @pl.when(mm == int(stick))
            def _(stick=stick):
                for cp in chunk_copies(bb, stick, cc, s):
                    cp.start()

    @pl.when(step == 0)
    def _():
        start(b, mix, kc, slot)

    @pl.when(step + 1 < n_steps)
    def _():
        nxt = step + 1
        start(nxt // (2 * nkc), (nxt // nkc) % 2, nxt % nkc, 1 - slot)

    for stick in (False, True):
        @pl.when(mix == int(stick))
        def _(stick=stick):
            for cp in chunk_copies(b, stick, kc, slot):
                cp.wait()

    def head(ref, h, width=HEAD_DIM, off=0):
        return ref[0, :, h * HEAD_DIM + off:h * HEAD_DIM + off + width]

    def diff_update(s, vs, first):
        mx = jnp.max(s, axis=-1, keepdims=True)
        m_new = mx if first else jnp.maximum(m_ref[...], mx)
        p = jnp.exp(s - m_new)
        ps = jnp.sum(p, axis=-1, keepdims=True)
        pv = jnp.stack([_dot(p[mp].astype(MXU), vs[mp // 2].astype(MXU)) for mp in range(N_MAPS_A)])
        if first:
            l_ref[...] = ps
            acc_ref[...] = pv
        else:
            alpha = jnp.exp(m_ref[...] - m_new)
            l_ref[...] = alpha * l_ref[...] + ps
            acc_ref[...] = alpha * acc_ref[...] + pv
        m_ref[...] = m_new

    def diff_q(mp):
        h, r = divmod(mp, 2)
        return (head(q_ref, h, DA_QK, r * DA_QK) * DA_QK ** -0.5).astype(MXU)

    @pl.when((mix == 0) & (kc == 0))
    def _():
        vis = (kk // CHUNK) <= (qq // CHUNK)
        s = jnp.stack([_dot_nt(diff_q(mp), head(kn_ref, mp // 2, DA_QK, (mp % 2) * DA_QK).astype(MXU))
                       for mp in range(N_MAPS_A)]) + bd_ref[0]
        diff_update(jnp.where(vis[None], s, NEG_INF), [head(vn_ref, h) for h in range(N_HEADS_A)], True)

    @pl.when(mix == 0)
    def _():
        kh = [ka_buf[slot, h].astype(MXU) for h in range(N_HEADS_A)]
        s = jnp.stack([_dot_nt(diff_q(mp), kh[mp // 2][:, (mp % 2) * DA_QK:(mp % 2 + 1) * DA_QK])
                       for mp in range(N_MAPS_A)]) + bc_ref[0]
        diff_update(s, [v_buf[slot, h] for h in range(N_HEADS_A)], False)

    @pl.when((mix == 0) & (kc == nkc - 1))
    def _():
        for h in range(N_HEADS_A):
            o = acc_ref[2 * h] / l_ref[2 * h] - lam_ref[0] * (acc_ref[2 * h + 1] / l_ref[2 * h + 1])
            o = _rms(o, g_ref[...]) * (1.0 - lam_init)
            o_ref[:, h * HEAD_DIM:(h + 1) * HEAD_DIM] = o.astype(o_ref.dtype)

    def stick_q(h):
        return head(q_ref, h).astype(MXU)

    @pl.when((mix == 1) & (kc == 0))
    def _():
        heads = range(N_HEADS_B)
        accs, carries = _stick_attend([stick_q(h) for h in heads],
                                      [[(head(kn_ref, h), head(vn_ref, h), kk < qq)] for h in heads],
                                      tri_ref, trid_ref, [None] * N_HEADS_B)
        acc_ref[:N_HEADS_B] = jnp.stack(accs)
        carry_ref[...] = jnp.stack(carries)

    @pl.when(mix == 1)
    def _():
        carry = carry_ref[...]
        heads = range(N_HEADS_B)
        blocks = [[(kb_buf[slot, h, j * TK:(j + 1) * TK, :], v_buf[slot, h, j * TK:(j + 1) * TK, :], None)
                   for j in range(tkc // TK - 1, -1, -1)] for h in heads]
        accs, carries = _stick_attend([stick_q(h) for h in heads], blocks, tri_ref, trid_ref,
                                      [carry[h] for h in heads])
        acc_ref[:N_HEADS_B] = acc_ref[:N_HEADS_B] + jnp.stack(accs)
        carry_ref[...] = jnp.stack(carries)

    @pl.when((mix == 1) & (kc == nkc - 1))
    def _():
        for h in range(N_HEADS_B):
            o_ref[:, h * HEAD_DIM:(h + 1) * HEAD_DIM] = acc_ref[h].astype(o_ref.dtype)


def _even_attn_past(qkv, past, table, lam, subln_g, lam_init, *, bsz, n):
    ak, av, bk, bv = past
    p_len = ak.shape[1]
    ak = ak.reshape(bsz, p_len, N_MAPS_A * DA_QK)
    tkc = _tile(p_len, TKC, TK)
    nkc = p_len // tkc
    assert n <= CHUNK and n % 8 == 0
    bd = _bias_tiles(table, [0], 0, n, n, N_MAPS_A)
    far = jnp.broadcast_to(_bias_far(table, N_MAPS_A)[:, None, None], (N_MAPS_A, n, tkc))
    bc = jnp.stack([far, _bias_tiles(table, [0], -tkc, n, tkc, N_MAPS_A)[0]])

    def row(s):
        return lambda b, mix, kc: (3 * mix + s, b, 0)

    width = N_HEADS_A * HEAD_DIM
    in_specs = [pl.BlockSpec(memory_space=pltpu.SMEM),
                pl.BlockSpec((1, n, width), row(0)),
                pl.BlockSpec((1, n, width), row(1)),
                pl.BlockSpec((1, n, width), row(2)),
                pl.BlockSpec(memory_space=pl.ANY), pl.BlockSpec(memory_space=pl.ANY),
                pl.BlockSpec(memory_space=pl.ANY), pl.BlockSpec(memory_space=pl.ANY),
                pl.BlockSpec((1, N_MAPS_A, n, n), lambda b, mix, kc: (0, 0, 0, 0)),
                pl.BlockSpec((1, N_MAPS_A, n, tkc), lambda b, mix, kc: ((kc + 1) // nkc, 0, 0, 0)),
                pl.BlockSpec((1, HEAD_DIM), lambda b, mix, kc: (0, 0)),
                pl.BlockSpec((TK, TK), lambda b, mix, kc: (0, 0)),
                pl.BlockSpec((n, n), lambda b, mix, kc: (0, 0))]
    return pl.pallas_call(
        functools.partial(_even_attn_past_kernel, n=n, tkc=tkc, nkc=nkc, lam_init=lam_init),
        grid=(bsz, 2, nkc), in_specs=in_specs,
        out_specs=pl.BlockSpec((n, width), lambda b, mix, kc: (b, mix)),
        out_shape=jax.ShapeDtypeStruct((bsz * n, 2 * width), MXU),
        scratch_shapes=[pltpu.VMEM((N_MAPS_A, n, 1), F32), pltpu.VMEM((N_MAPS_A, n, 1), F32),
                        pltpu.VMEM((N_MAPS_A, n, HEAD_DIM), F32), pltpu.VMEM((N_HEADS_B, n, LANES), F32),
                        pltpu.VMEM((2, N_HEADS_A, tkc, HEAD_DIM), F32), pltpu.VMEM((2, N_HEADS_B, tkc, HEAD_DIM), F32),
                        pltpu.VMEM((2, N_HEADS_A, tkc, HEAD_DIM), F32), pltpu.SemaphoreType.DMA((2,))],
        compiler_params=_cparams(("arbitrary", "arbitrary", "arbitrary")), name="even_attn_past",
    )(lam.reshape(1), qkv, qkv, qkv, ak, av, bk, bv, bd, bc, subln_g.reshape(1, HEAD_DIM),
      _tri_later(TK), _tri_later(n))


INT_MIN = int(np.iinfo(np.int32).min)
NEG_INF_KEY = int(np.array(-np.inf, np.float32).view(np.int32)) ^ 0x7FFFFFFF
INDEX_BITS = 15


def _sort_key(x):
    b = lax.bitcast_convert_type(x, jnp.int32)
    return b ^ ((b >> 31) & 0x7FFFFFFF)


def _indexer_kernel(*refs, tq, n, p_len, top_k):
    has_past = p_len > 0
    it = iter(refs)
    qi_ref, wi_ref, kin_ref = next(it), next(it), next(it)
    kip_ref = next(it) if has_past else None
    seln_ref = next(it)
    selp_ref = next(it) if has_past else None
    keyn_ref = next(it)
    keyp_ref = next(it) if has_past else None

    i = pl.program_id(1)
    qi = qi_ref[...]
    wi = wi_ref[...] * (N_IDX_HEADS ** -0.5 * D_IDX ** -0.5)

    def scores(kmat):
        kb = kmat.astype(MXU)
        acc = None
        for ih in range(N_IDX_HEADS):
            d = _dot_nt(qi[:, ih * D_IDX:(ih + 1) * D_IDX], kb)
            term = wi[:, ih:ih + 1] * jnp.maximum(d, 0.0)
            acc = term if acc is None else acc + term
        return acc

    def run(w_new):
        s_new = scores(kin_ref[:w_new, :])
        qq = i * tq + lax.broadcasted_iota(jnp.int32, (tq, w_new), 0)
        kk = lax.broadcasted_iota(jnp.int32, (tq, w_new), 1)
        s_new = jnp.where((kk // CHUNK) <= (qq // CHUNK), s_new, NEG_INF)
        keyn_ref[:, :w_new] = _sort_key(s_new)
        parts = [(keyn_ref, p_len, w_new, seln_ref)]
        if has_past:
            keyp_ref[...] = _sort_key(scores(kip_ref[0]))
            parts.append((keyp_ref, 0, p_len, selp_ref))

        def count(pred):
            tot = None
            for ref, base, width, _ in parts:
                idx = base + lax.broadcasted_iota(jnp.int32, (tq, width), 1)
                c = jnp.sum(jnp.where(pred(ref[:, :width], idx), 1.0, 0.0), axis=-1, keepdims=True)
                tot = c if tot is None else tot + c
            return tot

        def body(b, t):
            lo_bit = jnp.left_shift(jnp.int32(1), 30 - 2 * b)
            cands = [t + lo_bit, t + 2 * lo_bit, t + 3 * lo_bit]
            counts = [None] * 3
            for ref, _, width, _ in parts:
                k = ref[:, :width]
                for j in range(3):
                    c = jnp.sum(jnp.where(k >= cands[j], 1.0, 0.0), axis=-1, keepdims=True)
                    counts[j] = c if counts[j] is None else counts[j] + c
            for j in range(3):
                t = jnp.where(counts[j] >= top_k, cands[j], t)
            return t

        t = lax.fori_loop(0, 16, body, jnp.full((tq, 1), INT_MIN, jnp.int32))
        n_gt = count(lambda k, idx: k > t)
        n_eq = count(lambda k, idx: k == t)
        need = top_k - n_gt
        tie = jnp.where((n_eq != need) & (t > NEG_INF_KEY), 1.0, 0.0)

        def write(sel_of):
            for ref, base, width, out_ref in parts:
                k = ref[:, :width]
                idx = base + lax.broadcasted_iota(jnp.int32, (tq, width), 1)
                out_ref[:, :width] = jnp.where(sel_of(k, idx) & (k > NEG_INF_KEY), 1.0, 0.0)

        write(lambda k, idx: k >= t)
        if w_new < n:
            seln_ref[:, w_new:] = jnp.zeros((tq, n - w_new), F32)

        @pl.when(jnp.max(tie) > 0.0)
        def _():
            def jbody(b, jv):
                cand = jv + jnp.left_shift(jnp.int32(1), INDEX_BITS - 1 - b)
                c = count(lambda k, idx: (k == t) & (idx < cand))
                return jnp.where(c < need, cand, jv)

            jv = lax.fori_loop(0, INDEX_BITS, jbody, jnp.zeros((tq, 1), jnp.int32))
            write(lambda k, idx: (k > t) | ((k == t) & (idx <= jv)))

    if has_past or n <= TK:
        run(n)
    else:
        for c in range(n // TK):
            pl.when((i * tq) // TK == c)(functools.partial(run, (c + 1) * TK))


def _indexer(q_i, proj, k_idx_past, *, bsz, n, top_k):
    has_past = k_idx_past is not None
    tq = min(n, 128)
    nq = n // tq
    p_len = k_idx_past.shape[1] if has_past else 0
    assert n + p_len < 2 ** (INDEX_BITS - 1) and (n <= TK or n % TK == 0)
    t = bsz * n
    in_specs = [pl.BlockSpec((tq, N_IDX_HEADS * D_IDX), lambda b, i: (b * nq + i, 0)),
                pl.BlockSpec((tq, LANES), lambda b, i: (b * nq + i, 9)),
                pl.BlockSpec((n, D_IDX), lambda b, i: (b, 8))]
    args = [q_i, proj, proj]
    out_specs = [pl.BlockSpec((tq, n), lambda b, i: (b * nq + i, 0))]
    out_shape = [jax.ShapeDtypeStruct((t, n), F32)]
    scratch = [pltpu.VMEM((tq, n), jnp.int32)]
    if has_past:
        in_specs.append(pl.BlockSpec((1, p_len, D_IDX), lambda b, i: (b, 0, 0)))
        args.append(k_idx_past)
        out_specs.append(pl.BlockSpec((tq, p_len), lambda b, i: (b * nq + i, 0)))
        out_shape.append(jax.ShapeDtypeStruct((t, p_len), F32))
        scratch.append(pltpu.VMEM((tq, p_len), jnp.int32))
    res = pl.pallas_call(
        functools.partial(_indexer_kernel, tq=tq, n=n, p_len=p_len, top_k=top_k),
        grid=(bsz, nq), in_specs=in_specs, out_specs=out_specs, out_shape=out_shape,
        scratch_shapes=scratch,
        compiler_params=_cparams(("arbitrary", "arbitrary")),
        name="indexer_past" if has_past else "indexer",
    )(*args)
    return res[0], (res[1] if has_past else None)


def _sparse_attn_kernel(*refs, tq, tkd, n, p_len):
    has_past = p_len > 0
    it = iter(refs)
    far_ref, q_ref, kvn_ref, seln_ref = (next(it) for _ in range(4))
    kvp_ref, selp_ref = (next(it), next(it)) if has_past else (None, None)
    bd_ref, bp_ref, o_ref = (next(it) for _ in range(3))
    nh = N_HEADS_C
    i = pl.program_id(1)
    q = q_ref[...].reshape(nh * tq, KV_LORA)

    def attend(regions):
        parts = []
        for kvb, sel, bias in regions:
            kvb = kvb.astype(MXU)
            w = kvb.shape[0]
            s = _dot_nt(q, kvb).reshape(nh, tq, w) * HEAD_DIM ** -0.5 + bias
            parts.append((jnp.where(sel[None] > 0.0, s, NEG_INF), kvb))
        m = None
        for s, _ in parts:
            mx = jnp.max(s, axis=-1, keepdims=True)
            m = mx if m is None else jnp.maximum(m, mx)
        l = None
        acc = None
        for s, kvb in parts:
            w = kvb.shape[0]
            p = jnp.exp(s - m)
            ps = jnp.sum(p, axis=-1, keepdims=True)
            pv = _dot(p.reshape(nh * tq, w).astype(MXU), kvb)
            l = ps if l is None else l + ps
            acc = pv if acc is None else acc + pv
        o_ref[...] = (acc.reshape(nh, tq, KV_LORA) / l).astype(o_ref.dtype)

    if has_past:
        regions = []
        if p_len > TK:
            regions.append((kvp_ref[0, :p_len - TK, :], selp_ref[:, :p_len - TK], far_ref[...]))
        regions.append((kvp_ref[0, p_len - TK:, :], selp_ref[:, p_len - TK:], bp_ref[0]))
        regions.append((kvn_ref[...], seln_ref[...], bd_ref[0]))
        attend(regions)
    else:
        def run(c):
            regions = []
            if c >= 2:
                regions.append((kvn_ref[:(c - 1) * TK, :], seln_ref[:, :(c - 1) * TK], far_ref[...]))
            if c >= 1:
                regions.append((kvn_ref[(c - 1) * TK:c * TK, :], seln_ref[:, (c - 1) * TK:c * TK], bp_ref[0]))
            regions.append((kvn_ref[c * TK:(c + 1) * TK, :], seln_ref[:, c * TK:(c + 1) * TK], bd_ref[0]))
            attend(regions)

        for c in range(n // TK):
            pl.when((i * tq) // TK == c)(functools.partial(run, c))


def _sparse_attn(q_lat, kv_lat, sel_new, kv_past, sel_past, table, *, bsz, n):
    has_past = kv_past is not None
    nh = N_HEADS_C
    t = bsz * n
    if has_past:
        tq, tkd = min(n, TQ_SPARSE_PAST), n
        p_len = kv_past.shape[1]
        assert p_len % TK == 0 and n <= TK
    else:
        tq, tkd = min(n, TQ_SPARSE), TK
        p_len = 0
        assert n % TK == 0
    q_offsets = list(range(0, tkd, tq))
    nq = n // tq
    npar = len(q_offsets)
    far = _bias_far(table, nh).reshape(nh, 1, 1)
    bd = _bias_tiles(table, q_offsets, 0, tq, tkd, nh)
    bp = _bias_tiles(table, q_offsets, -TK, tq, TK, nh)
    in_specs = [pl.BlockSpec((nh, 1, 1), lambda b, i: (0, 0, 0)),
                pl.BlockSpec((nh, tq, KV_LORA), lambda b, i: (0, b * nq + i, 0)),
                pl.BlockSpec((n, KV_LORA), lambda b, i: (b, 0)),
                pl.BlockSpec((tq, n), lambda b, i: (b * nq + i, 0))]
    args = [far, q_lat, kv_lat, sel_new]
    if has_past:
        in_specs += [pl.BlockSpec((1, p_len, KV_LORA), lambda b, i: (b, 0, 0)),
                     pl.BlockSpec((tq, p_len), lambda b, i: (b * nq + i, 0))]
        args += [kv_past, sel_past]
    in_specs += [pl.BlockSpec((1, nh, tq, tkd), lambda b, i: (i % npar, 0, 0, 0)),
                 pl.BlockSpec((1, nh, tq, TK), lambda b, i: (i % npar, 0, 0, 0))]
    args += [bd, bp]
    return pl.pallas_call(
        functools.partial(_sparse_attn_kernel, tq=tq, tkd=tkd, n=n, p_len=p_len),
        grid=(bsz, nq), in_specs=in_specs,
        out_specs=pl.BlockSpec((nh, tq, KV_LORA), lambda b, i: (0, b * nq + i, 0)),
        out_shape=jax.ShapeDtypeStruct((nh, t, KV_LORA), MXU),
        compiler_params=_cparams(("arbitrary", "arbitrary")),
        name="sparse_attn_past" if has_past else "sparse_attn",
    )(*args)


def _tile(n, cap, mult):
    best = None
    for t in range(mult, min(n, cap) + 1, mult):
        if n % t == 0:
            best = t
    assert best is not None, (n, cap, mult)
    return best


def _per_chunk(v, reps):
    return jnp.repeat(v, reps, axis=0)[:, None, :]


def _trunk(x, mod, past, p, w):
    bsz, n, d = x.shape
    t = bsz * n
    x = x.reshape(t, d)
    tm = _tile(t, 512, CHUNK)
    tmn = _tile(t, 256, CHUNK)
    tn_d = _tile(d, 1024, LANES)
    depth = p['w_ada'].shape[0]
    p_len = 0 if past is None else past[0].shape[2]
    top_k = min(TOPK_MAX, (p_len + n) // 4)
    even_rows, odd_rows = [], []
    for l in range(depth):
        m6 = mod[l].reshape(bsz, 6, d)
        sh1, sc1, g1, sh2, sc2, g2 = (_per_chunk(m6[:, k], n // CHUNK) for k in range(6))
        h = _norm(x, p['norm_mix_g'][l], tm=tmn, sc=sc1, sh=sh1, out_dtype=MXU)
        if l % 2 == 0:
            i = l // 2
            lam_init = 0.8 - 0.6 * math.exp(-0.3 * l)
            qkv = _mm(h, p['w_in_even'][i], n_out=48 * LANES, tm=tm, tn=1024, seg_out=True, name="mm_qkv")
            past_i = None if past is None else tuple(a[i] for a in past[:4])
            lam = (jnp.exp(jnp.sum(p['lam_q1'][i].astype(F32) * p['lam_k1'][i].astype(F32)))
                   - jnp.exp(jnp.sum(p['lam_q2'][i].astype(F32) * p['lam_k2'][i].astype(F32)))
                   + lam_init)
            if past is None:
                o = _even_attn(qkv, p['rel_bias_table'], lam, p['subln_g'][i], lam_init, bsz=bsz, n=n)
            else:
                o = _even_attn_past(qkv, past_i, p['rel_bias_table'], lam, p['subln_g'][i], lam_init,
                                    bsz=bsz, n=n)
            even_rows.append((qkv[1].reshape(bsz, n, N_MAPS_A, DA_QK),
                              qkv[2].reshape(bsz, n, N_HEADS_A, HEAD_DIM),
                              qkv[4].reshape(bsz, n, N_HEADS_B, HEAD_DIM),
                              qkv[5].reshape(bsz, n, N_HEADS_B, HEAD_DIM)))
            x = _mm(o, p['w_out_even'][i], n_out=d, tm=tm, tn=tn_d, gate=g1, res=x)
        else:
            j = l // 2
            proj = _mm(h, w['w_in_odd'][j], n_out=10 * LANES, tm=tm, tn=10 * LANES)
            c_q = _norm(proj, p['g_q'][j], tm=tmn, width=Q_LORA, col_block=0, out_dtype=MXU)
            kv_lat = _norm(proj, p['g_kv'][j], tm=tmn, width=KV_LORA, col_block=1)
            k_i = proj[:, 1024:1152]
            q = _mm(c_q, p['w_uq'][j], n_out=N_HEADS_C * HEAD_DIM, tm=tm, tn=1024, out_dtype=MXU)
            q_lat = _mm_heads_out(q, w['w_uk_t'][j], tm=_tile(t, TM_HEADS, CHUNK))
            q_i = _mm(c_q, p['w_qidx'][j], n_out=N_IDX_HEADS * D_IDX, tm=tm, tn=1024, out_dtype=MXU)
            kv_past = None if past is None else past[4][j]
            ki_past = None if past is None else past[5][j]
            sel_new, sel_past = _indexer(q_i, proj, ki_past, bsz=bsz, n=n, top_k=top_k)
            o_lat = _sparse_attn(q_lat, kv_lat, sel_new, kv_past, sel_past, p['rel_bias_table'],
                                 bsz=bsz, n=n)
            o = _mm_heads_in(o_lat, w['w_uv_h'][j], tm=_tile(t, TM_HEADS, CHUNK))
            odd_rows.append((kv_lat.reshape(bsz, n, KV_LORA), k_i.reshape(bsz, n, D_IDX)))
            x = _mm(o, p['w_out_odd'][j], n_out=d, tm=tm, tn=tn_d, gate=g1, res=x)
        h2, route = _norm_router(x, p['norm_ffn_g'][l], sc2, sh2, w['w_router'], p['router_bias'], tm=tmn)
        x = _moe(h2, route, w['w_gate'], w['w_up'], w['w_down'], l, g2, x, tm=tmn)
    y = _norm(x, p['final_norm_g'], tm=tmn).reshape(bsz, n, d)
    ev = tuple(jnp.stack([r[m] for r in even_rows]) for m in range(4))
    od = tuple(jnp.stack([r[m] for r in odd_rows]) for m in range(2))
    return y, ev, od


def kernel(x_prompt, x_sample, cache_a_k, cache_a_v, cache_b_k, cache_b_v, cache_c_kv, cache_c_idx,
           c_prompt, c_sample, rel_bias_table, norm_mix_g, norm_ffn_g, final_norm_g, w_ada, b_ada,
           w_in_even, lam_q1, lam_k1, lam_q2, lam_k2, subln_g, w_out_even, w_in_odd, g_q, g_kv,
           w_uq, w_qidx, w_uk, w_uv, w_out_odd, w_router, router_bias, w_gate, w_up, w_down):
    p = dict(rel_bias_table=rel_bias_table, norm_mix_g=norm_mix_g, norm_ffn_g=norm_ffn_g,
             final_norm_g=final_norm_g, w_ada=w_ada, b_ada=b_ada, w_in_even=w_in_even,
             lam_q1=lam_q1, lam_k1=lam_k1, lam_q2=lam_q2, lam_k2=lam_k2, subln_g=subln_g,
             w_out_even=w_out_even, g_q=g_q, g_kv=g_kv, w_uq=w_uq, w_qidx=w_qidx,
             w_out_odd=w_out_odd, router_bias=router_bias)
    d = x_prompt.shape[-1]
    depth = w_ada.shape[0]
    w = dict(
        w_in_odd=jnp.pad(w_in_odd, ((0, 0), (0, 0), (0, 10 * LANES - w_in_odd.shape[-1]))),
        w_uk_t=jnp.transpose(w_uk, (0, 2, 3, 1)),
        w_uv_h=jnp.transpose(w_uv, (0, 2, 1, 3)),
        w_router=jnp.pad(w_router, ((0, 0), (0, LANES - N_EXPERTS))),
        w_gate=w_gate.astype(MXU), w_up=w_up.astype(MXU), w_down=w_down.astype(MXU))
    nb_p, nb_s = c_prompt.shape[0], c_sample.shape[0]
    c_all = jnp.concatenate([c_prompt, c_sample], axis=0)
    rows = -(-(nb_p + nb_s) // 16) * 16
    c_act = jnp.pad(c_all * (1.0 / (1.0 + jnp.exp(-c_all))), ((0, rows - nb_p - nb_s), (0, 0)))
    tn_ada = _tile(6 * d, 1024, LANES)
    mods = [_mm(c_act, w_ada, layer=l, n_out=6 * d, tm=rows, tn=tn_ada, bias=b_ada[l], name="mm_ada")
            for l in range(depth)]
    mod_p = [m[:nb_p] for m in mods]
    mod_s = [m[nb_p:nb_p + nb_s] for m in mods]

    y_prompt, ev_p, od_p = _trunk(x_prompt, mod_p, None, p, w)
    past = (cache_a_k, cache_a_v, cache_b_k, cache_b_v, cache_c_kv, cache_c_idx)
    y_sample, ev_s, od_s = _trunk(x_sample, mod_s, past, p, w)
    return (y_prompt, y_sample) + ev_p + od_p + ev_s + od_s
```
